```python
import jax, jax.numpy as jnp
from jax import lax
import numpy as np

D_MODEL = 4096
BATCH = 2
SEQ = 4096
DEPTH = 1

HEAD_DIM = 128
DIL_CONFIGS = ((128, 1), (512, 4), (2048, 16))
N_DIL_GROUPS = 3
DIL_HEADS = 8
DIL_WIDTH = N_DIL_GROUPS * DIL_HEADS * HEAD_DIM
DIL_OUT_WIDTH = DIL_HEADS * HEAD_DIM
DIFF_HEADS = 8
DIFF_QK_WIDTH = DIFF_HEADS * 2 * HEAD_DIM
DIFF_V_DIM = 2 * HEAD_DIM
DIFF_V_WIDTH = DIFF_HEADS * DIFF_V_DIM
MEM_LEN = 256
MEM_HEADS = 4
MEM_WIDTH = MEM_HEADS * HEAD_DIM
D_FF = 4 * D_MODEL
ROPE_THETA = 10000.0
Q_BLOCK = 128
NORM_EPS = 1e-6
MASK_VALUE = -1e30
IN_SPLITS = (DIL_WIDTH, DIL_WIDTH, DIL_WIDTH,
             DIFF_QK_WIDTH, DIFF_QK_WIDTH, DIFF_V_WIDTH,
             D_MODEL, D_MODEL)
D_IN = DIL_WIDTH * 3 + DIFF_QK_WIDTH * 2 + DIFF_V_WIDTH + 2 * D_MODEL

kernel_name = "hybrid_dilated_diffattn_gated_encoder"


def _split_points():
    pts, acc = [], 0
    for s in IN_SPLITS[:-1]:
        acc += s
        pts.append(acc)
    return pts


def rms_norm(x, g):
    xf = x.astype(jnp.float32)
    y = xf * lax.rsqrt(jnp.mean(xf * xf, axis=-1, keepdims=True) + NORM_EPS)
    return (y * g.astype(jnp.float32)).astype(x.dtype)


def rope(x, positions):
    d = x.shape[-1]
    inv_freq = ROPE_THETA ** (-jnp.arange(0, d, 2, dtype=jnp.float32) / d)
    ang = positions.astype(jnp.float32)[..., None] * inv_freq
    ang = ang.reshape(ang.shape[:2] + (1,) * (x.ndim - 3) + (d // 2,))
    cos, sin = jnp.cos(ang), jnp.sin(ang)
    xf = x.astype(jnp.float32)
    x1, x2 = xf[..., : d // 2], xf[..., d // 2:]
    out = jnp.concatenate([x1 * cos - x2 * sin, x2 * cos + x1 * sin], axis=-1)
    return out.astype(x.dtype)


def dilated_attention(q, k, v):
    b, s, g_n, h_n, d = q.shape
    nb = s // Q_BLOCK
    scale = HEAD_DIM ** -0.5
    q_blocks = q.reshape(b, nb, Q_BLOCK, g_n, h_n, d).transpose(1, 0, 2, 3, 4, 5)

    def one_block(args):
        q_blk, blk = args
        qpos = blk * Q_BLOCK + jnp.arange(Q_BLOCK)
        outs, lses = [], []
        for g, (window, dil) in enumerate(DIL_CONFIGS):
            n_side = window // (2 * dil)
            offsets = jnp.arange(-n_side, n_side + 1) * dil
            idx = qpos[:, None] + offsets[None, :]
            valid = (idx >= 0) & (idx < s)
            idx = jnp.clip(idx, 0, s - 1)
            k_sel = k[:, :, g][:, idx]
            v_sel = v[:, :, g][:, idx]
            sc = jnp.einsum('bqhd,bqjhd->bhqj', q_blk[:, :, g], k_sel,
                            preferred_element_type=jnp.float32) * scale
            sc = jnp.where(valid[None, None], sc, MASK_VALUE)
            m = jnp.max(sc, axis=-1, keepdims=True)
            p = jnp.exp(sc - m)
            l = jnp.sum(p, axis=-1, keepdims=True)
            o = jnp.einsum('bhqj,bqjhd->bqhd', p, v_sel.astype(jnp.float32))
            o = o / l.transpose(0, 2, 1, 3)
            outs.append(o)
            lses.append((m + jnp.log(l))[..., 0].transpose(0, 2, 1))
        o_all = jnp.stack(outs, axis=0)
        alpha = jax.nn.softmax(jnp.stack(lses, axis=0), axis=0)[..., None]
        return jnp.sum(alpha * o_all, axis=0).astype(q.dtype)

    out = lax.map(one_block, (q_blocks, jnp.arange(nb)))
    return out.transpose(1, 0, 2, 3, 4).reshape(b, s, h_n * d)


def differential_attention(q, k, v, lam, lam_init, subln):
    b, s, h_n, _, d = q.shape
    nb = s // Q_BLOCK
    scale = HEAD_DIM ** -0.5
    q_blocks = q.reshape(b, nb, Q_BLOCK, h_n, 2, d).transpose(1, 0, 2, 3, 4, 5)
    v32 = v.astype(jnp.float32)

    def one_block(q_blk):
        sc = jnp.einsum('bqhcd,bkhcd->bchqk', q_blk, k,
                        preferred_element_type=jnp.float32) * scale
        a = jax.nn.softmax(sc, axis=-1)
        attn = a[:, 0] - lam * a[:, 1]
        o = jnp.einsum('bhqk,bkhe->bqhe', attn, v32)
        o = rms_norm(o, subln) * (1.0 - lam_init)
        return o.astype(q.dtype)

    out = lax.map(one_block, q_blocks)
    return out.transpose(1, 0, 2, 3, 4).reshape(b, s, h_n * DIFF_V_DIM)


def memory_cross_attention(h, m, w_q, w_kv, w_o):
    b, s, _ = h.shape
    q = (h @ w_q).reshape(b, s, MEM_HEADS, HEAD_DIM)
    kv = m @ w_kv
    k, v = jnp.split(kv, 2, axis=-1)
    k = k.reshape(b, m.shape[1], MEM_HEADS, HEAD_DIM)
    v = v.reshape(b, m.shape[1], MEM_HEADS, HEAD_DIM)
    sc = jnp.einsum('bshd,bmhd->bhsm', q, k,
                    preferred_element_type=jnp.float32) * HEAD_DIM ** -0.5
    a = jax.nn.softmax(sc, axis=-1)
    o = jnp.einsum('bhsm,bmhd->bshd', a, v.astype(jnp.float32)).astype(h.dtype)
    return o.reshape(b, s, MEM_WIDTH) @ w_o


def setup_inputs(seed: int = 0) -> dict:
    key = jax.random.key(seed)
    ks = jax.random.split(key, 24)
    f32 = jnp.float32

    def dense(k, fan_in, fan_out):
        return jax.random.normal(k, (DEPTH, fan_in, fan_out), f32) * fan_in ** -0.5

    def gain(k, n):
        return 1.0 + 0.05 * jax.random.normal(k, (DEPTH, n), f32)

    return {
        "x": jax.random.normal(ks[0], (BATCH, SEQ, D_MODEL), f32),
        "mem": jax.random.normal(ks[1], (BATCH, MEM_LEN, D_MODEL), f32),
        "positions": (jnp.arange(SEQ, dtype=jnp.int32)[None, :]
                      + jax.random.randint(ks[2], (BATCH, 1), 0, 1024, dtype=jnp.int32)),
        "norm_mix_pre": gain(ks[3], D_MODEL),
        "w_in": dense(ks[4], D_MODEL, D_IN),
        "w_a": dense(ks[5], DIL_OUT_WIDTH, D_MODEL),
        "w_b": dense(ks[6], DIFF_V_WIDTH, D_MODEL),
        "w_mix_out": dense(ks[7], D_MODEL, D_MODEL),
        "norm_mix_post": gain(ks[8], D_MODEL),
        "lambda_q1": 0.1 * jax.random.normal(ks[9], (DEPTH, HEAD_DIM), f32),
        "lambda_k1": 0.1 * jax.random.normal(ks[10], (DEPTH, HEAD_DIM), f32),
        "lambda_q2": 0.1 * jax.random.normal(ks[11], (DEPTH, HEAD_DIM), f32),
        "lambda_k2": 0.1 * jax.random.normal(ks[12], (DEPTH, HEAD_DIM), f32),
        "diff_subln": gain(ks[13], DIFF_V_DIM),
        "norm_mem_pre": gain(ks[14], D_MODEL),
        "norm_mem_kv": gain(ks[15], D_MODEL),
        "w_mem_q": dense(ks[16], D_MODEL, MEM_WIDTH),
        "w_mem_kv": dense(ks[17], D_MODEL, 2 * MEM_WIDTH),
        "w_mem_o": dense(ks[18], MEM_WIDTH, D_MODEL),
        "norm_mem_post": gain(ks[19], D_MODEL),
        "norm_mlp_pre": gain(ks[20], D_MODEL),
        "w_mlp_up": dense(ks[21], D_MODEL, D_FF),
        "w_mlp_down": dense(ks[22], D_FF, D_MODEL),
        "norm_mlp_post": gain(ks[23], D_MODEL),
    }


def reference(x, mem, positions, norm_mix_pre, w_in, w_a, w_b, w_mix_out, norm_mix_post,
              lambda_q1, lambda_k1, lambda_q2, lambda_k2, diff_subln,
              norm_mem_pre, norm_mem_kv, w_mem_q, w_mem_kv, w_mem_o, norm_mem_post,
              norm_mlp_pre, w_mlp_up, w_mlp_down, norm_mlp_post):
    b, s, _ = x.shape
    for layer in range(DEPTH):
        h = rms_norm(x, norm_mix_pre[layer])
        proj = h @ w_in[layer]
        qa, ka, va, qb, kb, vb, ga, gb = jnp.split(proj, _split_points(), axis=-1)

        qa = rope(qa.reshape(b, s, N_DIL_GROUPS, DIL_HEADS, HEAD_DIM), positions)
        ka = rope(ka.reshape(b, s, N_DIL_GROUPS, DIL_HEADS, HEAD_DIM), positions)
        va = va.reshape(b, s, N_DIL_GROUPS, DIL_HEADS, HEAD_DIM)
        out_a = dilated_attention(qa, ka, va)

        qb = rope(qb.reshape(b, s, DIFF_HEADS, 2, HEAD_DIM), positions)
        kb = rope(kb.reshape(b, s, DIFF_HEADS, 2, HEAD_DIM), positions)
        vb = vb.reshape(b, s, DIFF_HEADS, DIFF_V_DIM)
        lam_init = 0.8 - 0.6 * float(np.exp(-0.3 * layer))
        lam = (jnp.exp(jnp.sum(lambda_q1[layer].astype(jnp.float32) * lambda_k1[layer].astype(jnp.float32)))
               - jnp.exp(jnp.sum(lambda_q2[layer].astype(jnp.float32) * lambda_k2[layer].astype(jnp.float32)))
               + lam_init)
        out_b = differential_attention(qb, kb, vb, lam, lam_init, diff_subln[layer])

        merged = (jax.nn.sigmoid(ga) * (out_a @ w_a[layer])
                  + jax.nn.sigmoid(gb) * (out_b @ w_b[layer]))
        x = x + rms_norm(merged @ w_mix_out[layer], norm_mix_post[layer])

        h = rms_norm(x, norm_mem_pre[layer])
        m = rms_norm(mem, norm_mem_kv[layer])
        y = memory_cross_attention(h, m, w_mem_q[layer], w_mem_kv[layer], w_mem_o[layer])
        x = x + rms_norm(y, norm_mem_post[layer])

        h = rms_norm(x, norm_mlp_pre[layer])
        u = jnp.square(jax.nn.relu(h @ w_mlp_up[layer]))
        x = x + rms_norm(u @ w_mlp_down[layer], norm_mlp_post[layer])
    return x
```

```python
import functools

import numpy as np
import jax
import jax.numpy as jnp
from jax import lax
from jax.experimental import pallas as pl
from jax.experimental.pallas import tpu as pltpu

F32 = jnp.float32
BF16 = jnp.bfloat16

HEAD_DIM = 128
DIL_CONFIGS = ((128, 1), (512, 4), (2048, 16))
N_DIL_GROUPS = 3
DIL_HEADS = 8
DIL_GROUP_WIDTH = DIL_HEADS * HEAD_DIM
DIL_WIDTH = N_DIL_GROUPS * DIL_GROUP_WIDTH
DIFF_HEADS = 8
DIFF_QK_WIDTH = DIFF_HEADS * 2 * HEAD_DIM
DIFF_V_DIM = 2 * HEAD_DIM
DIFF_V_WIDTH = DIFF_HEADS * DIFF_V_DIM
MEM_HEADS = 4
MEM_WIDTH = MEM_HEADS * HEAD_DIM
ROPE_THETA = 10000.0
Q_BLOCK = 128
NORM_EPS = 1e-6
MASK_VALUE = -1e30
SCALE = HEAD_DIM ** -0.5

OFF_QA = 0
OFF_KA = OFF_QA + DIL_WIDTH
OFF_VA = OFF_KA + DIL_WIDTH
OFF_QB = OFF_VA + DIL_WIDTH
OFF_KB = OFF_QB + DIFF_QK_WIDTH
OFF_VB = OFF_KB + DIFF_QK_WIDTH
OFF_GA = OFF_VB + DIFF_V_WIDTH

V7X_VMEM_BYTES = 64 * 1024 * 1024
VMEM_LIMIT = V7X_VMEM_BYTES - 6 * 1024 * 1024


def _params(*sem):
    return pltpu.CompilerParams(dimension_semantics=sem, vmem_limit_bytes=VMEM_LIMIT)


def _rms(x, g):
    return x * lax.rsqrt(jnp.mean(x * x, axis=-1, keepdims=True) + NORM_EPS) * g


def _rmsnorm_kernel(x_ref, g_ref, o_ref):
    o_ref[...] = _rms(x_ref[...], g_ref[...]).astype(o_ref.dtype)


def rmsnorm(x, g, tr=256):
    t, d = x.shape
    return pl.pallas_call(
        _rmsnorm_kernel,
        grid=(t // tr,),
        in_specs=[pl.BlockSpec((tr, d), lambda i: (i, 0)),
                  pl.BlockSpec((1, d), lambda i: (0, 0))],
        out_specs=pl.BlockSpec((tr, d), lambda i: (i, 0)),
        out_shape=jax.ShapeDtypeStruct((t, d), BF16),
        compiler_params=_params("parallel"),
        name="rmsnorm",
    )(x, g.reshape(1, d))


def _rope_table_kernel(pos_ref, inv_ref, sign_ref, cos_ref, sin_ref):
    ang = pos_ref[...] * inv_ref[...]
    cos_ref[...] = jnp.cos(ang)
    sin_ref[...] = jnp.sin(ang) * sign_ref[...]


def rope_tables(positions, tr=1024):
    t = positions.size
    half = HEAD_DIM // 2
    inv = ROPE_THETA ** (-jnp.arange(0, HEAD_DIM, 2, dtype=F32) / HEAD_DIM)
    inv2 = jnp.concatenate([inv, inv]).reshape(1, HEAD_DIM)
    sign = jnp.concatenate([-jnp.ones((half,), F32), jnp.ones((half,), F32)]).reshape(1, HEAD_DIM)
    pos = positions.astype(F32).reshape(t, 1)
    return pl.pallas_call(
        _rope_table_kernel,
        grid=(t // tr,),
        in_specs=[pl.BlockSpec((tr, 1), lambda i: (i, 0)),
                  pl.BlockSpec((1, HEAD_DIM), lambda i: (0, 0)),
                  pl.BlockSpec((1, HEAD_DIM), lambda i: (0, 0))],
        out_specs=[pl.BlockSpec((tr, HEAD_DIM), lambda i: (i, 0))] * 2,
        out_shape=[jax.ShapeDtypeStruct((t, HEAD_DIM), F32)] * 2,
        compiler_params=_params("parallel"),
        name="rope_tables",
    )(pos, inv2, sign)


def _inproj_kernel(h_ref, w_ref, cos_ref, sin_ref, o_ref, *, tn):
    j = pl.program_id(1)
    acc = jnp.dot(h_ref[...], w_ref[...], preferred_element_type=F32)
    in_a = j < OFF_VA // tn
    in_b = (j >= OFF_QB // tn) & (j < OFF_VB // tn)
    is_rope = in_a | in_b
    is_q = (j < OFF_KA // tn) | ((j >= OFF_QB // tn) & (j < OFF_KB // tn))

    @pl.when(is_rope)
    def _():
        f = jnp.where(is_q, SCALE, 1.0).astype(F32)
        c = cos_ref[...] * f
        s = sin_ref[...] * f
        for hh in range(tn // HEAD_DIM):
            sl = slice(hh * HEAD_DIM, (hh + 1) * HEAD_DIM)
            xh = acc[:, sl]
            o_ref[:, sl] = (xh * c + pltpu.roll(xh, HEAD_DIM // 2, 1) * s).astype(o_ref.dtype)

    @pl.when(jnp.logical_not(is_rope))
    def _():
        o_ref[...] = acc.astype(o_ref.dtype)


def in_projection(h, w, cos, sin, tm=1024, tn=1024):
    t, k = h.shape
    n = w.shape[1]
    return pl.pallas_call(
        functools.partial(_inproj_kernel, tn=tn),
        grid=(t // tm, n // tn),
        in_specs=[pl.BlockSpec((tm, k), lambda i, j: (i, 0)),
                  pl.BlockSpec((k, tn), lambda i, j: (0, j)),
                  pl.BlockSpec((tm, HEAD_DIM), lambda i, j: (i, 0)),
                  pl.BlockSpec((tm, HEAD_DIM), lambda i, j: (i, 0))],
        out_specs=pl.BlockSpec((tm, tn), lambda i, j: (i, j)),
        out_shape=jax.ShapeDtypeStruct((t, n), BF16),
        compiler_params=_params("parallel", "arbitrary"),
        name="in_projection",
    )(h, w, cos, sin)


def _mm_kernel(a_ref, w_ref, o_ref, *, epilogue):
    acc = jnp.dot(a_ref[...], w_ref[...], preferred_element_type=F32)
    if epilogue == "relu2":
        acc = jnp.square(jnp.maximum(acc, 0.0))
    elif epilogue == "scale":
        acc = acc * SCALE
    o_ref[...] = acc.astype(o_ref.dtype)


def matmul(a, w, out_dtype, tm, tn, epilogue=None, name="matmul"):
    m, k = a.shape
    n = w.shape[1]
    return pl.pallas_call(
        functools.partial(_mm_kernel, epilogue=epilogue),
        grid=(m // tm, n // tn),
        in_specs=[pl.BlockSpec((tm, k), lambda i, j: (i, 0)),
                  pl.BlockSpec((k, tn), lambda i, j: (0, j))],
        out_specs=pl.BlockSpec((tm, tn), lambda i, j: (i, j)),
        out_shape=jax.ShapeDtypeStruct((m, n), out_dtype),
        compiler_params=_params("parallel", "arbitrary"),
        name=name,
    )(a, w)


def _mm_kgrid_kernel(a_ref, w_ref, o_ref):
    @pl.when(pl.program_id(2) == 0)
    def _():
        o_ref[...] = jnp.zeros_like(o_ref)

    o_ref[...] += jnp.dot(a_ref[...], w_ref[...], preferred_element_type=F32)


def matmul_kgrid(a, w, tm, tn, tk, name="matmul_kgrid"):
    m, k = a.shape
    n = w.shape[1]
    return pl.pallas_call(
        _mm_kgrid_kernel,
        grid=(m // tm, n // tn, k // tk),
        in_specs=[pl.BlockSpec((tm, tk), lambda i, j, kk: (i, kk)),
                  pl.BlockSpec((tk, tn), lambda i, j, kk: (kk, j))],
        out_specs=pl.BlockSpec((tm, tn), lambda i, j, kk: (i, j)),
        out_shape=jax.ShapeDtypeStruct((m, n), F32),
        compiler_params=_params("parallel", "parallel", "arbitrary"),
        name=name,
    )(a, w)


def _dilated_kernel(q_ref, kp_ref, kc_ref, kn_ref, vp_ref, vc_ref, vn_ref,
                    o_ref, lse_ref, *, class_len):
    i = pl.program_id(2)
    half = Q_BLOCK // 2
    nk = 2 * Q_BLOCK
    row = lax.broadcasted_iota(jnp.int32, (Q_BLOCK, nk), 0)
    col = lax.broadcasted_iota(jnp.int32, (Q_BLOCK, nk), 1)
    rel = col - half - row
    kpos = i * Q_BLOCK - half + col
    valid = (rel >= -half) & (rel <= half) & (kpos >= 0) & (kpos < class_len)
    for hh in range(DIL_HEADS):
        sl = slice(hh * HEAD_DIM, (hh + 1) * HEAD_DIM)
        q = q_ref[0, :, sl]
        k = jnp.concatenate([kp_ref[0, :, sl], kc_ref[0, :, sl], kn_ref[0, :, sl]], axis=0)
        v = jnp.concatenate([vp_ref[0, :, sl], vc_ref[0, :, sl], vn_ref[0, :, sl]], axis=0)
        s = lax.dot_general(q, k, (((1,), (1,)), ((), ())), preferred_element_type=F32)
        s = jnp.where(valid, s, MASK_VALUE)
        m = jnp.max(s, axis=-1, keepdims=True)
        p = jnp.exp(s - m)
        l = jnp.sum(p, axis=-1, keepdims=True)
        o = jnp.dot(p.astype(BF16), v, preferred_element_type=F32) / l
        o_ref[0, :, sl] = o
        lse_ref[0, :, sl] = jnp.broadcast_to(m + jnp.log(l), (Q_BLOCK, HEAD_DIM))


def dilated_group(proj, g, dil):
    b, s, d_in = proj.shape
    class_len = s // dil
    half = Q_BLOCK // 2
    n_half_blocks = class_len // half
    pv = proj.reshape(b, class_len, dil * d_in)
    ncb = d_in // DIL_GROUP_WIDTH
    cq = OFF_QA // DIL_GROUP_WIDTH + g
    ck = OFF_KA // DIL_GROUP_WIDTH + g
    cv = OFF_VA // DIL_GROUP_WIDTH + g

    def cur(c):
        return pl.BlockSpec((1, Q_BLOCK, DIL_GROUP_WIDTH), lambda bb, r, i: (bb, i, r * ncb + c))

    def prev(c):
        return pl.BlockSpec((1, half, DIL_GROUP_WIDTH),
                            lambda bb, r, i: (bb, jnp.maximum(2 * i - 1, 0), r * ncb + c))

    def nxt(c):
        return pl.BlockSpec((1, half, DIL_GROUP_WIDTH),
                            lambda bb, r, i: (bb, jnp.minimum(2 * i + 2, n_half_blocks - 1), r * ncb + c))

    out_spec = pl.BlockSpec((1, Q_BLOCK, DIL_GROUP_WIDTH), lambda bb, r, i: (bb, i, r))
    out_sds = jax.ShapeDtypeStruct((b, class_len, dil * DIL_GROUP_WIDTH), F32)
    o, lse = pl.pallas_call(
        functools.partial(_dilated_kernel, class_len=class_len),
        grid=(b, dil, class_len // Q_BLOCK),
        in_specs=[cur(cq), prev(ck), cur(ck), nxt(ck), prev(cv), cur(cv), nxt(cv)],
        out_specs=[out_spec, out_spec],
        out_shape=[out_sds, out_sds],
        compiler_params=_params("parallel", "parallel", "arbitrary"),
        name=f"dilated_attention_d{dil}",
    )(pv, pv, pv, pv, pv, pv, pv)
    return o.reshape(b * s, DIL_GROUP_WIDTH), lse.reshape(b * s, DIL_GROUP_WIDTH)


def _dil_merge_kernel(o0, o1, o2, l0, l1, l2, out_ref):
    a0, a1, a2 = l0[...], l1[...], l2[...]
    m = jnp.maximum(jnp.maximum(a0, a1), a2)
    e0, e1, e2 = jnp.exp(a0 - m), jnp.exp(a1 - m), jnp.exp(a2 - m)
    num = e0 * o0[...] + e1 * o1[...] + e2 * o2[...]
    out_ref[...] = (num / (e0 + e1 + e2)).astype(out_ref.dtype)


def dilated_merge(os_, lses, tr=512):
    t, w = os_[0].shape
    spec = pl.BlockSpec((tr, w), lambda i: (i, 0))
    return pl.pallas_call(
        _dil_merge_kernel,
        grid=(t // tr,),
        in_specs=[spec] * 6,
        out_specs=spec,
        out_shape=jax.ShapeDtypeStruct((t, w), BF16),
        compiler_params=_params("parallel"),
        name="dilated_merge",
    )(*os_, *lses)


def _diff_kernel(q_ref, k_ref, v_ref, lq1, lk1, lq2, lk2, g_ref, o_ref, *, lam_init):
    lam = (jnp.exp(jnp.sum(lq1[...] * lk1[...], axis=-1, keepdims=True))
           - jnp.exp(jnp.sum(lq2[...] * lk2[...], axis=-1, keepdims=True)) + lam_init)

    def probs(c):
        sl = slice(c * HEAD_DIM, (c + 1) * HEAD_DIM)
        s = lax.dot_general(q_ref[0, :, sl], k_ref[0, :, sl], (((1,), (1,)), ((), ())),
                            preferred_element_type=F32)
        p = jnp.exp(s - jnp.max(s, axis=-1, keepdims=True))
        return p, jnp.sum(p, axis=-1, keepdims=True)

    p1, l1 = probs(0)
    p2, l2 = probs(1)
    attn = p1 * (1.0 / l1) - p2 * (lam / l2)
    o = jnp.dot(attn.astype(BF16), v_ref[0], preferred_element_type=F32)
    o_ref[0] = (_rms(o, g_ref[...]) * (1.0 - lam_init)).astype(o_ref.dtype)


def differential_attention(proj, lq1, lk1, lq2, lk2, subln, lam_init, tq=256):
    b, s, _ = proj.shape
    vec = pl.BlockSpec((1, HEAD_DIM), lambda bb, h, i: (0, 0))
    cq, ck, cv = OFF_QB // DIFF_V_DIM, OFF_KB // DIFF_V_DIM, OFF_VB // DIFF_V_DIM
    return pl.pallas_call(
        functools.partial(_diff_kernel, lam_init=lam_init),
        grid=(b, DIFF_HEADS, s // tq),
        in_specs=[pl.BlockSpec((1, tq, DIFF_V_DIM), lambda bb, h, i: (bb, i, cq + h)),
                  pl.BlockSpec((1, s, DIFF_V_DIM), lambda bb, h, i: (bb, 0, ck + h)),
                  pl.BlockSpec((1, s, DIFF_V_DIM), lambda bb, h, i: (bb, 0, cv + h)),
                  vec, vec, vec, vec,
                  pl.BlockSpec((1, DIFF_V_DIM), lambda bb, h, i: (0, 0))],
        out_specs=pl.BlockSpec((1, tq, DIFF_V_DIM), lambda bb, h, i: (bb, i, h)),
        out_shape=jax.ShapeDtypeStruct((b, s, DIFF_V_WIDTH), BF16),
        compiler_params=_params("parallel", "parallel", "arbitrary"),
        name="differential_attention",
    )(proj, proj, proj, lq1.reshape(1, -1), lk1.reshape(1, -1), lq2.reshape(1, -1),
      lk2.reshape(1, -1), subln.reshape(1, -1))


def _gate_kernel(oa_ref, wa_ref, ob_ref, wb_ref, ga_ref, gb_ref, o_ref):
    ya = jnp.dot(oa_ref[...], wa_ref[...], preferred_element_type=F32)
    yb = jnp.dot(ob_ref[...], wb_ref[...], preferred_element_type=F32)
    o = (jax.nn.sigmoid(ga_ref[...].astype(F32)) * ya
         + jax.nn.sigmoid(gb_ref[...].astype(F32)) * yb)
    o_ref[...] = o.astype(o_ref.dtype)


def gated_merge(out_a, w_a, out_b, w_b, proj2d, tm=1024, tn=512):
    t = out_a.shape[0]
    n = w_a.shape[1]
    ca = OFF_GA // tn
    cb = (OFF_GA + n) // tn
    return pl.pallas_call(
        _gate_kernel,
        grid=(t // tm, n // tn),
        in_specs=[pl.BlockSpec((tm, out_a.shape[1]), lambda i, j: (i, 0)),
                  pl.BlockSpec((w_a.shape[0], tn), lambda i, j: (0, j)),
                  pl.BlockSpec((tm, out_b.shape[1]), lambda i, j: (i, 0)),
                  pl.BlockSpec((w_b.shape[0], tn), lambda i, j: (0, j)),
                  pl.BlockSpec((tm, tn), lambda i, j: (i, ca + j)),
                  pl.BlockSpec((tm, tn), lambda i, j: (i, cb + j))],
        out_specs=pl.BlockSpec((tm, tn), lambda i, j: (i, j)),
        out_shape=jax.ShapeDtypeStruct((t, n), BF16),
        compiler_params=_params("parallel", "arbitrary"),
        name="gated_merge",
    )(out_a, w_a, out_b, w_b, proj2d, proj2d)


def _resnorm_kernel(y_ref, x_ref, gp_ref, gn_ref, xo_ref, ho_ref):
    xn = x_ref[...] + _rms(y_ref[...], gp_ref[...])
    xo_ref[...] = xn
    ho_ref[...] = _rms(xn, gn_ref[...]).astype(ho_ref.dtype)


def _resnorm_last_kernel(y_ref, x_ref, gp_ref, xo_ref):
    xo_ref[...] = x_ref[...] + _rms(y_ref[...], gp_ref[...])


def residual_norm(y, x, g_post, g_next=None, tr=256):
    t, d = x.shape
    row = pl.BlockSpec((tr, d), lambda i: (i, 0))
    vec = pl.BlockSpec((1, d), lambda i: (0, 0))
    if g_next is None:
        return pl.pallas_call(
            _resnorm_last_kernel,
            grid=(t // tr,),
            in_specs=[row, row, vec],
            out_specs=row,
            out_shape=jax.ShapeDtypeStruct((t, d), F32),
            compiler_params=_params("parallel"),
            name="residual_norm_last",
        )(y, x, g_post.reshape(1, d))
    return pl.pallas_call(
        _resnorm_kernel,
        grid=(t // tr,),
        in_specs=[row, row, vec, vec],
        out_specs=[row, row],
        out_shape=[jax.ShapeDtypeStruct((t, d), F32), jax.ShapeDtypeStruct((t, d), BF16)],
        compiler_params=_params("parallel"),
        name="residual_norm",
    )(y, x, g_post.reshape(1, d), g_next.reshape(1, d))


def _mem_attn_kernel(q_ref, kv_ref, o_ref):
    for hh in range(MEM_HEADS):
        sl = slice(hh * HEAD_DIM, (hh + 1) * HEAD_DIM)
        slv = slice(MEM_WIDTH + hh * HEAD_DIM, MEM_WIDTH + (hh + 1) * HEAD_DIM)
        s = lax.dot_general(q_ref[0, :, sl], kv_ref[0, :, sl], (((1,), (1,)), ((), ())),
                            preferred_element_type=F32)
        p = jnp.exp(s - jnp.max(s, axis=-1, keepdims=True))
        l = jnp.sum(p, axis=-1, keepdims=True)
        o = jnp.dot(p.astype(BF16), kv_ref[0, :, slv], preferred_element_type=F32) / l
        o_ref[0, :, sl] = o.astype(o_ref.dtype)


def memory_attention(q, kv, tq=512):
    b, s, w = q.shape
    m = kv.shape[1]
    return pl.pallas_call(
        _mem_attn_kernel,
        grid=(b, s // tq),
        in_specs=[pl.BlockSpec((1, tq, w), lambda bb, i: (bb, i, 0)),
                  pl.BlockSpec((1, m, 2 * w), lambda bb, i: (bb, 0, 0))],
        out_specs=pl.BlockSpec((1, tq, w), lambda bb, i: (bb, i, 0)),
        out_shape=jax.ShapeDtypeStruct((b, s, w), BF16),
        compiler_params=_params("parallel", "arbitrary"),
        name="memory_attention",
    )(q, kv)


def kernel(x, mem, positions, norm_mix_pre, w_in, w_a, w_b, w_mix_out, norm_mix_post,
           lambda_q1, lambda_k1, lambda_q2, lambda_k2, diff_subln,
           norm_mem_pre, norm_mem_kv, w_mem_q, w_mem_kv, w_mem_o, norm_mem_post,
           norm_mlp_pre, w_mlp_up, w_mlp_down, norm_mlp_post):
    b, s, d = x.shape
    t = b * s
    depth = w_in.shape[0]
    m_len = mem.shape[1]
    xt = x.reshape(t, d)
    memt = mem.reshape(b * m_len, d)
    cos, sin = rope_tables(positions)

    h = rmsnorm(xt, norm_mix_pre[0])
    for layer in range(depth):
        lam_init = 0.8 - 0.6 * float(np.exp(-0.3 * layer))

        proj = in_projection(h, w_in[layer].astype(BF16), cos, sin)
        proj3 = proj.reshape(b, s, -1)
        os_, lses = [], []
        for g, (_, dil) in enumerate(DIL_CONFIGS):
            o_g, lse_g = dilated_group(proj3, g, dil)
            os_.append(o_g)
            lses.append(lse_g)
        out_a = dilated_merge(os_, lses)
        out_b = differential_attention(proj3, lambda_q1[layer], lambda_k1[layer],
                                       lambda_q2[layer], lambda_k2[layer],
                                       diff_subln[layer], lam_init).reshape(t, -1)
        merged = gated_merge(out_a, w_a[layer].astype(BF16), out_b, w_b[layer].astype(BF16), proj)
        y = matmul(merged, w_mix_out[layer].astype(BF16), F32, 1024, 1024, name="mix_out")
        xt, h = residual_norm(y, xt, norm_mix_post[layer], norm_mem_pre[layer])

        mn = rmsnorm(memt, norm_mem_kv[layer])
        kv = matmul(mn, w_mem_kv[layer].astype(BF16), BF16, b * m_len, 512, name="mem_kv")
        q = matmul(h, w_mem_q[layer].astype(BF16), BF16, 1024, MEM_WIDTH, epilogue="scale",
                   name="mem_q")
        o = memory_attention(q.reshape(b, s, -1), kv.reshape(b, m_len, -1)).reshape(t, -1)
        y = matmul(o, w_mem_o[layer].astype(BF16), F32, 1024, 1024, name="mem_out")
        xt, h = residual_norm(y, xt, norm_mem_post[layer], norm_mlp_pre[layer])

        u = matmul(h, w_mlp_up[layer].astype(BF16), BF16, 1024, 1024, epilogue="relu2",
                   name="mlp_up")
        y = matmul_kgrid(u, w_mlp_down[layer].astype(BF16), 1024, 1024, 2048, name="mlp_down")
        if layer + 1 < depth:
            xt, h = residual_norm(y, xt, norm_mlp_post[layer], norm_mix_pre[layer + 1])
        else:
            xt = residual_norm(y, xt, norm_mlp_post[layer])
    return xt.reshape(b, s, d)
```

```python
import functools

import numpy as np
import jax
import jax.numpy as jnp
from jax import lax
from jax.experimental import pallas as pl
from jax.experimental.pallas import tpu as pltpu

F32 = jnp.float32
BF16 = jnp.bfloat16

HEAD_DIM = 128
BF16_ROWS = 16
DIL_CONFIGS = ((128, 1), (512, 4), (2048, 16))
N_DIL_GROUPS = 3
DIL_HEADS = 8
DIL_GROUP_WIDTH = DIL_HEADS * HEAD_DIM
DIL_WIDTH = N_DIL_GROUPS * DIL_GROUP_WIDTH
DIFF_HEADS = 8
DIFF_QK_WIDTH = DIFF_HEADS * 2 * HEAD_DIM
DIFF_V_DIM = 2 * HEAD_DIM
DIFF_V_WIDTH = DIFF_HEADS * DIFF_V_DIM
MEM_HEADS = 4
MEM_WIDTH = MEM_HEADS * HEAD_DIM
ROPE_THETA = 10000.0
Q_BLOCK = 128
NORM_EPS = 1e-6
MASK_VALUE = -1e30
LOG2E = 1.4426950408889634
Q_PRESCALE = HEAD_DIM ** -0.5 * LOG2E

W_QA = 0
W_KA = W_QA + DIL_WIDTH
W_VA = W_KA + DIL_WIDTH
W_QB = W_VA + DIL_WIDTH

PROJ_TN = DIL_GROUP_WIDTH
P_QA = 0
P_KA = P_QA + DIL_GROUP_WIDTH
P_VA = P_KA + DIL_GROUP_WIDTH
P_QB = P_VA + DIL_GROUP_WIDTH
P_KB = P_QB + DIFF_QK_WIDTH
P_VB = P_KB + DIFF_QK_WIDTH
P_GA = P_VB + DIFF_V_WIDTH

V7X_VMEM_BYTES = 64 * 1024 * 1024
VMEM_LIMIT = V7X_VMEM_BYTES - 6 * 1024 * 1024


def _params(*sem):
    return pltpu.CompilerParams(dimension_semantics=sem, vmem_limit_bytes=VMEM_LIMIT)


def _rms(x, g):
    return x * lax.rsqrt(jnp.mean(x * x, axis=-1, keepdims=True) + NORM_EPS) * g


def _heads(width):
    return [slice(hh * HEAD_DIM, (hh + 1) * HEAD_DIM) for hh in range(width // HEAD_DIM)]


def _qkt(q, k):
    return lax.dot_general(q, k, (((1,), (1,)), ((), ())), preferred_element_type=F32)


def _rmsnorm_kernel(x_ref, g_ref, o_ref):
    o_ref[...] = _rms(x_ref[...], g_ref[...]).astype(o_ref.dtype)


def rmsnorm(x, g, tr=256):
    t, d = x.shape
    return pl.pallas_call(
        _rmsnorm_kernel,
        grid=(t // tr,),
        in_specs=[pl.BlockSpec((tr, d), lambda i: (i, 0)),
                  pl.BlockSpec((1, d), lambda i: (0, 0))],
        out_specs=pl.BlockSpec((tr, d), lambda i: (i, 0)),
        out_shape=jax.ShapeDtypeStruct((t, d), BF16),
        compiler_params=_params("parallel"),
        name="rmsnorm",
    )(x, g.reshape(1, d))


def _rope_table_kernel(pos_ref, inv_ref, sign_ref, cos_ref, sin_ref):
    ang = pos_ref[...] * inv_ref[...]
    cos_ref[...] = jnp.cos(ang)
    sin_ref[...] = jnp.sin(ang) * sign_ref[...]


def rope_tables(positions, tr=1024):
    t = positions.size
    half = HEAD_DIM // 2
    inv = ROPE_THETA ** (-jnp.arange(0, HEAD_DIM, 2, dtype=F32) / HEAD_DIM)
    inv2 = jnp.concatenate([inv, inv]).reshape(1, HEAD_DIM)
    sign = jnp.concatenate([-jnp.ones((half,), F32), jnp.ones((half,), F32)]).reshape(1, HEAD_DIM)
    pos = positions.astype(F32).reshape(t, 1)
    return pl.pallas_call(
        _rope_table_kernel,
        grid=(t // tr,),
        in_specs=[pl.BlockSpec((tr, 1), lambda i: (i, 0)),
                  pl.BlockSpec((1, HEAD_DIM), lambda i: (0, 0)),
                  pl.BlockSpec((1, HEAD_DIM), lambda i: (0, 0))],
        out_specs=[pl.BlockSpec((tr, HEAD_DIM), lambda i: (i, 0))] * 2,
        out_shape=[jax.ShapeDtypeStruct((t, HEAD_DIM), F32)] * 2,
        compiler_params=_params("parallel"),
        name="rope_tables",
    )(pos, inv2, sign)


def _rope(xh, c, s):
    return xh * c + pltpu.roll(xh, HEAD_DIM // 2, 1) * s


def _inproj_kernel(h_ref, w_ref, cos_ref, sin_ref, o_ref):
    j = pl.program_id(1)
    acc = jnp.dot(h_ref[...], w_ref[...], preferred_element_type=F32)
    is_q = (j == P_QA // PROJ_TN) | ((j >= P_QB // PROJ_TN) & (j < P_KB // PROJ_TN))
    is_k = (j == P_KA // PROJ_TN) | ((j >= P_KB // PROJ_TN) & (j < P_VB // PROJ_TN))

    @pl.when(is_q | is_k)
    def _():
        f = jnp.where(is_q, Q_PRESCALE, 1.0).astype(F32)
        c = cos_ref[...] * f
        s = sin_ref[...] * f
        for sl in _heads(PROJ_TN):
            o_ref[:, sl] = _rope(acc[:, sl], c, s).astype(o_ref.dtype)

    @pl.when(jnp.logical_not(is_q | is_k))
    def _():
        o_ref[...] = acc.astype(o_ref.dtype)


def in_projection(h, w, cos, sin, tm=1024):
    t, k = h.shape
    tn = PROJ_TN
    n_out = w.shape[1] - 3 * (DIL_WIDTH - DIL_GROUP_WIDTH)
    n_group0 = 3
    stride_a = DIL_WIDTH // tn
    skip = W_QB // tn - n_group0

    def w_map(i, j):
        return (0, jnp.where(j < n_group0, j * stride_a, j + skip))

    return pl.pallas_call(
        _inproj_kernel,
        grid=(t // tm, n_out // tn),
        in_specs=[pl.BlockSpec((tm, k), lambda i, j: (i, 0)),
                  pl.BlockSpec((k, tn), w_map),
                  pl.BlockSpec((tm, HEAD_DIM), lambda i, j: (i, 0)),
                  pl.BlockSpec((tm, HEAD_DIM), lambda i, j: (i, 0))],
        out_specs=pl.BlockSpec((tm, tn), lambda i, j: (i, j)),
        out_shape=jax.ShapeDtypeStruct((t, n_out), BF16),
        compiler_params=_params("parallel", "arbitrary"),
        name="in_projection",
    )(h, w, cos, sin)


def _inproj_dilated_kernel(h_ref, w_ref, cos_ref, sin_ref, o_ref, slab_ref, *, dil):
    j = pl.program_id(1)
    tm = h_ref.shape[0]
    acc = jnp.dot(h_ref[...], w_ref[...], preferred_element_type=F32)
    f = jnp.where(j == 0, Q_PRESCALE, 1.0).astype(F32)
    c = jnp.where(j < 2, cos_ref[...] * f, 1.0)
    s = jnp.where(j < 2, sin_ref[...] * f, 0.0)
    for hh, sl in enumerate(_heads(PROJ_TN)):
        slab_ref[hh] = _rope(acc[:, sl], c, s)
        for r in range(dil):
            o_ref[0, r, :, sl] = slab_ref[hh, pl.ds(r, tm // dil, stride=dil), :].astype(o_ref.dtype)


def in_projection_dilated(h, w, cos, sin, g, dil, batch, tm=1024):
    t, k = h.shape
    tn = PROJ_TN
    s_len = t // batch
    tiles_per_seq = s_len // tm
    stride_a = DIL_WIDTH // tn
    return pl.pallas_call(
        functools.partial(_inproj_dilated_kernel, dil=dil),
        grid=(t // tm, 3),
        in_specs=[pl.BlockSpec((tm, k), lambda i, j: (i, 0)),
                  pl.BlockSpec((k, tn), lambda i, j: (0, j * stride_a + g)),
                  pl.BlockSpec((tm, HEAD_DIM), lambda i, j: (i, 0)),
                  pl.BlockSpec((tm, HEAD_DIM), lambda i, j: (i, 0))],
        out_specs=pl.BlockSpec((1, dil, tm // dil, tn),
                               lambda i, j: (i // tiles_per_seq, 0, i % tiles_per_seq, j)),
        out_shape=jax.ShapeDtypeStruct((batch, dil, s_len // dil, 3 * tn), BF16),
        scratch_shapes=[pltpu.VMEM((tn // HEAD_DIM, tm, HEAD_DIM), F32)],
        compiler_params=_params("parallel", "arbitrary"),
        name=f"in_projection_d{dil}",
    )(h, w, cos, sin)


def _mm_kernel(a_ref, w_ref, o_ref, *, epilogue):
    acc = jnp.dot(a_ref[...], w_ref[...], preferred_element_type=F32)
    if epilogue == "relu2":
        acc = jnp.square(jnp.maximum(acc, 0.0))
    elif epilogue == "q_prescale":
        acc = acc * Q_PRESCALE
    o_ref[...] = acc.astype(o_ref.dtype)


def matmul(a, w, out_dtype, tm, tn, epilogue=None, name="matmul"):
    m, k = a.shape
    n = w.shape[1]
    return pl.pallas_call(
        functools.partial(_mm_kernel, epilogue=epilogue),
        grid=(m // tm, n // tn),
        in_specs=[pl.BlockSpec((tm, k), lambda i, j: (i, 0)),
                  pl.BlockSpec((k, tn), lambda i, j: (0, j))],
        out_specs=pl.BlockSpec((tm, tn), lambda i, j: (i, j)),
        out_shape=jax.ShapeDtypeStruct((m, n), out_dtype),
        compiler_params=_params("parallel", "arbitrary"),
        name=name,
    )(a, w)


def _mm_kgrid_kernel(a_ref, w_ref, o_ref):
    @pl.when(pl.program_id(2) == 0)
    def _():
        o_ref[...] = jnp.zeros_like(o_ref)

    o_ref[...] += jnp.dot(a_ref[...], w_ref[...], preferred_element_type=F32)


def matmul_kgrid(a, w, tm, tn, tk, name="matmul_kgrid"):
    m, k = a.shape
    n = w.shape[1]
    return pl.pallas_call(
        _mm_kgrid_kernel,
        grid=(m // tm, n // tn, k // tk),
        in_specs=[pl.BlockSpec((tm, tk), lambda i, j, kk: (i, kk)),
                  pl.BlockSpec((tk, tn), lambda i, j, kk: (kk, j))],
        out_specs=pl.BlockSpec((tm, tn), lambda i, j, kk: (i, j)),
        out_shape=jax.ShapeDtypeStruct((m, n), F32),
        compiler_params=_params("parallel", "parallel", "arbitrary"),
        name=name,
    )(a, w)


def _dilated_kernel(q_ref, kp_ref, kc_ref, kn_ref, vp_ref, vc_ref, vn_ref,
                    o_ref, lse_ref, *, class_len):
    i = pl.program_id(2)
    half = Q_BLOCK // 2
    nk = 2 * Q_BLOCK
    row = lax.broadcasted_iota(jnp.int32, (Q_BLOCK, nk), 0)
    col = lax.broadcasted_iota(jnp.int32, (Q_BLOCK, nk), 1)
    rel = col - half - row
    kpos = i * Q_BLOCK - half + col
    valid = (rel >= -half) & (rel <= half) & (kpos >= 0) & (kpos < class_len)
    for sl in _heads(DIL_GROUP_WIDTH):
        q = q_ref[0, 0, :, sl]
        k = jnp.concatenate([kp_ref[0, 0, :, sl], kc_ref[0, 0, :, sl], kn_ref[0, 0, :, sl]], axis=0)
        v = jnp.concatenate([vp_ref[0, 0, :, sl], vc_ref[0, 0, :, sl], vn_ref[0, 0, :, sl]], axis=0)
        s = jnp.where(valid, _qkt(q, k), MASK_VALUE)
        m = jnp.max(s, axis=-1, keepdims=True)
        p = jnp.exp2(s - m)
        l = jnp.sum(p, axis=-1, keepdims=True)
        o_ref[0, 0, :, sl] = jnp.dot(p.astype(BF16), v, preferred_element_type=F32) / l
        lse_ref[0, 0, :, sl] = jnp.broadcast_to(m + jnp.log2(l), (Q_BLOCK, HEAD_DIM))


def dilated_group(qkv, cq, ck, cv):
    b, dil, class_len, _ = qkv.shape
    half = Q_BLOCK // 2
    n_half_blocks = class_len // half
    w = DIL_GROUP_WIDTH

    def cur(c):
        return pl.BlockSpec((1, 1, Q_BLOCK, w), lambda bb, r, i: (bb, r, i, c))

    def prev(c):
        return pl.BlockSpec((1, 1, half, w), lambda bb, r, i: (bb, r, jnp.maximum(2 * i - 1, 0), c))

    def nxt(c):
        return pl.BlockSpec((1, 1, half, w),
                            lambda bb, r, i: (bb, r, jnp.minimum(2 * i + 2, n_half_blocks - 1), c))

    out_spec = pl.BlockSpec((1, 1, Q_BLOCK, w), lambda bb, r, i: (bb, r, i, 0))
    out_sds = jax.ShapeDtypeStruct((b, dil, class_len, w), F32)
    return pl.pallas_call(
        functools.partial(_dilated_kernel, class_len=class_len),
        grid=(b, dil, class_len // Q_BLOCK),
        in_specs=[cur(cq), prev(ck), cur(ck), nxt(ck), prev(cv), cur(cv), nxt(cv)],
        out_specs=[out_spec, out_spec],
        out_shape=[out_sds, out_sds],
        compiler_params=_params("parallel", "parallel", "arbitrary"),
        name=f"dilated_attention_d{dil}",
    )(qkv, qkv, qkv, qkv, qkv, qkv, qkv)


def _dil_merge_kernel(o0_ref, l0_ref, o1_ref, l1_ref, o2_ref, l2_ref, out_ref,
                      so1, sl1, so2, sl2):
    for src_o, src_l, dst_o, dst_l in ((o1_ref, l1_ref, so1, sl1), (o2_ref, l2_ref, so2, sl2)):
        dil, n = src_o.shape[1], src_o.shape[2]
        for hh, sl in enumerate(_heads(DIL_GROUP_WIDTH)):
            for r in range(dil):
                dst_o[hh, pl.ds(r, n, stride=dil), :] = src_o[0, r, :, sl]
                dst_l[hh, pl.ds(r, n, stride=dil), :] = src_l[0, r, :, sl]
    for hh, sl in enumerate(_heads(DIL_GROUP_WIDTH)):
        a0, a1, a2 = l0_ref[0, 0, :, sl], sl1[hh], sl2[hh]
        m = jnp.maximum(jnp.maximum(a0, a1), a2)
        e0, e1, e2 = jnp.exp2(a0 - m), jnp.exp2(a1 - m), jnp.exp2(a2 - m)
        num = e0 * o0_ref[0, 0, :, sl] + e1 * so1[hh] + e2 * so2[hh]
        out_ref[0, :, sl] = (num / (e0 + e1 + e2)).astype(out_ref.dtype)


def dilated_merge(outs, tr=512):
    (o0, l0), (o1, l1), (o2, l2) = outs
    b, _, s_len, w = o0.shape

    def spec(a):
        dil = a.shape[1]
        return pl.BlockSpec((1, dil, tr // dil, w), lambda bb, i: (bb, 0, i, 0))

    slab = pltpu.VMEM((w // HEAD_DIM, tr, HEAD_DIM), F32)
    return pl.pallas_call(
        _dil_merge_kernel,
        grid=(b, s_len // tr),
        in_specs=[spec(o0), spec(l0), spec(o1), spec(l1), spec(o2), spec(l2)],
        out_specs=pl.BlockSpec((1, tr, w), lambda bb, i: (bb, i, 0)),
        out_shape=jax.ShapeDtypeStruct((b, s_len, w), BF16),
        scratch_shapes=[slab, slab, slab, slab],
        compiler_params=_params("parallel", "arbitrary"),
        name="dilated_merge",
    )(o0, l0, o1, l1, o2, l2)


def _diff_kernel(q_ref, k_ref, v_ref, lq1, lk1, lq2, lk2, g_ref, o_ref,
                 s_scr, p_scr, l_scr, *, lam_init):
    lam = (jnp.exp(jnp.sum(lq1[...] * lk1[...], axis=-1, keepdims=True))
           - jnp.exp(jnp.sum(lq2[...] * lk2[...], axis=-1, keepdims=True)) + lam_init)
    tq = q_ref.shape[1]

    def component(c):
        sl = slice(c * HEAD_DIM, (c + 1) * HEAD_DIM)
        s_scr[c] = _qkt(q_ref[0, :, sl], k_ref[0, :, sl])
        for r in range(tq // BF16_ROWS):
            rows = slice(r * BF16_ROWS, (r + 1) * BF16_ROWS)
            m = jnp.max(s_scr[c, rows, :], axis=-1, keepdims=True)
            p = jnp.exp2(s_scr[c, rows, :] - m)
            l_scr[c, rows, :] = jnp.broadcast_to(jnp.sum(p, axis=-1, keepdims=True),
                                                 (BF16_ROWS, HEAD_DIM))
            p_scr[c, rows, :] = p.astype(BF16)
        return jnp.dot(p_scr[c], v_ref[0], preferred_element_type=F32) / l_scr[c, :, :1]

    o = component(0) - lam * component(1)
    o_ref[0] = (_rms(o, g_ref[...]) * (1.0 - lam_init)).astype(o_ref.dtype)


def differential_attention(proj, lq1, lk1, lq2, lk2, subln, lam_init, tq=512):
    b, s, _ = proj.shape
    vec = pl.BlockSpec((1, HEAD_DIM), lambda bb, h, i: (0, 0))
    cq, ck, cv = P_QB // DIFF_V_DIM, P_KB // DIFF_V_DIM, P_VB // DIFF_V_DIM
    return pl.pallas_call(
        functools.partial(_diff_kernel, lam_init=lam_init),
        grid=(b, DIFF_HEADS, s // tq),
        in_specs=[pl.BlockSpec((1, tq, DIFF_V_DIM), lambda bb, h, i: (bb, i, cq + h)),
                  pl.BlockSpec((1, s, DIFF_V_DIM), lambda bb, h, i: (bb, 0, ck + h)),
                  pl.BlockSpec((1, s, DIFF_V_DIM), lambda bb, h, i: (bb, 0, cv + h)),
                  vec, vec, vec, vec,
                  pl.BlockSpec((1, DIFF_V_DIM), lambda bb, h, i: (0, 0))],
        out_specs=pl.BlockSpec((1, tq, DIFF_V_DIM), lambda bb, h, i: (bb, i, h)),
        out_shape=jax.ShapeDtypeStruct((b, s, DIFF_V_WIDTH), BF16),
        scratch_shapes=[pltpu.VMEM((2, tq, s), F32), pltpu.VMEM((2, tq, s), BF16),
                        pltpu.VMEM((2, tq, HEAD_DIM), F32)],
        compiler_params=_params("parallel", "parallel", "arbitrary"),
        name="differential_attention",
    )(proj, proj, proj, lq1.reshape(1, -1), lk1.reshape(1, -1), lq2.reshape(1, -1),
      lk2.reshape(1, -1), subln.reshape(1, -1))


def _gate_kernel(oa_ref, wa_ref, ob_ref, wb_ref, ga_ref, gb_ref, o_ref):
    ya = jnp.dot(oa_ref[...], wa_ref[...], preferred_element_type=F32)
    yb = jnp.dot(ob_ref[...], wb_ref[...], preferred_element_type=F32)
    o = (jax.nn.sigmoid(ga_ref[...].astype(F32)) * ya
         + jax.nn.sigmoid(gb_ref[...].astype(F32)) * yb)
    o_ref[...] = o.astype(o_ref.dtype)


def gated_merge(out_a, w_a, out_b, w_b, proj2d, tm=1024, tn=512):
    t = out_a.shape[0]
    n = w_a.shape[1]
    ca = P_GA // tn
    cb = (P_GA + n) // tn
    return pl.pallas_call(
        _gate_kernel,
        grid=(t // tm, n // tn),
        in_specs=[pl.BlockSpec((tm, out_a.shape[1]), lambda i, j: (i, 0)),
                  pl.BlockSpec((w_a.shape[0], tn), lambda i, j: (0, j)),
                  pl.BlockSpec((tm, out_b.shape[1]), lambda i, j: (i, 0)),
                  pl.BlockSpec((w_b.shape[0], tn), lambda i, j: (0, j)),
                  pl.BlockSpec((tm, tn), lambda i, j: (i, ca + j)),
                  pl.BlockSpec((tm, tn), lambda i, j: (i, cb + j))],
        out_specs=pl.BlockSpec((tm, tn), lambda i, j: (i, j)),
        out_shape=jax.ShapeDtypeStruct((t, n), BF16),
        compiler_params=_params("parallel", "arbitrary"),
        name="gated_merge",
    )(out_a, w_a, out_b, w_b, proj2d, proj2d)


def _resnorm_kernel(y_ref, x_ref, gp_ref, gn_ref, xo_ref, ho_ref):
    xn = x_ref[...] + _rms(y_ref[...], gp_ref[...])
    xo_ref[...] = xn
    ho_ref[...] = _rms(xn, gn_ref[...]).astype(ho_ref.dtype)


def _resnorm_last_kernel(y_ref, x_ref, gp_ref, xo_ref):
    xo_ref[...] = x_ref[...] + _rms(y_ref[...], gp_ref[...])


def residual_norm(y, x, g_post, g_next=None, tr=256):
    t, d = x.shape
    row = pl.BlockSpec((tr, d), lambda i: (i, 0))
    vec = pl.BlockSpec((1, d), lambda i: (0, 0))
    if g_next is None:
        return pl.pallas_call(
            _resnorm_last_kernel,
            grid=(t // tr,),
            in_specs=[row, row, vec],
            out_specs=row,
            out_shape=jax.ShapeDtypeStruct((t, d), F32),
            compiler_params=_params("parallel"),
            name="residual_norm_last",
        )(y, x, g_post.reshape(1, d))
    return pl.pallas_call(
        _resnorm_kernel,
        grid=(t // tr,),
        in_specs=[row, row, vec, vec],
        out_specs=[row, row],
        out_shape=[jax.ShapeDtypeStruct((t, d), F32), jax.ShapeDtypeStruct((t, d), BF16)],
        compiler_params=_params("parallel"),
        name="residual_norm",
    )(y, x, g_post.reshape(1, d), g_next.reshape(1, d))


def _mem_attn_kernel(q_ref, kv_ref, o_ref):
    for hh, sl in enumerate(_heads(MEM_WIDTH)):
        slv = slice(MEM_WIDTH + hh * HEAD_DIM, MEM_WIDTH + (hh + 1) * HEAD_DIM)
        s = _qkt(q_ref[0, :, sl], kv_ref[0, :, sl])
        p = jnp.exp2(s - jnp.max(s, axis=-1, keepdims=True))
        l = jnp.sum(p, axis=-1, keepdims=True)
        o = jnp.dot(p.astype(BF16), kv_ref[0, :, slv], preferred_element_type=F32) / l
        o_ref[0, :, sl] = o.astype(o_ref.dtype)


def memory_attention(q, kv, tq=512):
    b, s, w = q.shape
    m = kv.shape[1]
    return pl.pallas_call(
        _mem_attn_kernel,
        grid=(b, s // tq),
        in_specs=[pl.BlockSpec((1, tq, w), lambda bb, i: (bb, i, 0)),
                  pl.BlockSpec((1, m, 2 * w), lambda bb, i: (bb, 0, 0))],
        out_specs=pl.BlockSpec((1, tq, w), lambda bb, i: (bb, i, 0)),
        out_shape=jax.ShapeDtypeStruct((b, s, w), BF16),
        compiler_params=_params("parallel", "arbitrary"),
        name="memory_attention",
    )(q, kv)


def kernel(x, mem, positions, norm_mix_pre, w_in, w_a, w_b, w_mix_out, norm_mix_post,
           lambda_q1, lambda_k1, lambda_q2, lambda_k2, diff_subln,
           norm_mem_pre, norm_mem_kv, w_mem_q, w_mem_kv, w_mem_o, norm_mem_post,
           norm_mlp_pre, w_mlp_up, w_mlp_down, norm_mlp_post):
    b, s, d = x.shape
    t = b * s
    depth = w_in.shape[0]
    m_len = mem.shape[1]
    xt = x.reshape(t, d)
    memt = mem.reshape(b * m_len, d)
    cos, sin = rope_tables(positions)

    h = rmsnorm(xt, norm_mix_pre[0])
    for layer in range(depth):
        lam_init = 0.8 - 0.6 * float(np.exp(-0.3 * layer))

        w_in_b = w_in[layer].astype(BF16)
        proj = in_projection(h, w_in_b, cos, sin)
        proj3 = proj.reshape(b, s, -1)
        tile = DIL_GROUP_WIDTH
        outs = [dilated_group(proj3.reshape(b, 1, s, -1), P_QA // tile, P_KA // tile, P_VA // tile)]
        for g, (_, dil) in enumerate(DIL_CONFIGS):
            if dil > 1:
                qkv = in_projection_dilated(h, w_in_b, cos, sin, g, dil, b)
                outs.append(dilated_group(qkv, 0, 1, 2))
        out_a = dilated_merge(outs).reshape(t, -1)
        out_b = differential_attention(proj3, lambda_q1[layer], lambda_k1[layer],
                                       lambda_q2[layer], lambda_k2[layer],
                                       diff_subln[layer], lam_init).reshape(t, -1)
        merged = gated_merge(out_a, w_a[layer].astype(BF16), out_b, w_b[layer].astype(BF16), proj)
        y = matmul(merged, w_mix_out[layer].astype(BF16), F32, 1024, 1024, name="mix_out")
        xt, h = residual_norm(y, xt, norm_mix_post[layer], norm_mem_pre[layer])

        mn = rmsnorm(memt, norm_mem_kv[layer])
        kv = matmul(mn, w_mem_kv[layer].astype(BF16), BF16, b * m_len, 512, name="mem_kv")
        q = matmul(h, w_mem_q[layer].astype(BF16), BF16, 1024, MEM_WIDTH, epilogue="q_prescale",
                   name="mem_q")
        o = memory_attention(q.reshape(b, s, -1), kv.reshape(b, m_len, -1)).reshape(t, -1)
        y = matmul(o, w_mem_o[layer].astype(BF16), F32, 1024, 1024, name="mem_out")
        xt, h = residual_norm(y, xt, norm_mem_post[layer], norm_mlp_pre[layer])

        u = matmul(h, w_mlp_up[layer].astype(BF16), BF16, 1024, 1024, epilogue="relu2",
                   name="mlp_up")
        y = matmul_kgrid(u, w_mlp_down[layer].astype(BF16), 1024, 1024, 2048, name="mlp_down")
        if layer + 1 < depth:
            xt, h = residual_norm(y, xt, norm_mlp_post[layer], norm_mix_pre[layer + 1])
        else:
            xt = residual_norm(y, xt, norm_mlp_post[layer])
    return xt.reshape(b, s, d)
```

```python
import functools

import numpy as np
import jax
import jax.numpy as jnp
from jax import lax
from jax.experimental import pallas as pl
from jax.experimental.pallas import tpu as pltpu

F32 = jnp.float32
BF16 = jnp.bfloat16

HEAD_DIM = 128
BF16_ROWS = 16
DIL_CONFIGS = ((128, 1), (512, 4), (2048, 16))
N_DIL_GROUPS = 3
DIL_HEADS = 8
DIL_GROUP_WIDTH = DIL_HEADS * HEAD_DIM
DIL_WIDTH = N_DIL_GROUPS * DIL_GROUP_WIDTH
DIFF_HEADS = 8
DIFF_QK_WIDTH = DIFF_HEADS * 2 * HEAD_DIM
DIFF_V_DIM = 2 * HEAD_DIM
DIFF_V_WIDTH = DIFF_HEADS * DIFF_V_DIM
MEM_HEADS = 4
MEM_WIDTH = MEM_HEADS * HEAD_DIM
ROPE_THETA = 10000.0
Q_BLOCK = 128
NORM_EPS = 1e-6
MASK_VALUE = -1e30
LOG2E = 1.4426950408889634
Q_PRESCALE = HEAD_DIM ** -0.5 * LOG2E

W_QA = 0
W_KA = W_QA + DIL_WIDTH
W_VA = W_KA + DIL_WIDTH
W_QB = W_VA + DIL_WIDTH

PROJ_TN = 512
P_QA = 0
P_KA = P_QA + DIL_GROUP_WIDTH
P_VA = P_KA + DIL_GROUP_WIDTH
P_QB = P_VA + DIL_GROUP_WIDTH
P_KB = P_QB + DIFF_QK_WIDTH
P_VB = P_KB + DIFF_QK_WIDTH
P_GA = P_VB + DIFF_V_WIDTH

V7X_VMEM_BYTES = 64 * 1024 * 1024
VMEM_LIMIT = V7X_VMEM_BYTES - 6 * 1024 * 1024


def _params(*sem):
    return pltpu.CompilerParams(dimension_semantics=sem, vmem_limit_bytes=VMEM_LIMIT)


def _rms(x, g):
    return x * lax.rsqrt(jnp.mean(x * x, axis=-1, keepdims=True) + NORM_EPS) * g


def _heads(width):
    return [slice(hh * HEAD_DIM, (hh + 1) * HEAD_DIM) for hh in range(width // HEAD_DIM)]


def _qkt(q, k):
    return lax.dot_general(q, k, (((1,), (1,)), ((), ())), preferred_element_type=F32)


def _rmsnorm_kernel(x_ref, g_ref, o_ref):
    o_ref[...] = _rms(x_ref[...], g_ref[...]).astype(o_ref.dtype)


def rmsnorm(x, g, tr=256):
    t, d = x.shape
    return pl.pallas_call(
        _rmsnorm_kernel,
        grid=(t // tr,),
        in_specs=[pl.BlockSpec((tr, d), lambda i: (i, 0)),
                  pl.BlockSpec((1, d), lambda i: (0, 0))],
        out_specs=pl.BlockSpec((tr, d), lambda i: (i, 0)),
        out_shape=jax.ShapeDtypeStruct((t, d), BF16),
        compiler_params=_params("parallel"),
        name="rmsnorm",
    )(x, g.reshape(1, d))


def _rope_table_kernel(pos_ref, inv_ref, sign_ref, cos_ref, sin_ref):
    ang = pos_ref[...] * inv_ref[...]
    cos_ref[...] = jnp.cos(ang)
    sin_ref[...] = jnp.sin(ang) * sign_ref[...]


def rope_tables(positions, tr=1024):
    t = positions.size
    half = HEAD_DIM // 2
    inv = ROPE_THETA ** (-jnp.arange(0, HEAD_DIM, 2, dtype=F32) / HEAD_DIM)
    inv2 = jnp.concatenate([inv, inv]).reshape(1, HEAD_DIM)
    sign = jnp.concatenate([-jnp.ones((half,), F32), jnp.ones((half,), F32)]).reshape(1, HEAD_DIM)
    pos = positions.astype(F32).reshape(t, 1)
    return pl.pallas_call(
        _rope_table_kernel,
        grid=(t // tr,),
        in_specs=[pl.BlockSpec((tr, 1), lambda i: (i, 0)),
                  pl.BlockSpec((1, HEAD_DIM), lambda i: (0, 0)),
                  pl.BlockSpec((1, HEAD_DIM), lambda i: (0, 0))],
        out_specs=[pl.BlockSpec((tr, HEAD_DIM), lambda i: (i, 0))] * 2,
        out_shape=[jax.ShapeDtypeStruct((t, HEAD_DIM), F32)] * 2,
        compiler_params=_params("parallel"),
        name="rope_tables",
    )(pos, inv2, sign)


def _rope(xh, c, s):
    return xh * c + pltpu.roll(xh, HEAD_DIM // 2, 1) * s


def _inproj_kernel(h_ref, w_ref, cos_ref, sin_ref, o_ref):
    j = pl.program_id(1)
    acc = jnp.dot(h_ref[...], w_ref[...].astype(BF16), preferred_element_type=F32)
    is_q = (j < P_KA // PROJ_TN) | ((j >= P_QB // PROJ_TN) & (j < P_KB // PROJ_TN))
    is_k = (((j >= P_KA // PROJ_TN) & (j < P_VA // PROJ_TN))
            | ((j >= P_KB // PROJ_TN) & (j < P_VB // PROJ_TN)))

    @pl.when(is_q | is_k)
    def _():
        f = jnp.where(is_q, Q_PRESCALE, 1.0).astype(F32)
        c = cos_ref[...] * f
        s = sin_ref[...] * f
        for sl in _heads(PROJ_TN):
            o_ref[:, sl] = _rope(acc[:, sl], c, s).astype(o_ref.dtype)

    @pl.when(jnp.logical_not(is_q | is_k))
    def _():
        o_ref[...] = acc.astype(o_ref.dtype)


def in_projection(h, w, cos, sin, tm=1024):
    t, k = h.shape
    tn = PROJ_TN
    n_out = w.shape[1] - 3 * (DIL_WIDTH - DIL_GROUP_WIDTH)
    per = DIL_GROUP_WIDTH // tn
    n_group0 = 3 * per
    stride_a = DIL_WIDTH // tn
    skip = W_QB // tn - n_group0

    def w_map(i, j):
        return (0, jnp.where(j < n_group0, (j // per) * stride_a + j % per, j + skip))

    return pl.pallas_call(
        _inproj_kernel,
        grid=(t // tm, n_out // tn),
        in_specs=[pl.BlockSpec((tm, k), lambda i, j: (i, 0)),
                  pl.BlockSpec((k, tn), w_map),
                  pl.BlockSpec((tm, HEAD_DIM), lambda i, j: (i, 0)),
                  pl.BlockSpec((tm, HEAD_DIM), lambda i, j: (i, 0))],
        out_specs=pl.BlockSpec((tm, tn), lambda i, j: (i, j)),
        out_shape=jax.ShapeDtypeStruct((t, n_out), BF16),
        compiler_params=_params("parallel", "arbitrary"),
        name="in_projection",
    )(h, w, cos, sin)


def _inproj_dilated_kernel(h_ref, w_ref, cos_ref, sin_ref, o_ref, slab_ref, *, dil):
    seg = pl.program_id(1) // (DIL_GROUP_WIDTH // PROJ_TN)
    tm = h_ref.shape[0]
    acc = jnp.dot(h_ref[...], w_ref[...].astype(BF16), preferred_element_type=F32)
    f = jnp.where(seg == 0, Q_PRESCALE, 1.0).astype(F32)
    c = jnp.where(seg < 2, cos_ref[...] * f, 1.0)
    s = jnp.where(seg < 2, sin_ref[...] * f, 0.0)
    for hh, sl in enumerate(_heads(PROJ_TN)):
        slab_ref[hh] = _rope(acc[:, sl], c, s)
        for r in range(dil):
            o_ref[0, r, :, sl] = slab_ref[hh, pl.ds(r, tm // dil, stride=dil), :].astype(o_ref.dtype)


def in_projection_dilated(h, w, cos, sin, g, dil, batch, tm=1024):
    t, k = h.shape
    tn = PROJ_TN
    s_len = t // batch
    tiles_per_seq = s_len // tm
    per = DIL_GROUP_WIDTH // tn
    stride_a = DIL_WIDTH // tn
    return pl.pallas_call(
        functools.partial(_inproj_dilated_kernel, dil=dil),
        grid=(t // tm, 3 * per),
        in_specs=[pl.BlockSpec((tm, k), lambda i, j: (i, 0)),
                  pl.BlockSpec((k, tn), lambda i, j: (0, (j // per) * stride_a + g * per + j % per)),
                  pl.BlockSpec((tm, HEAD_DIM), lambda i, j: (i, 0)),
                  pl.BlockSpec((tm, HEAD_DIM), lambda i, j: (i, 0))],
        out_specs=pl.BlockSpec((1, dil, tm // dil, tn),
                               lambda i, j: (i // tiles_per_seq, 0, i % tiles_per_seq, j)),
        out_shape=jax.ShapeDtypeStruct((batch, dil, s_len // dil, 3 * DIL_GROUP_WIDTH), BF16),
        scratch_shapes=[pltpu.VMEM((tn // HEAD_DIM, tm, HEAD_DIM), F32)],
        compiler_params=_params("parallel", "arbitrary"),
        name=f"in_projection_d{dil}",
    )(h, w, cos, sin)


def _mm_kernel(a_ref, w_ref, o_ref, *, epilogue):
    acc = jnp.dot(a_ref[...], w_ref[...].astype(BF16), preferred_element_type=F32)
    if epilogue == "relu2":
        acc = jnp.square(jnp.maximum(acc, 0.0))
    elif epilogue == "q_prescale":
        acc = acc * Q_PRESCALE
    o_ref[...] = acc.astype(o_ref.dtype)


def matmul(a, w, out_dtype, tm, tn, epilogue=None, name="matmul"):
    m, k = a.shape
    n = w.shape[1]
    return pl.pallas_call(
        functools.partial(_mm_kernel, epilogue=epilogue),
        grid=(m // tm, n // tn),
        in_specs=[pl.BlockSpec((tm, k), lambda i, j: (i, 0)),
                  pl.BlockSpec((k, tn), lambda i, j: (0, j))],
        out_specs=pl.BlockSpec((tm, tn), lambda i, j: (i, j)),
        out_shape=jax.ShapeDtypeStruct((m, n), out_dtype),
        compiler_params=_params("parallel", "arbitrary"),
        name=name,
    )(a, w)


def _mm_kgrid_kernel(a_ref, w_ref, o_ref):
    @pl.when(pl.program_id(2) == 0)
    def _():
        o_ref[...] = jnp.zeros_like(o_ref)

    o_ref[...] += jnp.dot(a_ref[...], w_ref[...].astype(BF16), preferred_element_type=F32)


def matmul_kgrid(a, w, tm, tn, tk, name="matmul_kgrid"):
    m, k = a.shape
    n = w.shape[1]
    return pl.pallas_call(
        _mm_kgrid_kernel,
        grid=(m // tm, n // tn, k // tk),
        in_specs=[pl.BlockSpec((tm, tk), lambda i, j, kk: (i, kk)),
                  pl.BlockSpec((tk, tn), lambda i, j, kk: (kk, j))],
        out_specs=pl.BlockSpec((tm, tn), lambda i, j, kk: (i, j)),
        out_shape=jax.ShapeDtypeStruct((m, n), F32),
        compiler_params=_params("parallel", "parallel", "arbitrary"),
        name=name,
    )(a, w)


def _dilated_kernel(q_ref, kp_ref, kc_ref, kn_ref, vp_ref, vc_ref, vn_ref,
                    o_ref, lse_ref, *, class_len):
    i = pl.program_id(2)
    half = Q_BLOCK // 2
    nk = 2 * Q_BLOCK
    row = lax.broadcasted_iota(jnp.int32, (Q_BLOCK, nk), 0)
    col = lax.broadcasted_iota(jnp.int32, (Q_BLOCK, nk), 1)
    rel = col - half - row
    kpos = i * Q_BLOCK - half + col
    valid = (rel >= -half) & (rel <= half) & (kpos >= 0) & (kpos < class_len)
    for sl in _heads(DIL_GROUP_WIDTH):
        q = q_ref[0, 0, :, sl]
        k = jnp.concatenate([kp_ref[0, 0, :, sl], kc_ref[0, 0, :, sl], kn_ref[0, 0, :, sl]], axis=0)
        v = jnp.concatenate([vp_ref[0, 0, :, sl], vc_ref[0, 0, :, sl], vn_ref[0, 0, :, sl]], axis=0)
        s = jnp.where(valid, _qkt(q, k), MASK_VALUE)
        m = jnp.max(s, axis=-1, keepdims=True)
        p = jnp.exp2(s - m)
        l = jnp.sum(p, axis=-1, keepdims=True)
        o_ref[0, 0, :, sl] = jnp.dot(p.astype(BF16), v, preferred_element_type=F32) / l
        lse_ref[0, 0, :, sl] = jnp.broadcast_to(m + jnp.log2(l), (Q_BLOCK, HEAD_DIM))


def dilated_group(qkv, cq, ck, cv):
    b, dil, class_len, _ = qkv.shape
    half = Q_BLOCK // 2
    n_half_blocks = class_len // half
    w = DIL_GROUP_WIDTH

    def cur(c):
        return pl.BlockSpec((1, 1, Q_BLOCK, w), lambda bb, r, i: (bb, r, i, c))

    def prev(c):
        return pl.BlockSpec((1, 1, half, w), lambda bb, r, i: (bb, r, jnp.maximum(2 * i - 1, 0), c))

    def nxt(c):
        return pl.BlockSpec((1, 1, half, w),
                            lambda bb, r, i: (bb, r, jnp.minimum(2 * i + 2, n_half_blocks - 1), c))

    out_spec = pl.BlockSpec((1, 1, Q_BLOCK, w), lambda bb, r, i: (bb, r, i, 0))
    out_sds = jax.ShapeDtypeStruct((b, dil, class_len, w), F32)
    return pl.pallas_call(
        functools.partial(_dilated_kernel, class_len=class_len),
        grid=(b, dil, class_len // Q_BLOCK),
        in_specs=[cur(cq), prev(ck), cur(ck), nxt(ck), prev(cv), cur(cv), nxt(cv)],
        out_specs=[out_spec, out_spec],
        out_shape=[out_sds, out_sds],
        compiler_params=_params("parallel", "parallel", "arbitrary"),
        name=f"dilated_attention_d{dil}",
    )(qkv, qkv, qkv, qkv, qkv, qkv, qkv)


def _dil_merge_kernel(o0_ref, l0_ref, o1_ref, l1_ref, o2_ref, l2_ref, out_ref,
                      so1, sl1, so2, sl2):
    for src_o, src_l, dst_o, dst_l in ((o1_ref, l1_ref, so1, sl1), (o2_ref, l2_ref, so2, sl2)):
        dil, n = src_o.shape[1], src_o.shape[2]
        for hh, sl in enumerate(_heads(DIL_GROUP_WIDTH)):
            for r in range(dil):
                dst_o[hh, pl.ds(r, n, stride=dil), :] = src_o[0, r, :, sl]
                dst_l[hh, pl.ds(r, n, stride=dil), :] = src_l[0, r, :, sl]
    for hh, sl in enumerate(_heads(DIL_GROUP_WIDTH)):
        a0, a1, a2 = l0_ref[0, 0, :, sl], sl1[hh], sl2[hh]
        m = jnp.maximum(jnp.maximum(a0, a1), a2)
        e0, e1, e2 = jnp.exp2(a0 - m), jnp.exp2(a1 - m), jnp.exp2(a2 - m)
        num = e0 * o0_ref[0, 0, :, sl] + e1 * so1[hh] + e2 * so2[hh]
        out_ref[0, :, sl] = (num / (e0 + e1 + e2)).astype(out_ref.dtype)


def dilated_merge(outs, tr=512):
    (o0, l0), (o1, l1), (o2, l2) = outs
    b, _, s_len, w = o0.shape

    def spec(a):
        dil = a.shape[1]
        return pl.BlockSpec((1, dil, tr // dil, w), lambda bb, i: (bb, 0, i, 0))

    slab = pltpu.VMEM((w // HEAD_DIM, tr, HEAD_DIM), F32)
    return pl.pallas_call(
        _dil_merge_kernel,
        grid=(b, s_len // tr),
        in_specs=[spec(o0), spec(l0), spec(o1), spec(l1), spec(o2), spec(l2)],
        out_specs=pl.BlockSpec((1, tr, w), lambda bb, i: (bb, i, 0)),
        out_shape=jax.ShapeDtypeStruct((b, s_len, w), BF16),
        scratch_shapes=[slab, slab, slab, slab],
        compiler_params=_params("parallel", "arbitrary"),
        name="dilated_merge",
    )(o0, l0, o1, l1, o2, l2)


def _diff_kernel(q_ref, k_ref, v_ref, lq1, lk1, lq2, lk2, g_ref, o_ref,
                 s_scr, p_scr, l_scr, *, lam_init):
    lam = (jnp.exp(jnp.sum(lq1[...] * lk1[...], axis=-1, keepdims=True))
           - jnp.exp(jnp.sum(lq2[...] * lk2[...], axis=-1, keepdims=True)) + lam_init)
    tq = q_ref.shape[1]

    def component(c):
        sl = slice(c * HEAD_DIM, (c + 1) * HEAD_DIM)
        s_scr[c] = _qkt(q_ref[0, :, sl], k_ref[0, :, sl])
        for r in range(tq // BF16_ROWS):
            rows = slice(r * BF16_ROWS, (r + 1) * BF16_ROWS)
            m = jnp.max(s_scr[c, rows, :], axis=-1, keepdims=True)
            p = jnp.exp2(s_scr[c, rows, :] - m)
            l_scr[c, rows, :] = jnp.broadcast_to(jnp.sum(p, axis=-1, keepdims=True),
                                                 (BF16_ROWS, HEAD_DIM))
            p_scr[c, rows, :] = p.astype(BF16)
        return jnp.dot(p_scr[c], v_ref[0], preferred_element_type=F32) / l_scr[c, :, :1]

    o = component(0) - lam * component(1)
    o_ref[0] = (_rms(o, g_ref[...]) * (1.0 - lam_init)).astype(o_ref.dtype)


def differential_attention(proj, lq1, lk1, lq2, lk2, subln, lam_init, tq=512):
    b, s, _ = proj.shape
    vec = pl.BlockSpec((1, HEAD_DIM), lambda bb, h, i: (0, 0))
    cq, ck, cv = P_QB // DIFF_V_DIM, P_KB // DIFF_V_DIM, P_VB // DIFF_V_DIM
    return pl.pallas_call(
        functools.partial(_diff_kernel, lam_init=lam_init),
        grid=(b, DIFF_HEADS, s // tq),
        in_specs=[pl.BlockSpec((1, tq, DIFF_V_DIM), lambda bb, h, i: (bb, i, cq + h)),
                  pl.BlockSpec((1, s, DIFF_V_DIM), lambda bb, h, i: (bb, 0, ck + h)),
                  pl.BlockSpec((1, s, DIFF_V_DIM), lambda bb, h, i: (bb, 0, cv + h)),
                  vec, vec, vec, vec,
                  pl.BlockSpec((1, DIFF_V_DIM), lambda bb, h, i: (0, 0))],
        out_specs=pl.BlockSpec((1, tq, DIFF_V_DIM), lambda bb, h, i: (bb, i, h)),
        out_shape=jax.ShapeDtypeStruct((b, s, DIFF_V_WIDTH), BF16),
        scratch_shapes=[pltpu.VMEM((2, tq, s), F32), pltpu.VMEM((2, tq, s), BF16),
                        pltpu.VMEM((2, tq, HEAD_DIM), F32)],
        compiler_params=_params("parallel", "parallel", "arbitrary"),
        name="differential_attention",
    )(proj, proj, proj, lq1.reshape(1, -1), lk1.reshape(1, -1), lq2.reshape(1, -1),
      lk2.reshape(1, -1), subln.reshape(1, -1))


def _gate_kernel(oa_ref, wa_ref, ob_ref, wb_ref, ga_ref, gb_ref, o_ref):
    ya = jnp.dot(oa_ref[...], wa_ref[...].astype(BF16), preferred_element_type=F32)
    yb = jnp.dot(ob_ref[...], wb_ref[...].astype(BF16), preferred_element_type=F32)
    o = (jax.nn.sigmoid(ga_ref[...].astype(F32)) * ya
         + jax.nn.sigmoid(gb_ref[...].astype(F32)) * yb)
    o_ref[...] = o.astype(o_ref.dtype)


def gated_merge(out_a, w_a, out_b, w_b, proj2d, tm=1024, tn=512):
    t = out_a.shape[0]
    n = w_a.shape[1]
    ca = P_GA // tn
    cb = (P_GA + n) // tn
    return pl.pallas_call(
        _gate_kernel,
        grid=(t // tm, n // tn),
        in_specs=[pl.BlockSpec((tm, out_a.shape[1]), lambda i, j: (i, 0)),
                  pl.BlockSpec((w_a.shape[0], tn), lambda i, j: (0, j)),
                  pl.BlockSpec((tm, out_b.shape[1]), lambda i, j: (i, 0)),
                  pl.BlockSpec((w_b.shape[0], tn), lambda i, j: (0, j)),
                  pl.BlockSpec((tm, tn), lambda i, j: (i, ca + j)),
                  pl.BlockSpec((tm, tn), lambda i, j: (i, cb + j))],
        out_specs=pl.BlockSpec((tm, tn), lambda i, j: (i, j)),
        out_shape=jax.ShapeDtypeStruct((t, n), BF16),
        compiler_params=_params("parallel", "arbitrary"),
        name="gated_merge",
    )(out_a, w_a, out_b, w_b, proj2d, proj2d)


def _resnorm_kernel(y_ref, x_ref, gp_ref, gn_ref, xo_ref, ho_ref):
    xn = x_ref[...] + _rms(y_ref[...], gp_ref[...])
    xo_ref[...] = xn
    ho_ref[...] = _rms(xn, gn_ref[...]).astype(ho_ref.dtype)


def _resnorm_last_kernel(y_ref, x_ref, gp_ref, xo_ref):
    xo_ref[...] = x_ref[...] + _rms(y_ref[...], gp_ref[...])


def residual_norm(y, x, g_post, g_next=None, tr=256):
    t, d = x.shape
    row = pl.BlockSpec((tr, d), lambda i: (i, 0))
    vec = pl.BlockSpec((1, d), lambda i: (0, 0))
    if g_next is None:
        return pl.pallas_call(
            _resnorm_last_kernel,
            grid=(t // tr,),
            in_specs=[row, row, vec],
            out_specs=row,
            out_shape=jax.ShapeDtypeStruct((t, d), F32),
            compiler_params=_params("parallel"),
            name="residual_norm_last",
        )(y, x, g_post.reshape(1, d))
    return pl.pallas_call(
        _resnorm_kernel,
        grid=(t // tr,),
        in_specs=[row, row, vec, vec],
        out_specs=[row, row],
        out_shape=[jax.ShapeDtypeStruct((t, d), F32), jax.ShapeDtypeStruct((t, d), BF16)],
        compiler_params=_params("parallel"),
        name="residual_norm",
    )(y, x, g_post.reshape(1, d), g_next.reshape(1, d))


def _mem_attn_kernel(q_ref, kv_ref, o_ref):
    for hh, sl in enumerate(_heads(MEM_WIDTH)):
        slv = slice(MEM_WIDTH + hh * HEAD_DIM, MEM_WIDTH + (hh + 1) * HEAD_DIM)
        s = _qkt(q_ref[0, :, sl], kv_ref[0, :, sl])
        p = jnp.exp2(s - jnp.max(s, axis=-1, keepdims=True))
        l = jnp.sum(p, axis=-1, keepdims=True)
        o = jnp.dot(p.astype(BF16), kv_ref[0, :, slv], preferred_element_type=F32) / l
        o_ref[0, :, sl] = o.astype(o_ref.dtype)


def memory_attention(q, kv, tq=512):
    b, s, w = q.shape
    m = kv.shape[1]
    return pl.pallas_call(
        _mem_attn_kernel,
        grid=(b, s // tq),
        in_specs=[pl.BlockSpec((1, tq, w), lambda bb, i: (bb, i, 0)),
                  pl.BlockSpec((1, m, 2 * w), lambda bb, i: (bb, 0, 0))],
        out_specs=pl.BlockSpec((1, tq, w), lambda bb, i: (bb, i, 0)),
        out_shape=jax.ShapeDtypeStruct((b, s, w), BF16),
        compiler_params=_params("parallel", "arbitrary"),
        name="memory_attention",
    )(q, kv)


def kernel(x, mem, positions, norm_mix_pre, w_in, w_a, w_b, w_mix_out, norm_mix_post,
           lambda_q1, lambda_k1, lambda_q2, lambda_k2, diff_subln,
           norm_mem_pre, norm_mem_kv, w_mem_q, w_mem_kv, w_mem_o, norm_mem_post,
           norm_mlp_pre, w_mlp_up, w_mlp_down, norm_mlp_post):
    b, s, d = x.shape
    t = b * s
    depth = w_in.shape[0]
    m_len = mem.shape[1]
    xt = x.reshape(t, d)
    memt = mem.reshape(b * m_len, d)
    cos, sin = rope_tables(positions)

    h = rmsnorm(xt, norm_mix_pre[0])
    for layer in range(depth):
        lam_init = 0.8 - 0.6 * float(np.exp(-0.3 * layer))

        w_in_b = w_in[layer]
        proj = in_projection(h, w_in_b, cos, sin)
        proj3 = proj.reshape(b, s, -1)
        tile = DIL_GROUP_WIDTH
        outs = [dilated_group(proj3.reshape(b, 1, s, -1), P_QA // tile, P_KA // tile, P_VA // tile)]
        for g, (_, dil) in enumerate(DIL_CONFIGS):
            if dil > 1:
                qkv = in_projection_dilated(h, w_in_b, cos, sin, g, dil, b)
                outs.append(dilated_group(qkv, 0, 1, 2))
        out_a = dilated_merge(outs).reshape(t, -1)
        out_b = differential_attention(proj3, lambda_q1[layer], lambda_k1[layer],
                                       lambda_q2[layer], lambda_k2[layer],
                                       diff_subln[layer], lam_init).reshape(t, -1)
        merged = gated_merge(out_a, w_a[layer], out_b, w_b[layer], proj)
        y = matmul(merged, w_mix_out[layer], F32, 1024, 512, name="mix_out")
        xt, h = residual_norm(y, xt, norm_mix_post[layer], norm_mem_pre[layer])

        mn = rmsnorm(memt, norm_mem_kv[layer])
        kv = matmul(mn, w_mem_kv[layer], BF16, b * m_len, 512, name="mem_kv")
        q = matmul(h, w_mem_q[layer], BF16, 1024, MEM_WIDTH, epilogue="q_prescale",
                   name="mem_q")
        o = memory_attention(q.reshape(b, s, -1), kv.reshape(b, m_len, -1)).reshape(t, -1)
        y = matmul(o, w_mem_o[layer], F32, 1024, 1024, name="mem_out")
        xt, h = residual_norm(y, xt, norm_mem_post[layer], norm_mlp_pre[layer])

        u = matmul(h, w_mlp_up[layer], BF16, 1024, 512, epilogue="relu2", name="mlp_up")
        y = matmul_kgrid(u, w_mlp_down[layer], 2048, 1024, 1024, name="mlp_down")
        if layer + 1 < depth:
            xt, h = residual_norm(y, xt, norm_mlp_post[layer], norm_mix_pre[layer + 1])
        else:
            xt = residual_norm(y, xt, norm_mlp_post[layer])
    return xt.reshape(b, s, d)
```

```python
import functools

import numpy as np
import jax
import jax.numpy as jnp
from jax import lax
from jax.experimental import pallas as pl
from jax.experimental.pallas import tpu as pltpu

F32 = jnp.float32
BF16 = jnp.bfloat16

HEAD_DIM = 128
BF16_ROWS = 16
SOFTMAX_CHUNK = 1024
DIL_CONFIGS = ((128, 1), (512, 4), (2048, 16))
N_DIL_GROUPS = 3
DIL_HEADS = 8
DIL_GROUP_WIDTH = DIL_HEADS * HEAD_DIM
DIL_WIDTH = N_DIL_GROUPS * DIL_GROUP_WIDTH
DIFF_HEADS = 8
DIFF_QK_WIDTH = DIFF_HEADS * 2 * HEAD_DIM
DIFF_V_DIM = 2 * HEAD_DIM
DIFF_V_WIDTH = DIFF_HEADS * DIFF_V_DIM
MEM_HEADS = 4
MEM_WIDTH = MEM_HEADS * HEAD_DIM
ROPE_THETA = 10000.0
Q_BLOCK = 128
NORM_EPS = 1e-6
MASK_VALUE = -1e30
LOG2E = 1.4426950408889634
Q_PRESCALE = HEAD_DIM ** -0.5 * LOG2E

W_QA = 0
W_KA = W_QA + DIL_WIDTH
W_VA = W_KA + DIL_WIDTH
W_QB = W_VA + DIL_WIDTH

PROJ_TN = 512
P_QA = 0
P_KA = P_QA + DIL_GROUP_WIDTH
P_VA = P_KA + DIL_GROUP_WIDTH
P_QB = P_VA + DIL_GROUP_WIDTH
P_KB = P_QB + DIFF_QK_WIDTH
P_VB = P_KB + DIFF_QK_WIDTH
P_GA = P_VB + DIFF_V_WIDTH

V7X_VMEM_BYTES = 64 * 1024 * 1024
VMEM_LIMIT = V7X_VMEM_BYTES - 6 * 1024 * 1024


def _params(*sem):
    return pltpu.CompilerParams(dimension_semantics=sem, vmem_limit_bytes=VMEM_LIMIT)


def _rms(x, g):
    return x * lax.rsqrt(jnp.mean(x * x, axis=-1, keepdims=True) + NORM_EPS) * g


def _heads(width):
    return [slice(hh * HEAD_DIM, (hh + 1) * HEAD_DIM) for hh in range(width // HEAD_DIM)]


def _qkt(q, k):
    return lax.dot_general(q, k, (((1,), (1,)), ((), ())), preferred_element_type=F32)


def _rmsnorm_kernel(x_ref, g_ref, o_ref):
    o_ref[...] = _rms(x_ref[...], g_ref[...]).astype(o_ref.dtype)


def rmsnorm(x, g, tr=256):
    t, d = x.shape
    return pl.pallas_call(
        _rmsnorm_kernel,
        grid=(t // tr,),
        in_specs=[pl.BlockSpec((tr, d), lambda i: (i, 0)),
                  pl.BlockSpec((1, d), lambda i: (0, 0))],
        out_specs=pl.BlockSpec((tr, d), lambda i: (i, 0)),
        out_shape=jax.ShapeDtypeStruct((t, d), BF16),
        compiler_params=_params("parallel"),
        name="rmsnorm",
    )(x, g.reshape(1, d))


def _rope_table_kernel(pos_ref, inv_ref, sign_ref, cos_ref, sin_ref):
    ang = pos_ref[...] * inv_ref[...]
    cos_ref[...] = jnp.cos(ang)
    sin_ref[...] = jnp.sin(ang) * sign_ref[...]


def rope_tables(positions, tr=1024):
    t = positions.size
    half = HEAD_DIM // 2
    inv = ROPE_THETA ** (-jnp.arange(0, HEAD_DIM, 2, dtype=F32) / HEAD_DIM)
    inv2 = jnp.concatenate([inv, inv]).reshape(1, HEAD_DIM)
    sign = jnp.concatenate([-jnp.ones((half,), F32), jnp.ones((half,), F32)]).reshape(1, HEAD_DIM)
    pos = positions.astype(F32).reshape(t, 1)
    return pl.pallas_call(
        _rope_table_kernel,
        grid=(t // tr,),
        in_specs=[pl.BlockSpec((tr, 1), lambda i: (i, 0)),
                  pl.BlockSpec((1, HEAD_DIM), lambda i: (0, 0)),
                  pl.BlockSpec((1, HEAD_DIM), lambda i: (0, 0))],
        out_specs=[pl.BlockSpec((tr, HEAD_DIM), lambda i: (i, 0))] * 2,
        out_shape=[jax.ShapeDtypeStruct((t, HEAD_DIM), F32)] * 2,
        compiler_params=_params("parallel"),
        name="rope_tables",
    )(pos, inv2, sign)


def _rope(xh, c, s):
    return xh * c + pltpu.roll(xh, HEAD_DIM // 2, 1) * s


def _inproj_kernel(h_ref, w_ref, cos_ref, sin_ref, o_ref):
    j = pl.program_id(1)
    acc = jnp.dot(h_ref[...], w_ref[...].astype(BF16), preferred_element_type=F32)
    is_q = (j < P_KA // PROJ_TN) | ((j >= P_QB // PROJ_TN) & (j < P_KB // PROJ_TN))
    is_k = (((j >= P_KA // PROJ_TN) & (j < P_VA // PROJ_TN))
            | ((j >= P_KB // PROJ_TN) & (j < P_VB // PROJ_TN)))

    @pl.when(is_q | is_k)
    def _():
        f = jnp.where(is_q, Q_PRESCALE, 1.0).astype(F32)
        c = cos_ref[...] * f
        s = sin_ref[...] * f
        for sl in _heads(PROJ_TN):
            o_ref[:, sl] = _rope(acc[:, sl], c, s).astype(o_ref.dtype)

    @pl.when(jnp.logical_not(is_q | is_k))
    def _():
        o_ref[...] = acc.astype(o_ref.dtype)


def in_projection(h, w, cos, sin, tm=1024):
    t, k = h.shape
    tn = PROJ_TN
    n_out = w.shape[1] - 3 * (DIL_WIDTH - DIL_GROUP_WIDTH)
    per = DIL_GROUP_WIDTH // tn
    n_group0 = 3 * per
    stride_a = DIL_WIDTH // tn
    skip = W_QB // tn - n_group0

    def w_map(i, j):
        return (0, jnp.where(j < n_group0, (j // per) * stride_a + j % per, j + skip))

    return pl.pallas_call(
        _inproj_kernel,
        grid=(t // tm, n_out // tn),
        in_specs=[pl.BlockSpec((tm, k), lambda i, j: (i, 0)),
                  pl.BlockSpec((k, tn), w_map),
                  pl.BlockSpec((tm, HEAD_DIM), lambda i, j: (i, 0)),
                  pl.BlockSpec((tm, HEAD_DIM), lambda i, j: (i, 0))],
        out_specs=pl.BlockSpec((tm, tn), lambda i, j: (i, j)),
        out_shape=jax.ShapeDtypeStruct((t, n_out), BF16),
        compiler_params=_params("parallel", "arbitrary"),
        name="in_projection",
    )(h, w, cos, sin)


def _inproj_dilated_kernel(h_ref, w_ref, cos_ref, sin_ref, o_ref, slab_ref, *, dil):
    seg = pl.program_id(1) // (DIL_GROUP_WIDTH // PROJ_TN)
    tm = h_ref.shape[0]
    acc = jnp.dot(h_ref[...], w_ref[...].astype(BF16), preferred_element_type=F32)
    f = jnp.where(seg == 0, Q_PRESCALE, 1.0).astype(F32)
    c = jnp.where(seg < 2, cos_ref[...] * f, 1.0)
    s = jnp.where(seg < 2, sin_ref[...] * f, 0.0)
    for hh, sl in enumerate(_heads(PROJ_TN)):
        slab_ref[hh] = _rope(acc[:, sl], c, s)
        for r in range(dil):
            o_ref[0, r, :, sl] = slab_ref[hh, pl.ds(r, tm // dil, stride=dil), :].astype(o_ref.dtype)


def in_projection_dilated(h, w, cos, sin, g, dil, batch, tm=1024):
    t, k = h.shape
    tn = PROJ_TN
    s_len = t // batch
    tiles_per_seq = s_len // tm
    per = DIL_GROUP_WIDTH // tn
    stride_a = DIL_WIDTH // tn
    return pl.pallas_call(
        functools.partial(_inproj_dilated_kernel, dil=dil),
        grid=(t // tm, 3 * per),
        in_specs=[pl.BlockSpec((tm, k), lambda i, j: (i, 0)),
                  pl.BlockSpec((k, tn), lambda i, j: (0, (j // per) * stride_a + g * per + j % per)),
                  pl.BlockSpec((tm, HEAD_DIM), lambda i, j: (i, 0)),
                  pl.BlockSpec((tm, HEAD_DIM), lambda i, j: (i, 0))],
        out_specs=pl.BlockSpec((1, dil, tm // dil, tn),
                               lambda i, j: (i // tiles_per_seq, 0, i % tiles_per_seq, j)),
        out_shape=jax.ShapeDtypeStruct((batch, dil, s_len // dil, 3 * DIL_GROUP_WIDTH), BF16),
        scratch_shapes=[pltpu.VMEM((tn // HEAD_DIM, tm, HEAD_DIM), F32)],
        compiler_params=_params("parallel", "arbitrary"),
        name=f"in_projection_d{dil}",
    )(h, w, cos, sin)


def _mm_kernel(a_ref, w_ref, o_ref, *, epilogue):
    acc = jnp.dot(a_ref[...], w_ref[...].astype(BF16), preferred_element_type=F32)
    if epilogue == "relu2":
        acc = jnp.square(jnp.maximum(acc, 0.0))
    elif epilogue == "q_prescale":
        acc = acc * Q_PRESCALE
    o_ref[...] = acc.astype(o_ref.dtype)


def matmul(a, w, out_dtype, tm, tn, epilogue=None, name="matmul"):
    m, k = a.shape
    n = w.shape[1]
    return pl.pallas_call(
        functools.partial(_mm_kernel, epilogue=epilogue),
        grid=(m // tm, n // tn),
        in_specs=[pl.BlockSpec((tm, k), lambda i, j: (i, 0)),
                  pl.BlockSpec((k, tn), lambda i, j: (0, j))],
        out_specs=pl.BlockSpec((tm, tn), lambda i, j: (i, j)),
        out_shape=jax.ShapeDtypeStruct((m, n), out_dtype),
        compiler_params=_params("parallel", "arbitrary"),
        name=name,
    )(a, w)


def _mm_kgrid_kernel(a_ref, w_ref, o_ref, acc_ref):
    kk = pl.program_id(2)

    @pl.when(kk == 0)
    def _():
        acc_ref[...] = jnp.zeros_like(acc_ref)

    acc_ref[...] += jnp.dot(a_ref[...], w_ref[...].astype(BF16), preferred_element_type=F32)

    @pl.when(kk == pl.num_programs(2) - 1)
    def _():
        o_ref[...] = acc_ref[...].astype(o_ref.dtype)


def matmul_kgrid(a, w, out_dtype, tm, tn, tk, name="matmul_kgrid"):
    m, k = a.shape
    n = w.shape[1]
    return pl.pallas_call(
        _mm_kgrid_kernel,
        grid=(m // tm, n // tn, k // tk),
        in_specs=[pl.BlockSpec((tm, tk), lambda i, j, kk: (i, kk)),
                  pl.BlockSpec((tk, tn), lambda i, j, kk: (kk, j))],
        out_specs=pl.BlockSpec((tm, tn), lambda i, j, kk: (i, j)),
        out_shape=jax.ShapeDtypeStruct((m, n), out_dtype),
        scratch_shapes=[pltpu.VMEM((tm, tn), F32)],
        compiler_params=_params("parallel", "parallel", "arbitrary"),
        name=name,
    )(a, w)


def _dilated_kernel(q_ref, kp_ref, kc_ref, kn_ref, vp_ref, vc_ref, vn_ref,
                    o_ref, lse_ref, *, class_len):
    i = pl.program_id(2)
    half = Q_BLOCK // 2
    nk = 2 * Q_BLOCK
    row = lax.broadcasted_iota(jnp.int32, (Q_BLOCK, nk), 0)
    col = lax.broadcasted_iota(jnp.int32, (Q_BLOCK, nk), 1)
    rel = col - half - row
    kpos = i * Q_BLOCK - half + col
    valid = (rel >= -half) & (rel <= half) & (kpos >= 0) & (kpos < class_len)
    for sl in _heads(DIL_GROUP_WIDTH):
        q = q_ref[0, 0, :, sl]
        k = jnp.concatenate([kp_ref[0, 0, :, sl], kc_ref[0, 0, :, sl], kn_ref[0, 0, :, sl]], axis=0)
        v = jnp.concatenate([vp_ref[0, 0, :, sl], vc_ref[0, 0, :, sl], vn_ref[0, 0, :, sl]], axis=0)
        s = jnp.where(valid, _qkt(q, k), MASK_VALUE)
        m = jnp.max(s, axis=-1, keepdims=True)
        p = jnp.exp2(s - m)
        l = jnp.sum(p, axis=-1, keepdims=True)
        o_ref[0, 0, :, sl] = jnp.dot(p.astype(BF16), v, preferred_element_type=F32) / l
        lse_ref[0, 0, :, sl] = jnp.broadcast_to(m + jnp.log2(l), (Q_BLOCK, HEAD_DIM))


def dilated_group(qkv, cq, ck, cv):
    b, dil, class_len, _ = qkv.shape
    half = Q_BLOCK // 2
    n_half_blocks = class_len // half
    w = DIL_GROUP_WIDTH

    def cur(c):
        return pl.BlockSpec((1, 1, Q_BLOCK, w), lambda bb, r, i: (bb, r, i, c))

    def prev(c):
        return pl.BlockSpec((1, 1, half, w), lambda bb, r, i: (bb, r, jnp.maximum(2 * i - 1, 0), c))

    def nxt(c):
        return pl.BlockSpec((1, 1, half, w),
                            lambda bb, r, i: (bb, r, jnp.minimum(2 * i + 2, n_half_blocks - 1), c))

    out_spec = pl.BlockSpec((1, 1, Q_BLOCK, w), lambda bb, r, i: (bb, r, i, 0))
    out_sds = jax.ShapeDtypeStruct((b, dil, class_len, w), F32)
    return pl.pallas_call(
        functools.partial(_dilated_kernel, class_len=class_len),
        grid=(b, dil, class_len // Q_BLOCK),
        in_specs=[cur(cq), prev(ck), cur(ck), nxt(ck), prev(cv), cur(cv), nxt(cv)],
        out_specs=[out_spec, out_spec],
        out_shape=[out_sds, out_sds],
        compiler_params=_params("parallel", "parallel", "arbitrary"),
        name=f"dilated_attention_d{dil}",
    )(qkv, qkv, qkv, qkv, qkv, qkv, qkv)


def _dil_merge_kernel(o0_ref, l0_ref, o1_ref, l1_ref, o2_ref, l2_ref, out_ref,
                      so1, sl1, so2, sl2):
    for src_o, src_l, dst_o, dst_l in ((o1_ref, l1_ref, so1, sl1), (o2_ref, l2_ref, so2, sl2)):
        dil, n = src_o.shape[1], src_o.shape[2]
        for hh, sl in enumerate(_heads(DIL_GROUP_WIDTH)):
            for r in range(dil):
                dst_o[hh, pl.ds(r, n, stride=dil), :] = src_o[0, r, :, sl]
                dst_l[hh, pl.ds(r, n, stride=dil), :] = src_l[0, r, :, sl]
    for hh, sl in enumerate(_heads(DIL_GROUP_WIDTH)):
        a0, a1, a2 = l0_ref[0, 0, :, sl], sl1[hh], sl2[hh]
        m = jnp.maximum(jnp.maximum(a0, a1), a2)
        e0, e1, e2 = jnp.exp2(a0 - m), jnp.exp2(a1 - m), jnp.exp2(a2 - m)
        num = e0 * o0_ref[0, 0, :, sl] + e1 * so1[hh] + e2 * so2[hh]
        out_ref[0, :, sl] = (num / (e0 + e1 + e2)).astype(out_ref.dtype)


def dilated_merge(outs, tr=512):
    (o0, l0), (o1, l1), (o2, l2) = outs
    b, _, s_len, w = o0.shape

    def spec(a):
        dil = a.shape[1]
        return pl.BlockSpec((1, dil, tr // dil, w), lambda bb, i: (bb, 0, i, 0))

    slab = pltpu.VMEM((w // HEAD_DIM, tr, HEAD_DIM), F32)
    return pl.pallas_call(
        _dil_merge_kernel,
        grid=(b, s_len // tr),
        in_specs=[spec(o0), spec(l0), spec(o1), spec(l1), spec(o2), spec(l2)],
        out_specs=pl.BlockSpec((1, tr, w), lambda bb, i: (bb, i, 0)),
        out_shape=jax.ShapeDtypeStruct((b, s_len, w), BF16),
        scratch_shapes=[slab, slab, slab, slab],
        compiler_params=_params("parallel", "arbitrary"),
        name="dilated_merge",
    )(o0, l0, o1, l1, o2, l2)


def _diff_kernel(q_ref, k_ref, v_ref, lq1, lk1, lq2, lk2, g_ref, o_ref,
                 s_scr, p_scr, l_scr, *, lam_init):
    lam = (jnp.exp(jnp.sum(lq1[...] * lk1[...], axis=-1, keepdims=True))
           - jnp.exp(jnp.sum(lq2[...] * lk2[...], axis=-1, keepdims=True)) + lam_init)
    th = q_ref.shape[1] // 2
    n_keys = k_ref.shape[1]

    def scores(half, c, buf):
        sl = slice(c * HEAD_DIM, (c + 1) * HEAD_DIM)
        s_scr[buf] = _qkt(q_ref[0, half * th:(half + 1) * th, sl], k_ref[0, :, sl])

    def softmax(buf):
        for r in range(th // BF16_ROWS):
            rows = slice(r * BF16_ROWS, (r + 1) * BF16_ROWS)
            m = jnp.max(s_scr[buf, rows, :], axis=-1, keepdims=True)
            l = jnp.zeros((BF16_ROWS, 1), F32)
            for c0 in range(0, n_keys, SOFTMAX_CHUNK):
                cols = slice(c0, c0 + SOFTMAX_CHUNK)
                p = jnp.exp2(s_scr[buf, rows, cols] - m)
                l = l + jnp.sum(p, axis=-1, keepdims=True)
                p_scr[buf, rows, cols] = p.astype(BF16)
            l_scr[buf, rows, :] = jnp.broadcast_to(l, (BF16_ROWS, HEAD_DIM))

    def values(buf):
        return jnp.dot(p_scr[buf], v_ref[0], preferred_element_type=F32) / l_scr[buf, :, :1]

    def finish(half, o1, o2):
        o = o1 - lam * o2
        o_ref[0, half * th:(half + 1) * th, :] = (
            _rms(o, g_ref[...]) * (1.0 - lam_init)).astype(o_ref.dtype)

    scores(0, 0, 0)
    scores(0, 1, 1)
    softmax(0)
    scores(1, 0, 0)
    softmax(1)
    o_a1 = values(0)
    scores(1, 1, 1)
    softmax(0)
    o_a2 = values(1)
    finish(0, o_a1, o_a2)
    softmax(1)
    o_b1 = values(0)
    o_b2 = values(1)
    finish(1, o_b1, o_b2)


def differential_attention(proj, lq1, lk1, lq2, lk2, subln, lam_init, tq=512):
    b, s, _ = proj.shape
    vec = pl.BlockSpec((1, HEAD_DIM), lambda bb, h, i: (0, 0))
    cq, ck, cv = P_QB // DIFF_V_DIM, P_KB // DIFF_V_DIM, P_VB // DIFF_V_DIM
    return pl.pallas_call(
        functools.partial(_diff_kernel, lam_init=lam_init),
        grid=(b, DIFF_HEADS, s // tq),
        in_specs=[pl.BlockSpec((1, tq, DIFF_V_DIM), lambda bb, h, i: (bb, i, cq + h)),
                  pl.BlockSpec((1, s, DIFF_V_DIM), lambda bb, h, i: (bb, 0, ck + h)),
                  pl.BlockSpec((1, s, DIFF_V_DIM), lambda bb, h, i: (bb, 0, cv + h)),
                  vec, vec, vec, vec,
                  pl.BlockSpec((1, DIFF_V_DIM), lambda bb, h, i: (0, 0))],
        out_specs=pl.BlockSpec((1, tq, DIFF_V_DIM), lambda bb, h, i: (bb, i, h)),
        out_shape=jax.ShapeDtypeStruct((b, s, DIFF_V_WIDTH), BF16),
        scratch_shapes=[pltpu.VMEM((2, tq // 2, s), F32), pltpu.VMEM((2, tq // 2, s), BF16),
                        pltpu.VMEM((2, tq // 2, HEAD_DIM), F32)],
        compiler_params=_params("parallel", "parallel", "arbitrary"),
        name="differential_attention",
    )(proj, proj, proj, lq1.reshape(1, -1), lk1.reshape(1, -1), lq2.reshape(1, -1),
      lk2.reshape(1, -1), subln.reshape(1, -1))


def _gate_kernel(oa_ref, wa_ref, ob_ref, wb_ref, ga_ref, gb_ref, o_ref):
    ya = jnp.dot(oa_ref[...], wa_ref[...].astype(BF16), preferred_element_type=F32)
    yb = jnp.dot(ob_ref[...], wb_ref[...].astype(BF16), preferred_element_type=F32)
    o = (jax.nn.sigmoid(ga_ref[...].astype(F32)) * ya
         + jax.nn.sigmoid(gb_ref[...].astype(F32)) * yb)
    o_ref[...] = o.astype(o_ref.dtype)


def gated_merge(out_a, w_a, out_b, w_b, proj2d, tm=1024, tn=512):
    t = out_a.shape[0]
    n = w_a.shape[1]
    ca = P_GA // tn
    cb = (P_GA + n) // tn
    return pl.pallas_call(
        _gate_kernel,
        grid=(t // tm, n // tn),
        in_specs=[pl.BlockSpec((tm, out_a.shape[1]), lambda i, j: (i, 0)),
                  pl.BlockSpec((w_a.shape[0], tn), lambda i, j: (0, j)),
                  pl.BlockSpec((tm, out_b.shape[1]), lambda i, j: (i, 0)),
                  pl.BlockSpec((w_b.shape[0], tn), lambda i, j: (0, j)),
                  pl.BlockSpec((tm, tn), lambda i, j: (i, ca + j)),
                  pl.BlockSpec((tm, tn), lambda i, j: (i, cb + j))],
        out_specs=pl.BlockSpec((tm, tn), lambda i, j: (i, j)),
        out_shape=jax.ShapeDtypeStruct((t, n), BF16),
        compiler_params=_params("parallel", "arbitrary"),
        name="gated_merge",
    )(out_a, w_a, out_b, w_b, proj2d, proj2d)


def _resnorm_kernel(y_ref, x_ref, gp_ref, gn_ref, xo_ref, ho_ref):
    xn = x_ref[...] + _rms(y_ref[...].astype(F32), gp_ref[...])
    xo_ref[...] = xn
    ho_ref[...] = _rms(xn, gn_ref[...]).astype(ho_ref.dtype)


def _resnorm_last_kernel(y_ref, x_ref, gp_ref, xo_ref):
    xo_ref[...] = x_ref[...] + _rms(y_ref[...].astype(F32), gp_ref[...])


def residual_norm(y, x, g_post, g_next=None, tr=256):
    t, d = x.shape
    row = pl.BlockSpec((tr, d), lambda i: (i, 0))
    vec = pl.BlockSpec((1, d), lambda i: (0, 0))
    if g_next is None:
        return pl.pallas_call(
            _resnorm_last_kernel,
            grid=(t // tr,),
            in_specs=[row, row, vec],
            out_specs=row,
            out_shape=jax.ShapeDtypeStruct((t, d), F32),
            compiler_params=_params("parallel"),
            name="residual_norm_last",
        )(y, x, g_post.reshape(1, d))
    return pl.pallas_call(
        _resnorm_kernel,
        grid=(t // tr,),
        in_specs=[row, row, vec, vec],
        out_specs=[row, row],
        out_shape=[jax.ShapeDtypeStruct((t, d), F32), jax.ShapeDtypeStruct((t, d), BF16)],
        compiler_params=_params("parallel"),
        name="residual_norm",
    )(y, x, g_post.reshape(1, d), g_next.reshape(1, d))


def _mem_attn_kernel(q_ref, kv_ref, o_ref):
    for hh, sl in enumerate(_heads(MEM_WIDTH)):
        slv = slice(MEM_WIDTH + hh * HEAD_DIM, MEM_WIDTH + (hh + 1) * HEAD_DIM)
        s = _qkt(q_ref[0, :, sl], kv_ref[0, :, sl])
        p = jnp.exp2(s - jnp.max(s, axis=-1, keepdims=True))
        l = jnp.sum(p, axis=-1, keepdims=True)
        o = jnp.dot(p.astype(BF16), kv_ref[0, :, slv], preferred_element_type=F32) / l
        o_ref[0, :, sl] = o.astype(o_ref.dtype)


def memory_attention(q, kv, tq=512):
    b, s, w = q.shape
    m = kv.shape[1]
    return pl.pallas_call(
        _mem_attn_kernel,
        grid=(b, s // tq),
        in_specs=[pl.BlockSpec((1, tq, w), lambda bb, i: (bb, i, 0)),
                  pl.BlockSpec((1, m, 2 * w), lambda bb, i: (bb, 0, 0))],
        out_specs=pl.BlockSpec((1, tq, w), lambda bb, i: (bb, i, 0)),
        out_shape=jax.ShapeDtypeStruct((b, s, w), BF16),
        compiler_params=_params("parallel", "arbitrary"),
        name="memory_attention",
    )(q, kv)


def kernel(x, mem, positions, norm_mix_pre, w_in, w_a, w_b, w_mix_out, norm_mix_post,
           lambda_q1, lambda_k1, lambda_q2, lambda_k2, diff_subln,
           norm_mem_pre, norm_mem_kv, w_mem_q, w_mem_kv, w_mem_o, norm_mem_post,
           norm_mlp_pre, w_mlp_up, w_mlp_down, norm_mlp_post):
    b, s, d = x.shape
    t = b * s
    depth = w_in.shape[0]
    m_len = mem.shape[1]
    xt = x.reshape(t, d)
    memt = mem.reshape(b * m_len, d)
    cos, sin = rope_tables(positions)

    h = rmsnorm(xt, norm_mix_pre[0])
    for layer in range(depth):
        lam_init = 0.8 - 0.6 * float(np.exp(-0.3 * layer))

        w_in_b = w_in[layer]
        proj = in_projection(h, w_in_b, cos, sin)
        proj3 = proj.reshape(b, s, -1)
        tile = DIL_GROUP_WIDTH
        outs = [dilated_group(proj3.reshape(b, 1, s, -1), P_QA // tile, P_KA // tile, P_VA // tile)]
        for g, (_, dil) in enumerate(DIL_CONFIGS):
            if dil > 1:
                qkv = in_projection_dilated(h, w_in_b, cos, sin, g, dil, b)
                outs.append(dilated_group(qkv, 0, 1, 2))
        out_a = dilated_merge(outs).reshape(t, -1)
        out_b = differential_attention(proj3, lambda_q1[layer], lambda_k1[layer],
                                       lambda_q2[layer], lambda_k2[layer],
                                       diff_subln[layer], lam_init).reshape(t, -1)
        merged = gated_merge(out_a, w_a[layer], out_b, w_b[layer], proj)
        y = matmul(merged, w_mix_out[layer], BF16, 1024, 512, name="mix_out")
        xt, h = residual_norm(y, xt, norm_mix_post[layer], norm_mem_pre[layer])

        mn = rmsnorm(memt, norm_mem_kv[layer])
        kv = matmul(mn, w_mem_kv[layer], BF16, b * m_len, 512, name="mem_kv")
        q = matmul(h, w_mem_q[layer], BF16, 1024, MEM_WIDTH, epilogue="q_prescale",
                   name="mem_q")
        o = memory_attention(q.reshape(b, s, -1), kv.reshape(b, m_len, -1)).reshape(t, -1)
        y = matmul(o, w_mem_o[layer], BF16, 1024, 1024, name="mem_out")
        xt, h = residual_norm(y, xt, norm_mem_post[layer], norm_mlp_pre[layer])

        u = matmul(h, w_mlp_up[layer], BF16, 1024, 512, epilogue="relu2", name="mlp_up")
        y = matmul_kgrid(u, w_mlp_down[layer], BF16, 2048, 1024, 1024, name="mlp_down")
        if layer + 1 < depth:
            xt, h = residual_norm(y, xt, norm_mlp_post[layer], norm_mix_pre[layer + 1])
        else:
            xt = residual_norm(y, xt, norm_mlp_post[layer])
    return xt.reshape(b, s, d)
```

```python
import functools

import numpy as np
import jax
import jax.numpy as jnp
from jax import lax
from jax.experimental import pallas as pl
from jax.experimental.pallas import tpu as pltpu

F32 = jnp.float32
BF16 = jnp.bfloat16

HEAD_DIM = 128
BF16_ROWS = 16
SOFTMAX_CHUNK = 1024
DIL_CONFIGS = ((128, 1), (512, 4), (2048, 16))
N_DIL_GROUPS = 3
DIL_HEADS = 8
DIL_GROUP_WIDTH = DIL_HEADS * HEAD_DIM
DIL_WIDTH = N_DIL_GROUPS * DIL_GROUP_WIDTH
DIFF_HEADS = 8
DIFF_QK_WIDTH = DIFF_HEADS * 2 * HEAD_DIM
DIFF_V_DIM = 2 * HEAD_DIM
DIFF_V_WIDTH = DIFF_HEADS * DIFF_V_DIM
MEM_HEADS = 4
MEM_WIDTH = MEM_HEADS * HEAD_DIM
ROPE_THETA = 10000.0
Q_BLOCK = 128
NORM_EPS = 1e-6
MASK_VALUE = -1e30
LOG2E = 1.4426950408889634
Q_PRESCALE = HEAD_DIM ** -0.5 * LOG2E

W_QA = 0
W_KA = W_QA + DIL_WIDTH
W_VA = W_KA + DIL_WIDTH
W_QB = W_VA + DIL_WIDTH

PROJ_TN = 512
P_QA = 0
P_KA = P_QA + DIL_GROUP_WIDTH
P_VA = P_KA + DIL_GROUP_WIDTH
P_QB = P_VA + DIL_GROUP_WIDTH
P_KB = P_QB + DIFF_QK_WIDTH
P_VB = P_KB + DIFF_QK_WIDTH
P_GA = P_VB + DIFF_V_WIDTH

V7X_VMEM_BYTES = 64 * 1024 * 1024
VMEM_LIMIT = V7X_VMEM_BYTES - 6 * 1024 * 1024


def _params(*sem):
    return pltpu.CompilerParams(dimension_semantics=sem, vmem_limit_bytes=VMEM_LIMIT)


def _rms(x, g):
    return x * lax.rsqrt(jnp.mean(x * x, axis=-1, keepdims=True) + NORM_EPS) * g


def _heads(width):
    return [slice(hh * HEAD_DIM, (hh + 1) * HEAD_DIM) for hh in range(width // HEAD_DIM)]


def _qkt(q, k):
    return lax.dot_general(q, k, (((1,), (1,)), ((), ())), preferred_element_type=F32)


def _rmsnorm_kernel(x_ref, g_ref, o_ref):
    o_ref[...] = _rms(x_ref[...], g_ref[...]).astype(o_ref.dtype)


def rmsnorm(x, g, tr=256):
    t, d = x.shape
    return pl.pallas_call(
        _rmsnorm_kernel,
        grid=(t // tr,),
        in_specs=[pl.BlockSpec((tr, d), lambda i: (i, 0)),
                  pl.BlockSpec((1, d), lambda i: (0, 0))],
        out_specs=pl.BlockSpec((tr, d), lambda i: (i, 0)),
        out_shape=jax.ShapeDtypeStruct((t, d), BF16),
        compiler_params=_params("parallel"),
        name="rmsnorm",
    )(x, g.reshape(1, d))


def _rope_table_kernel(pos_ref, inv_ref, sign_ref, cos_ref, sin_ref):
    ang = pos_ref[...] * inv_ref[...]
    cos_ref[...] = jnp.cos(ang)
    sin_ref[...] = jnp.sin(ang) * sign_ref[...]


def rope_tables(positions, tr=1024):
    t = positions.size
    half = HEAD_DIM // 2
    inv = ROPE_THETA ** (-jnp.arange(0, HEAD_DIM, 2, dtype=F32) / HEAD_DIM)
    inv2 = jnp.concatenate([inv, inv]).reshape(1, HEAD_DIM)
    sign = jnp.concatenate([-jnp.ones((half,), F32), jnp.ones((half,), F32)]).reshape(1, HEAD_DIM)
    pos = positions.astype(F32).reshape(t, 1)
    return pl.pallas_call(
        _rope_table_kernel,
        grid=(t // tr,),
        in_specs=[pl.BlockSpec((tr, 1), lambda i: (i, 0)),
                  pl.BlockSpec((1, HEAD_DIM), lambda i: (0, 0)),
                  pl.BlockSpec((1, HEAD_DIM), lambda i: (0, 0))],
        out_specs=[pl.BlockSpec((tr, HEAD_DIM), lambda i: (i, 0))] * 2,
        out_shape=[jax.ShapeDtypeStruct((t, HEAD_DIM), F32)] * 2,
        compiler_params=_params("parallel"),
        name="rope_tables",
    )(pos, inv2, sign)


def _rope(xh, c, s):
    return xh * c + pltpu.roll(xh, HEAD_DIM // 2, 1) * s


def _inproj_kernel(h_ref, w_ref, cos_ref, sin_ref, o_ref):
    j = pl.program_id(1)
    acc = jnp.dot(h_ref[...], w_ref[...].astype(BF16), preferred_element_type=F32)
    is_q = (j < P_KA // PROJ_TN) | ((j >= P_QB // PROJ_TN) & (j < P_KB // PROJ_TN))
    is_k = (((j >= P_KA // PROJ_TN) & (j < P_VA // PROJ_TN))
            | ((j >= P_KB // PROJ_TN) & (j < P_VB // PROJ_TN)))

    @pl.when(is_q | is_k)
    def _():
        f = jnp.where(is_q, Q_PRESCALE, 1.0).astype(F32)
        c = cos_ref[...] * f
        s = sin_ref[...] * f
        for sl in _heads(PROJ_TN):
            o_ref[:, sl] = _rope(acc[:, sl], c, s).astype(o_ref.dtype)

    @pl.when(jnp.logical_not(is_q | is_k))
    def _():
        o_ref[...] = acc.astype(o_ref.dtype)


def in_projection(h, w, cos, sin, tm=1024):
    t, k = h.shape
    tn = PROJ_TN
    n_out = w.shape[1] - 3 * (DIL_WIDTH - DIL_GROUP_WIDTH)
    per = DIL_GROUP_WIDTH // tn
    n_group0 = 3 * per
    stride_a = DIL_WIDTH // tn
    skip = W_QB // tn - n_group0

    def w_map(i, j):
        return (0, jnp.where(j < n_group0, (j // per) * stride_a + j % per, j + skip))

    return pl.pallas_call(
        _inproj_kernel,
        grid=(t // tm, n_out // tn),
        in_specs=[pl.BlockSpec((tm, k), lambda i, j: (i, 0)),
                  pl.BlockSpec((k, tn), w_map),
                  pl.BlockSpec((tm, HEAD_DIM), lambda i, j: (i, 0)),
                  pl.BlockSpec((tm, HEAD_DIM), lambda i, j: (i, 0))],
        out_specs=pl.BlockSpec((tm, tn), lambda i, j: (i, j)),
        out_shape=jax.ShapeDtypeStruct((t, n_out), BF16),
        compiler_params=_params("parallel", "arbitrary"),
        name="in_projection",
    )(h, w, cos, sin)


def _inproj_dilated_kernel(h_ref, w_ref, cos_ref, sin_ref, o_ref, slab_ref, *, dil):
    seg = pl.program_id(1) // (DIL_GROUP_WIDTH // PROJ_TN)
    tm = h_ref.shape[0]
    acc = jnp.dot(h_ref[...], w_ref[...].astype(BF16), preferred_element_type=F32)
    f = jnp.where(seg == 0, Q_PRESCALE, 1.0).astype(F32)
    c = jnp.where(seg < 2, cos_ref[...] * f, 1.0)
    s = jnp.where(seg < 2, sin_ref[...] * f, 0.0)
    for hh, sl in enumerate(_heads(PROJ_TN)):
        slab_ref[hh] = _rope(acc[:, sl], c, s)
        for r in range(dil):
            o_ref[0, r, :, sl] = slab_ref[hh, pl.ds(r, tm // dil, stride=dil), :].astype(o_ref.dtype)


def in_projection_dilated(h, w, cos, sin, g, dil, batch, tm=1024):
    t, k = h.shape
    tn = PROJ_TN
    s_len = t // batch
    tiles_per_seq = s_len // tm
    per = DIL_GROUP_WIDTH // tn
    stride_a = DIL_WIDTH // tn
    return pl.pallas_call(
        functools.partial(_inproj_dilated_kernel, dil=dil),
        grid=(t // tm, 3 * per),
        in_specs=[pl.BlockSpec((tm, k), lambda i, j: (i, 0)),
                  pl.BlockSpec((k, tn), lambda i, j: (0, (j // per) * stride_a + g * per + j % per)),
                  pl.BlockSpec((tm, HEAD_DIM), lambda i, j: (i, 0)),
                  pl.BlockSpec((tm, HEAD_DIM), lambda i, j: (i, 0))],
        out_specs=pl.BlockSpec((1, dil, tm // dil, tn),
                               lambda i, j: (i // tiles_per_seq, 0, i % tiles_per_seq, j)),
        out_shape=jax.ShapeDtypeStruct((batch, dil, s_len // dil, 3 * DIL_GROUP_WIDTH), BF16),
        scratch_shapes=[pltpu.VMEM((tn // HEAD_DIM, tm, HEAD_DIM), F32)],
        compiler_params=_params("parallel", "arbitrary"),
        name=f"in_projection_d{dil}",
    )(h, w, cos, sin)


def _mm_kernel(a_ref, w_ref, o_ref, *, epilogue):
    acc = jnp.dot(a_ref[...], w_ref[...].astype(BF16), preferred_element_type=F32)
    if epilogue == "relu2":
        acc = jnp.square(jnp.maximum(acc, 0.0))
    elif epilogue == "q_prescale":
        acc = acc * Q_PRESCALE
    o_ref[...] = acc.astype(o_ref.dtype)


def matmul(a, w, out_dtype, tm, tn, epilogue=None, name="matmul"):
    m, k = a.shape
    n = w.shape[1]
    return pl.pallas_call(
        functools.partial(_mm_kernel, epilogue=epilogue),
        grid=(m // tm, n // tn),
        in_specs=[pl.BlockSpec((tm, k), lambda i, j: (i, 0)),
                  pl.BlockSpec((k, tn), lambda i, j: (0, j))],
        out_specs=pl.BlockSpec((tm, tn), lambda i, j: (i, j)),
        out_shape=jax.ShapeDtypeStruct((m, n), out_dtype),
        compiler_params=_params("parallel", "arbitrary"),
        name=name,
    )(a, w)


def _mm_kgrid_kernel(a_ref, w_ref, o_ref, acc_ref):
    kk = pl.program_id(2)

    @pl.when(kk == 0)
    def _():
        acc_ref[...] = jnp.zeros_like(acc_ref)

    acc_ref[...] += jnp.dot(a_ref[...], w_ref[...].astype(BF16), preferred_element_type=F32)

    @pl.when(kk == pl.num_programs(2) - 1)
    def _():
        o_ref[...] = acc_ref[...].astype(o_ref.dtype)


def matmul_kgrid(a, w, out_dtype, tm, tn, tk, name="matmul_kgrid"):
    m, k = a.shape
    n = w.shape[1]
    return pl.pallas_call(
        _mm_kgrid_kernel,
        grid=(m // tm, n // tn, k // tk),
        in_specs=[pl.BlockSpec((tm, tk), lambda i, j, kk: (i, kk)),
                  pl.BlockSpec((tk, tn), lambda i, j, kk: (kk, j))],
        out_specs=pl.BlockSpec((tm, tn), lambda i, j, kk: (i, j)),
        out_shape=jax.ShapeDtypeStruct((m, n), out_dtype),
        scratch_shapes=[pltpu.VMEM((tm, tn), F32)],
        compiler_params=_params("parallel", "parallel", "arbitrary"),
        name=name,
    )(a, w)


DIL_STEP = 2 * Q_BLOCK


def _dilated_kernel(q_ref, kp_ref, kc_ref, kn_ref, vp_ref, vc_ref, vn_ref,
                    o_ref, lse_ref, *, class_len):
    i = pl.program_id(2)
    half = Q_BLOCK // 2
    nk = 2 * Q_BLOCK
    row = lax.broadcasted_iota(jnp.int32, (Q_BLOCK, nk), 0)
    col = lax.broadcasted_iota(jnp.int32, (Q_BLOCK, nk), 1)
    lane = lax.broadcasted_iota(jnp.int32, (Q_BLOCK, HEAD_DIM), 1)
    rel = col - half - row
    in_band = (rel >= -half) & (rel <= half)
    for sb in range(DIL_STEP // Q_BLOCK):
        q0 = sb * Q_BLOCK
        kpos = i * DIL_STEP + q0 - half + col
        valid = in_band & (kpos >= 0) & (kpos < class_len)
        lse_tile = jnp.zeros((Q_BLOCK, HEAD_DIM), F32)
        for hh, sl in enumerate(_heads(DIL_GROUP_WIDTH)):
            q = q_ref[0, 0, q0:q0 + Q_BLOCK, sl]
            if sb == 0:
                k = jnp.concatenate([kp_ref[0, 0, :, sl], kc_ref[0, 0, :nk - half, sl]], axis=0)
                v = jnp.concatenate([vp_ref[0, 0, :, sl], vc_ref[0, 0, :nk - half, sl]], axis=0)
            else:
                k = jnp.concatenate([kc_ref[0, 0, half:, sl], kn_ref[0, 0, :, sl]], axis=0)
                v = jnp.concatenate([vc_ref[0, 0, half:, sl], vn_ref[0, 0, :, sl]], axis=0)
            s = jnp.where(valid, _qkt(q, k), MASK_VALUE)
            m = jnp.max(s, axis=-1, keepdims=True)
            p = jnp.exp2(s - m)
            l = jnp.sum(p, axis=-1, keepdims=True)
            o = jnp.dot(p.astype(BF16), v, preferred_element_type=F32) / l
            o_ref[0, 0, q0:q0 + Q_BLOCK, sl] = o.astype(o_ref.dtype)
            lse_tile = jnp.where(lane == hh, m + jnp.log2(l), lse_tile)
        lse_ref[0, 0, q0:q0 + Q_BLOCK, :] = lse_tile


def dilated_group(qkv, cq, ck, cv):
    b, dil, class_len, _ = qkv.shape
    half = Q_BLOCK // 2
    per_step = DIL_STEP // half
    n_half_blocks = class_len // half
    w = DIL_GROUP_WIDTH

    def cur(c):
        return pl.BlockSpec((1, 1, DIL_STEP, w), lambda bb, r, i: (bb, r, i, c))

    def prev(c):
        return pl.BlockSpec((1, 1, half, w),
                            lambda bb, r, i: (bb, r, jnp.maximum(per_step * i - 1, 0), c))

    def nxt(c):
        return pl.BlockSpec((1, 1, half, w),
                            lambda bb, r, i: (bb, r, jnp.minimum(per_step * (i + 1), n_half_blocks - 1), c))

    return pl.pallas_call(
        functools.partial(_dilated_kernel, class_len=class_len),
        grid=(b, dil, class_len // DIL_STEP),
        in_specs=[cur(cq), prev(ck), cur(ck), nxt(ck), prev(cv), cur(cv), nxt(cv)],
        out_specs=[pl.BlockSpec((1, 1, DIL_STEP, w), lambda bb, r, i: (bb, r, i, 0)),
                   pl.BlockSpec((1, 1, DIL_STEP, HEAD_DIM), lambda bb, r, i: (bb, r, i, 0))],
        out_shape=[jax.ShapeDtypeStruct((b, dil, class_len, w), BF16),
                   jax.ShapeDtypeStruct((b, dil, class_len, HEAD_DIM), F32)],
        compiler_params=_params("parallel", "parallel", "arbitrary"),
        name=f"dilated_attention_d{dil}",
    )(qkv, qkv, qkv, qkv, qkv, qkv, qkv)


def _dil_merge_kernel(o0_ref, l0_ref, o1_ref, l1_ref, o2_ref, l2_ref, out_ref,
                      so1, sl1, so2, sl2):
    for src_o, src_l, dst_o, dst_l in ((o1_ref, l1_ref, so1, sl1), (o2_ref, l2_ref, so2, sl2)):
        dil, n = src_o.shape[1], src_o.shape[2]
        for r in range(dil):
            dst_l[pl.ds(r, n, stride=dil), :] = src_l[0, r]
            for hh, sl in enumerate(_heads(DIL_GROUP_WIDTH)):
                dst_o[hh, pl.ds(r, n, stride=dil), :] = src_o[0, r, :, sl].astype(F32)
    a0, a1, a2 = l0_ref[0, 0], sl1[...], sl2[...]
    m = jnp.maximum(jnp.maximum(a0, a1), a2)
    e0, e1, e2 = jnp.exp2(a0 - m), jnp.exp2(a1 - m), jnp.exp2(a2 - m)
    inv = 1.0 / (e0 + e1 + e2)
    w0, w1, w2 = e0 * inv, e1 * inv, e2 * inv
    for hh, sl in enumerate(_heads(DIL_GROUP_WIDTH)):
        h1 = slice(hh, hh + 1)
        out = (w0[:, h1] * o0_ref[0, 0, :, sl].astype(F32) + w1[:, h1] * so1[hh]
               + w2[:, h1] * so2[hh])
        out_ref[0, :, sl] = out.astype(out_ref.dtype)


def dilated_merge(outs, tr=512):
    (o0, l0), (o1, l1), (o2, l2) = outs
    b, _, s_len, w = o0.shape

    def spec(a):
        dil = a.shape[1]
        return pl.BlockSpec((1, dil, tr // dil, a.shape[3]), lambda bb, i: (bb, 0, i, 0))

    slab_o = pltpu.VMEM((w // HEAD_DIM, tr, HEAD_DIM), F32)
    slab_l = pltpu.VMEM((tr, HEAD_DIM), F32)
    return pl.pallas_call(
        _dil_merge_kernel,
        grid=(b, s_len // tr),
        in_specs=[spec(o0), spec(l0), spec(o1), spec(l1), spec(o2), spec(l2)],
        out_specs=pl.BlockSpec((1, tr, w), lambda bb, i: (bb, i, 0)),
        out_shape=jax.ShapeDtypeStruct((b, s_len, w), BF16),
        scratch_shapes=[slab_o, slab_l, slab_o, slab_l],
        compiler_params=_params("parallel", "arbitrary"),
        name="dilated_merge",
    )(o0, l0, o1, l1, o2, l2)


def _diff_kernel(q_ref, k_ref, v_ref, lq1, lk1, lq2, lk2, g_ref, o_ref,
                 s_scr, p_scr, l_scr, *, lam_init):
    lam = (jnp.exp(jnp.sum(lq1[...] * lk1[...], axis=-1, keepdims=True))
           - jnp.exp(jnp.sum(lq2[...] * lk2[...], axis=-1, keepdims=True)) + lam_init)
    th = q_ref.shape[1] // 2
    n_keys = k_ref.shape[1]

    def scores(half, c, buf):
        sl = slice(c * HEAD_DIM, (c + 1) * HEAD_DIM)
        s_scr[buf] = _qkt(q_ref[0, half * th:(half + 1) * th, sl], k_ref[0, :, sl])

    def softmax(buf):
        for r in range(th // BF16_ROWS):
            rows = slice(r * BF16_ROWS, (r + 1) * BF16_ROWS)
            m = jnp.max(s_scr[buf, rows, :], axis=-1, keepdims=True)
            l = jnp.zeros((BF16_ROWS, 1), F32)
            for c0 in range(0, n_keys, SOFTMAX_CHUNK):
                cols = slice(c0, c0 + SOFTMAX_CHUNK)
                p = jnp.exp2(s_scr[buf, rows, cols] - m)
                l = l + jnp.sum(p, axis=-1, keepdims=True)
                p_scr[buf, rows, cols] = p.astype(BF16)
            l_scr[buf, rows, :] = jnp.broadcast_to(l, (BF16_ROWS, HEAD_DIM))

    def values(buf):
        return jnp.dot(p_scr[buf], v_ref[0], preferred_element_type=F32) / l_scr[buf, :, :1]

    def finish(half, o1, o2):
        o = o1 - lam * o2
        o_ref[0, half * th:(half + 1) * th, :] = (
            _rms(o, g_ref[...]) * (1.0 - lam_init)).astype(o_ref.dtype)

    scores(0, 0, 0)
    scores(0, 1, 1)
    softmax(0)
    scores(1, 0, 0)
    softmax(1)
    o_a1 = values(0)
    scores(1, 1, 1)
    softmax(0)
    o_a2 = values(1)
    finish(0, o_a1, o_a2)
    softmax(1)
    o_b1 = values(0)
    o_b2 = values(1)
    finish(1, o_b1, o_b2)


def differential_attention(proj, lq1, lk1, lq2, lk2, subln, lam_init, tq=1024):
    b, s, _ = proj.shape
    vec = pl.BlockSpec((1, HEAD_DIM), lambda bb, h, i: (0, 0))
    cq, ck, cv = P_QB // DIFF_V_DIM, P_KB // DIFF_V_DIM, P_VB // DIFF_V_DIM
    return pl.pallas_call(
        functools.partial(_diff_kernel, lam_init=lam_init),
        grid=(b, DIFF_HEADS, s // tq),
        in_specs=[pl.BlockSpec((1, tq, DIFF_V_DIM), lambda bb, h, i: (bb, i, cq + h)),
                  pl.BlockSpec((1, s, DIFF_V_DIM), lambda bb, h, i: (bb, 0, ck + h)),
                  pl.BlockSpec((1, s, DIFF_V_DIM), lambda bb, h, i: (bb, 0, cv + h)),
                  vec, vec, vec, vec,
                  pl.BlockSpec((1, DIFF_V_DIM), lambda bb, h, i: (0, 0))],
        out_specs=pl.BlockSpec((1, tq, DIFF_V_DIM), lambda bb, h, i: (bb, i, h)),
        out_shape=jax.ShapeDtypeStruct((b, s, DIFF_V_WIDTH), BF16),
        scratch_shapes=[pltpu.VMEM((2, tq // 2, s), F32), pltpu.VMEM((2, tq // 2, s), BF16),
                        pltpu.VMEM((2, tq // 2, HEAD_DIM), F32)],
        compiler_params=_params("parallel", "parallel", "arbitrary"),
        name="differential_attention",
    )(proj, proj, proj, lq1.reshape(1, -1), lk1.reshape(1, -1), lq2.reshape(1, -1),
      lk2.reshape(1, -1), subln.reshape(1, -1))


def _gate_kernel(oa_ref, wa_ref, ob_ref, wb_ref, ga_ref, gb_ref, o_ref):
    ya = jnp.dot(oa_ref[...], wa_ref[...].astype(BF16), preferred_element_type=F32)
    yb = jnp.dot(ob_ref[...], wb_ref[...].astype(BF16), preferred_element_type=F32)
    o = (jax.nn.sigmoid(ga_ref[...].astype(F32)) * ya
         + jax.nn.sigmoid(gb_ref[...].astype(F32)) * yb)
    o_ref[...] = o.astype(o_ref.dtype)


def gated_merge(out_a, w_a, out_b, w_b, proj2d, tm=1024, tn=512):
    t = out_a.shape[0]
    n = w_a.shape[1]
    ca = P_GA // tn
    cb = (P_GA + n) // tn
    return pl.pallas_call(
        _gate_kernel,
        grid=(t // tm, n // tn),
        in_specs=[pl.BlockSpec((tm, out_a.shape[1]), lambda i, j: (i, 0)),
                  pl.BlockSpec((w_a.shape[0], tn), lambda i, j: (0, j)),
                  pl.BlockSpec((tm, out_b.shape[1]), lambda i, j: (i, 0)),
                  pl.BlockSpec((w_b.shape[0], tn), lambda i, j: (0, j)),
                  pl.BlockSpec((tm, tn), lambda i, j: (i, ca + j)),
                  pl.BlockSpec((tm, tn), lambda i, j: (i, cb + j))],
        out_specs=pl.BlockSpec((tm, tn), lambda i, j: (i, j)),
        out_shape=jax.ShapeDtypeStruct((t, n), BF16),
        compiler_params=_params("parallel", "arbitrary"),
        name="gated_merge",
    )(out_a, w_a, out_b, w_b, proj2d, proj2d)


def _resnorm_kernel(y_ref, x_ref, gp_ref, gn_ref, xo_ref, ho_ref):
    xn = x_ref[...] + _rms(y_ref[...].astype(F32), gp_ref[...])
    xo_ref[...] = xn
    ho_ref[...] = _rms(xn, gn_ref[...]).astype(ho_ref.dtype)


def _resnorm_last_kernel(y_ref, x_ref, gp_ref, xo_ref):
    xo_ref[...] = x_ref[...] + _rms(y_ref[...].astype(F32), gp_ref[...])


def residual_norm(y, x, g_post, g_next=None, tr=256):
    t, d = x.shape
    row = pl.BlockSpec((tr, d), lambda i: (i, 0))
    vec = pl.BlockSpec((1, d), lambda i: (0, 0))
    if g_next is None:
        return pl.pallas_call(
            _resnorm_last_kernel,
            grid=(t // tr,),
            in_specs=[row, row, vec],
            out_specs=row,
            out_shape=jax.ShapeDtypeStruct((t, d), F32),
            compiler_params=_params("parallel"),
            name="residual_norm_last",
        )(y, x, g_post.reshape(1, d))
    return pl.pallas_call(
        _resnorm_kernel,
        grid=(t // tr,),
        in_specs=[row, row, vec, vec],
        out_specs=[row, row],
        out_shape=[jax.ShapeDtypeStruct((t, d), F32), jax.ShapeDtypeStruct((t, d), BF16)],
        compiler_params=_params("parallel"),
        name="residual_norm",
    )(y, x, g_post.reshape(1, d), g_next.reshape(1, d))


def _mem_attn_kernel(q_ref, kv_ref, o_ref):
    for hh, sl in enumerate(_heads(MEM_WIDTH)):
        slv = slice(MEM_WIDTH + hh * HEAD_DIM, MEM_WIDTH + (hh + 1) * HEAD_DIM)
        s = _qkt(q_ref[0, :, sl], kv_ref[0, :, sl])
        p = jnp.exp2(s - jnp.max(s, axis=-1, keepdims=True))
        l = jnp.sum(p, axis=-1, keepdims=True)
        o = jnp.dot(p.astype(BF16), kv_ref[0, :, slv], preferred_element_type=F32) / l
        o_ref[0, :, sl] = o.astype(o_ref.dtype)


def memory_attention(q, kv, tq=512):
    b, s, w = q.shape
    m = kv.shape[1]
    return pl.pallas_call(
        _mem_attn_kernel,
        grid=(b, s // tq),
        in_specs=[pl.BlockSpec((1, tq, w), lambda bb, i: (bb, i, 0)),
                  pl.BlockSpec((1, m, 2 * w), lambda bb, i: (bb, 0, 0))],
        out_specs=pl.BlockSpec((1, tq, w), lambda bb, i: (bb, i, 0)),
        out_shape=jax.ShapeDtypeStruct((b, s, w), BF16),
        compiler_params=_params("parallel", "arbitrary"),
        name="memory_attention",
    )(q, kv)


def kernel(x, mem, positions, norm_mix_pre, w_in, w_a, w_b, w_mix_out, norm_mix_post,
           lambda_q1, lambda_k1, lambda_q2, lambda_k2, diff_subln,
           norm_mem_pre, norm_mem_kv, w_mem_q, w_mem_kv, w_mem_o, norm_mem_post,
           norm_mlp_pre, w_mlp_up, w_mlp_down, norm_mlp_post):
    b, s, d = x.shape
    t = b * s
    depth = w_in.shape[0]
    m_len = mem.shape[1]
    xt = x.reshape(t, d)
    memt = mem.reshape(b * m_len, d)
    cos, sin = rope_tables(positions)

    h = rmsnorm(xt, norm_mix_pre[0])
    for layer in range(depth):
        lam_init = 0.8 - 0.6 * float(np.exp(-0.3 * layer))

        w_in_b = w_in[layer]
        proj = in_projection(h, w_in_b, cos, sin)
        proj3 = proj.reshape(b, s, -1)
        tile = DIL_GROUP_WIDTH
        outs = [dilated_group(proj3.reshape(b, 1, s, -1), P_QA // tile, P_KA // tile, P_VA // tile)]
        for g, (_, dil) in enumerate(DIL_CONFIGS):
            if dil > 1:
                qkv = in_projection_dilated(h, w_in_b, cos, sin, g, dil, b)
                outs.append(dilated_group(qkv, 0, 1, 2))
        out_a = dilated_merge(outs).reshape(t, -1)
        out_b = differential_attention(proj3, lambda_q1[layer], lambda_k1[layer],
                                       lambda_q2[layer], lambda_k2[layer],
                                       diff_subln[layer], lam_init).reshape(t, -1)
        merged = gated_merge(out_a, w_a[layer], out_b, w_b[layer], proj)
        y = matmul(merged, w_mix_out[layer], BF16, 1024, 512, name="mix_out")
        xt, h = residual_norm(y, xt, norm_mix_post[layer], norm_mem_pre[layer])

        mn = rmsnorm(memt, norm_mem_kv[layer])
        kv = matmul(mn, w_mem_kv[layer], BF16, b * m_len, 512, name="mem_kv")
        q = matmul(h, w_mem_q[layer], BF16, 1024, MEM_WIDTH, epilogue="q_prescale",
                   name="mem_q")
        o = memory_attention(q.reshape(b, s, -1), kv.reshape(b, m_len, -1)).reshape(t, -1)
        y = matmul(o, w_mem_o[layer], BF16, 1024, 1024, name="mem_out")
        xt, h = residual_norm(y, xt, norm_mem_post[layer], norm_mlp_pre[layer])

        u = matmul(h, w_mlp_up[layer], BF16, 1024, 512, epilogue="relu2", name="mlp_up")
        y = matmul_kgrid(u, w_mlp_down[layer], BF16, 2048, 1024, 1024, name="mlp_down")
        if layer + 1 < depth:
            xt, h = residual_norm(y, xt, norm_mlp_post[layer], norm_mix_pre[layer + 1])
        else:
            xt = residual_norm(y, xt, norm_mlp_post[layer])
    return xt.reshape(b, s, d)
```

```python
import functools

import numpy as np
import jax
import jax.numpy as jnp
from jax import lax
from jax.experimental import pallas as pl
from jax.experimental.pallas import tpu as pltpu

F32 = jnp.float32
BF16 = jnp.bfloat16

HEAD_DIM = 128
BF16_ROWS = 16
SOFTMAX_CHUNK = 1024
DIL_CONFIGS = ((128, 1), (512, 4), (2048, 16))
N_DIL_GROUPS = 3
DIL_HEADS = 8
DIL_GROUP_WIDTH = DIL_HEADS * HEAD_DIM
DIL_WIDTH = N_DIL_GROUPS * DIL_GROUP_WIDTH
DIFF_HEADS = 8
DIFF_QK_WIDTH = DIFF_HEADS * 2 * HEAD_DIM
DIFF_V_DIM = 2 * HEAD_DIM
DIFF_V_WIDTH = DIFF_HEADS * DIFF_V_DIM
MEM_HEADS = 4
MEM_WIDTH = MEM_HEADS * HEAD_DIM
ROPE_THETA = 10000.0
Q_BLOCK = 128
NORM_EPS = 1e-6
MASK_VALUE = -1e30
LOG2E = 1.4426950408889634
Q_PRESCALE = HEAD_DIM ** -0.5 * LOG2E

W_QA = 0
W_KA = W_QA + DIL_WIDTH
W_VA = W_KA + DIL_WIDTH
W_QB = W_VA + DIL_WIDTH

PROJ_TN = 512
P_QA = 0
P_KA = P_QA + DIL_GROUP_WIDTH
P_VA = P_KA + DIL_GROUP_WIDTH
P_QB = P_VA + DIL_GROUP_WIDTH
P_KB = P_QB + DIFF_QK_WIDTH
P_VB = P_KB + DIFF_QK_WIDTH
P_GA = P_VB + DIFF_V_WIDTH

V7X_VMEM_BYTES = 64 * 1024 * 1024
VMEM_LIMIT = V7X_VMEM_BYTES - 6 * 1024 * 1024


def _params(*sem):
    return pltpu.CompilerParams(dimension_semantics=sem, vmem_limit_bytes=VMEM_LIMIT)


def _rms(x, g):
    return x * lax.rsqrt(jnp.mean(x * x, axis=-1, keepdims=True) + NORM_EPS) * g


def _heads(width):
    return [slice(hh * HEAD_DIM, (hh + 1) * HEAD_DIM) for hh in range(width // HEAD_DIM)]


def _qkt(q, k):
    return lax.dot_general(q, k, (((1,), (1,)), ((), ())), preferred_element_type=F32)


def _rmsnorm_kernel(x_ref, g_ref, o_ref):
    o_ref[...] = _rms(x_ref[...], g_ref[...]).astype(o_ref.dtype)


def rmsnorm(x, g, tr=256):
    t, d = x.shape
    return pl.pallas_call(
        _rmsnorm_kernel,
        grid=(t // tr,),
        in_specs=[pl.BlockSpec((tr, d), lambda i: (i, 0)),
                  pl.BlockSpec((1, d), lambda i: (0, 0))],
        out_specs=pl.BlockSpec((tr, d), lambda i: (i, 0)),
        out_shape=jax.ShapeDtypeStruct((t, d), BF16),
        compiler_params=_params("parallel"),
        name="rmsnorm",
    )(x, g.reshape(1, d))


def _rope_table_kernel(pos_ref, inv_ref, sign_ref, cos_ref, sin_ref):
    ang = pos_ref[...] * inv_ref[...]
    cos_ref[...] = jnp.cos(ang)
    sin_ref[...] = jnp.sin(ang) * sign_ref[...]


def rope_tables(positions, tr=1024):
    t = positions.size
    half = HEAD_DIM // 2
    inv = ROPE_THETA ** (-jnp.arange(0, HEAD_DIM, 2, dtype=F32) / HEAD_DIM)
    inv2 = jnp.concatenate([inv, inv]).reshape(1, HEAD_DIM)
    sign = jnp.concatenate([-jnp.ones((half,), F32), jnp.ones((half,), F32)]).reshape(1, HEAD_DIM)
    pos = positions.astype(F32).reshape(t, 1)
    return pl.pallas_call(
        _rope_table_kernel,
        grid=(t // tr,),
        in_specs=[pl.BlockSpec((tr, 1), lambda i: (i, 0)),
                  pl.BlockSpec((1, HEAD_DIM), lambda i: (0, 0)),
                  pl.BlockSpec((1, HEAD_DIM), lambda i: (0, 0))],
        out_specs=[pl.BlockSpec((tr, HEAD_DIM), lambda i: (i, 0))] * 2,
        out_shape=[jax.ShapeDtypeStruct((t, HEAD_DIM), F32)] * 2,
        compiler_params=_params("parallel"),
        name="rope_tables",
    )(pos, inv2, sign)


def _rope(xh, c, s):
    return xh * c + pltpu.roll(xh, HEAD_DIM // 2, 1) * s


def _inproj_kernel(h_ref, w_ref, cos_ref, sin_ref, o_ref):
    j = pl.program_id(1)
    acc = jnp.dot(h_ref[...], w_ref[...].astype(BF16), preferred_element_type=F32)
    is_q = (j < P_KA // PROJ_TN) | ((j >= P_QB // PROJ_TN) & (j < P_KB // PROJ_TN))
    is_k = (((j >= P_KA // PROJ_TN) & (j < P_VA // PROJ_TN))
            | ((j >= P_KB // PROJ_TN) & (j < P_VB // PROJ_TN)))

    @pl.when(is_q | is_k)
    def _():
        f = jnp.where(is_q, Q_PRESCALE, 1.0).astype(F32)
        c = cos_ref[...] * f
        s = sin_ref[...] * f
        for sl in _heads(PROJ_TN):
            o_ref[:, sl] = _rope(acc[:, sl], c, s).astype(o_ref.dtype)

    @pl.when(jnp.logical_not(is_q | is_k))
    def _():
        o_ref[...] = acc.astype(o_ref.dtype)


def in_projection(h, w, cos, sin, tm=1024):
    t, k = h.shape
    tn = PROJ_TN
    n_out = w.shape[1] - 3 * (DIL_WIDTH - DIL_GROUP_WIDTH)
    per = DIL_GROUP_WIDTH // tn
    n_group0 = 3 * per
    stride_a = DIL_WIDTH // tn
    skip = W_QB // tn - n_group0

    def w_map(i, j):
        return (0, jnp.where(j < n_group0, (j // per) * stride_a + j % per, j + skip))

    return pl.pallas_call(
        _inproj_kernel,
        grid=(t // tm, n_out // tn),
        in_specs=[pl.BlockSpec((tm, k), lambda i, j: (i, 0)),
                  pl.BlockSpec((k, tn), w_map),
                  pl.BlockSpec((tm, HEAD_DIM), lambda i, j: (i, 0)),
                  pl.BlockSpec((tm, HEAD_DIM), lambda i, j: (i, 0))],
        out_specs=pl.BlockSpec((tm, tn), lambda i, j: (i, j)),
        out_shape=jax.ShapeDtypeStruct((t, n_out), BF16),
        compiler_params=_params("parallel", "arbitrary"),
        name="in_projection",
    )(h, w, cos, sin)


def _inproj_dilated_kernel(h_ref, w_ref, cos_ref, sin_ref, o_ref, slab_ref, *, dil):
    seg = pl.program_id(1) // (DIL_GROUP_WIDTH // PROJ_TN)
    tm = h_ref.shape[0]
    acc = jnp.dot(h_ref[...], w_ref[...].astype(BF16), preferred_element_type=F32)
    f = jnp.where(seg == 0, Q_PRESCALE, 1.0).astype(F32)
    c = jnp.where(seg < 2, cos_ref[...] * f, 1.0)
    s = jnp.where(seg < 2, sin_ref[...] * f, 0.0)
    for hh, sl in enumerate(_heads(PROJ_TN)):
        slab_ref[hh] = _rope(acc[:, sl], c, s)
        for r in range(dil):
            o_ref[0, r, :, sl] = slab_ref[hh, pl.ds(r, tm // dil, stride=dil), :].astype(o_ref.dtype)


def in_projection_dilated(h, w, cos, sin, g, dil, batch, tm=1024):
    t, k = h.shape
    tn = PROJ_TN
    s_len = t // batch
    tiles_per_seq = s_len // tm
    per = DIL_GROUP_WIDTH // tn
    stride_a = DIL_WIDTH // tn
    return pl.pallas_call(
        functools.partial(_inproj_dilated_kernel, dil=dil),
        grid=(t // tm, 3 * per),
        in_specs=[pl.BlockSpec((tm, k), lambda i, j: (i, 0)),
                  pl.BlockSpec((k, tn), lambda i, j: (0, (j // per) * stride_a + g * per + j % per)),
                  pl.BlockSpec((tm, HEAD_DIM), lambda i, j: (i, 0)),
                  pl.BlockSpec((tm, HEAD_DIM), lambda i, j: (i, 0))],
        out_specs=pl.BlockSpec((1, dil, tm // dil, tn),
                               lambda i, j: (i // tiles_per_seq, 0, i % tiles_per_seq, j)),
        out_shape=jax.ShapeDtypeStruct((batch, dil, s_len // dil, 3 * DIL_GROUP_WIDTH), BF16),
        scratch_shapes=[pltpu.VMEM((tn // HEAD_DIM, tm, HEAD_DIM), F32)],
        compiler_params=_params("parallel", "arbitrary"),
        name=f"in_projection_d{dil}",
    )(h, w, cos, sin)


def _mm_kernel(a_ref, w_ref, o_ref, *, epilogue):
    acc = jnp.dot(a_ref[...], w_ref[...].astype(BF16), preferred_element_type=F32)
    if epilogue == "relu2":
        acc = jnp.square(jnp.maximum(acc, 0.0))
    o_ref[...] = acc.astype(o_ref.dtype)


def matmul(a, w, out_dtype, tm, tn, epilogue=None, name="matmul"):
    m, k = a.shape
    n = w.shape[1]
    return pl.pallas_call(
        functools.partial(_mm_kernel, epilogue=epilogue),
        grid=(m // tm, n // tn),
        in_specs=[pl.BlockSpec((tm, k), lambda i, j: (i, 0)),
                  pl.BlockSpec((k, tn), lambda i, j: (0, j))],
        out_specs=pl.BlockSpec((tm, tn), lambda i, j: (i, j)),
        out_shape=jax.ShapeDtypeStruct((m, n), out_dtype),
        compiler_params=_params("parallel", "arbitrary"),
        name=name,
    )(a, w)


def _mm_kgrid_kernel(a_ref, w_ref, o_ref, acc_ref):
    kk = pl.program_id(2)

    @pl.when(kk == 0)
    def _():
        acc_ref[...] = jnp.zeros_like(acc_ref)

    acc_ref[...] += jnp.dot(a_ref[...], w_ref[...].astype(BF16), preferred_element_type=F32)

    @pl.when(kk == pl.num_programs(2) - 1)
    def _():
        o_ref[...] = acc_ref[...].astype(o_ref.dtype)


def matmul_kgrid(a, w, out_dtype, tm, tn, tk, name="matmul_kgrid"):
    m, k = a.shape
    n = w.shape[1]
    return pl.pallas_call(
        _mm_kgrid_kernel,
        grid=(m // tm, n // tn, k // tk),
        in_specs=[pl.BlockSpec((tm, tk), lambda i, j, kk: (i, kk)),
                  pl.BlockSpec((tk, tn), lambda i, j, kk: (kk, j))],
        out_specs=pl.BlockSpec((tm, tn), lambda i, j, kk: (i, j)),
        out_shape=jax.ShapeDtypeStruct((m, n), out_dtype),
        scratch_shapes=[pltpu.VMEM((tm, tn), F32)],
        compiler_params=_params("parallel", "parallel", "arbitrary"),
        name=name,
    )(a, w)


DIL_STEP = 2 * Q_BLOCK


def _dilated_kernel(q_ref, kp_ref, kc_ref, kn_ref, vp_ref, vc_ref, vn_ref,
                    o_ref, lse_ref, *, class_len):
    i = pl.program_id(2)
    half = Q_BLOCK // 2
    nk = 2 * Q_BLOCK
    row = lax.broadcasted_iota(jnp.int32, (Q_BLOCK, nk), 0)
    col = lax.broadcasted_iota(jnp.int32, (Q_BLOCK, nk), 1)
    lane = lax.broadcasted_iota(jnp.int32, (Q_BLOCK, HEAD_DIM), 1)
    rel = col - half - row
    in_band = (rel >= -half) & (rel <= half)
    for sb in range(DIL_STEP // Q_BLOCK):
        q0 = sb * Q_BLOCK
        kpos = i * DIL_STEP + q0 - half + col
        valid = in_band & (kpos >= 0) & (kpos < class_len)
        lse_tile = jnp.zeros((Q_BLOCK, HEAD_DIM), F32)
        for hh, sl in enumerate(_heads(DIL_GROUP_WIDTH)):
            q = q_ref[0, 0, q0:q0 + Q_BLOCK, sl]
            if sb == 0:
                k = jnp.concatenate([kp_ref[0, 0, :, sl], kc_ref[0, 0, :nk - half, sl]], axis=0)
                v = jnp.concatenate([vp_ref[0, 0, :, sl], vc_ref[0, 0, :nk - half, sl]], axis=0)
            else:
                k = jnp.concatenate([kc_ref[0, 0, half:, sl], kn_ref[0, 0, :, sl]], axis=0)
                v = jnp.concatenate([vc_ref[0, 0, half:, sl], vn_ref[0, 0, :, sl]], axis=0)
            s = jnp.where(valid, _qkt(q, k), MASK_VALUE)
            m = jnp.max(s, axis=-1, keepdims=True)
            p = jnp.exp2(s - m)
            l = jnp.sum(p, axis=-1, keepdims=True)
            o = jnp.dot(p.astype(BF16), v, preferred_element_type=F32) / l
            o_ref[0, 0, q0:q0 + Q_BLOCK, sl] = o.astype(o_ref.dtype)
            lse_tile = jnp.where(lane == hh, m + jnp.log2(l), lse_tile)
        lse_ref[0, 0, q0:q0 + Q_BLOCK, :] = lse_tile


def dilated_group(qkv, cq, ck, cv):
    b, dil, class_len, _ = qkv.shape
    half = Q_BLOCK // 2
    per_step = DIL_STEP // half
    n_half_blocks = class_len // half
    w = DIL_GROUP_WIDTH

    def cur(c):
        return pl.BlockSpec((1, 1, DIL_STEP, w), lambda bb, r, i: (bb, r, i, c))

    def prev(c):
        return pl.BlockSpec((1, 1, half, w),
                            lambda bb, r, i: (bb, r, jnp.maximum(per_step * i - 1, 0), c))

    def nxt(c):
        return pl.BlockSpec((1, 1, half, w),
                            lambda bb, r, i: (bb, r, jnp.minimum(per_step * (i + 1), n_half_blocks - 1), c))

    return pl.pallas_call(
        functools.partial(_dilated_kernel, class_len=class_len),
        grid=(b, dil, class_len // DIL_STEP),
        in_specs=[cur(cq), prev(ck), cur(ck), nxt(ck), prev(cv), cur(cv), nxt(cv)],
        out_specs=[pl.BlockSpec((1, 1, DIL_STEP, w), lambda bb, r, i: (bb, r, i, 0)),
                   pl.BlockSpec((1, 1, DIL_STEP, HEAD_DIM), lambda bb, r, i: (bb, r, i, 0))],
        out_shape=[jax.ShapeDtypeStruct((b, dil, class_len, w), BF16),
                   jax.ShapeDtypeStruct((b, dil, class_len, HEAD_DIM), F32)],
        compiler_params=_params("parallel", "parallel", "arbitrary"),
        name=f"dilated_attention_d{dil}",
    )(qkv, qkv, qkv, qkv, qkv, qkv, qkv)


def _dil_merge_kernel(o0_ref, l0_ref, o1_ref, l1_ref, o2_ref, l2_ref, out_ref,
                      so1, sl1, so2, sl2):
    for src_o, src_l, dst_o, dst_l in ((o1_ref, l1_ref, so1, sl1), (o2_ref, l2_ref, so2, sl2)):
        dil, n = src_o.shape[1], src_o.shape[2]
        for r in range(dil):
            dst_l[pl.ds(r, n, stride=dil), :] = src_l[0, r]
            for hh, sl in enumerate(_heads(DIL_GROUP_WIDTH)):
                dst_o[hh, pl.ds(r, n, stride=dil), :] = src_o[0, r, :, sl].astype(F32)
    a0, a1, a2 = l0_ref[0, 0], sl1[...], sl2[...]
    m = jnp.maximum(jnp.maximum(a0, a1), a2)
    e0, e1, e2 = jnp.exp2(a0 - m), jnp.exp2(a1 - m), jnp.exp2(a2 - m)
    inv = 1.0 / (e0 + e1 + e2)
    w0, w1, w2 = e0 * inv, e1 * inv, e2 * inv
    for hh, sl in enumerate(_heads(DIL_GROUP_WIDTH)):
        h1 = slice(hh, hh + 1)
        out = (w0[:, h1] * o0_ref[0, 0, :, sl].astype(F32) + w1[:, h1] * so1[hh]
               + w2[:, h1] * so2[hh])
        out_ref[0, :, sl] = out.astype(out_ref.dtype)


def dilated_merge(outs, tr=512):
    (o0, l0), (o1, l1), (o2, l2) = outs
    b, _, s_len, w = o0.shape

    def spec(a):
        dil = a.shape[1]
        return pl.BlockSpec((1, dil, tr // dil, a.shape[3]), lambda bb, i: (bb, 0, i, 0))

    slab_o = pltpu.VMEM((w // HEAD_DIM, tr, HEAD_DIM), F32)
    slab_l = pltpu.VMEM((tr, HEAD_DIM), F32)
    return pl.pallas_call(
        _dil_merge_kernel,
        grid=(b, s_len // tr),
        in_specs=[spec(o0), spec(l0), spec(o1), spec(l1), spec(o2), spec(l2)],
        out_specs=pl.BlockSpec((1, tr, w), lambda bb, i: (bb, i, 0)),
        out_shape=jax.ShapeDtypeStruct((b, s_len, w), BF16),
        scratch_shapes=[slab_o, slab_l, slab_o, slab_l],
        compiler_params=_params("parallel", "arbitrary"),
        name="dilated_merge",
    )(o0, l0, o1, l1, o2, l2)


def _diff_kernel(q_ref, k_ref, v_ref, lq1, lk1, lq2, lk2, g_ref, o_ref,
                 s_scr, p_scr, l_scr, *, lam_init):
    lam = (jnp.exp(jnp.sum(lq1[...] * lk1[...], axis=-1, keepdims=True))
           - jnp.exp(jnp.sum(lq2[...] * lk2[...], axis=-1, keepdims=True)) + lam_init)
    th = q_ref.shape[1] // 2
    n_keys = k_ref.shape[1]

    def scores(half, c, buf):
        sl = slice(c * HEAD_DIM, (c + 1) * HEAD_DIM)
        s_scr[buf] = _qkt(q_ref[0, half * th:(half + 1) * th, sl], k_ref[0, :, sl])

    def softmax(buf):
        for r in range(th // BF16_ROWS):
            rows = slice(r * BF16_ROWS, (r + 1) * BF16_ROWS)
            m = jnp.max(s_scr[buf, rows, :], axis=-1, keepdims=True)
            l = jnp.zeros((BF16_ROWS, 1), F32)
            for c0 in range(0, n_keys, SOFTMAX_CHUNK):
                cols = slice(c0, c0 + SOFTMAX_CHUNK)
                p = jnp.exp2(s_scr[buf, rows, cols] - m)
                l = l + jnp.sum(p, axis=-1, keepdims=True)
                p_scr[buf, rows, cols] = p.astype(BF16)
            l_scr[buf, rows, :] = jnp.broadcast_to(l, (BF16_ROWS, HEAD_DIM))

    def values(buf):
        return jnp.dot(p_scr[buf], v_ref[0], preferred_element_type=F32) / l_scr[buf, :, :1]

    def finish(half, o1, o2):
        o = o1 - lam * o2
        o_ref[0, half * th:(half + 1) * th, :] = (
            _rms(o, g_ref[...]) * (1.0 - lam_init)).astype(o_ref.dtype)

    scores(0, 0, 0)
    scores(0, 1, 1)
    softmax(0)
    scores(1, 0, 0)
    softmax(1)
    o_a1 = values(0)
    scores(1, 1, 1)
    softmax(0)
    o_a2 = values(1)
    finish(0, o_a1, o_a2)
    softmax(1)
    o_b1 = values(0)
    o_b2 = values(1)
    finish(1, o_b1, o_b2)


def differential_attention(proj, lq1, lk1, lq2, lk2, subln, lam_init, tq=1024):
    b, s, _ = proj.shape
    vec = pl.BlockSpec((1, HEAD_DIM), lambda bb, h, i: (0, 0))
    cq, ck, cv = P_QB // DIFF_V_DIM, P_KB // DIFF_V_DIM, P_VB // DIFF_V_DIM
    return pl.pallas_call(
        functools.partial(_diff_kernel, lam_init=lam_init),
        grid=(b, DIFF_HEADS, s // tq),
        in_specs=[pl.BlockSpec((1, tq, DIFF_V_DIM), lambda bb, h, i: (bb, i, cq + h)),
                  pl.BlockSpec((1, s, DIFF_V_DIM), lambda bb, h, i: (bb, 0, ck + h)),
                  pl.BlockSpec((1, s, DIFF_V_DIM), lambda bb, h, i: (bb, 0, cv + h)),
                  vec, vec, vec, vec,
                  pl.BlockSpec((1, DIFF_V_DIM), lambda bb, h, i: (0, 0))],
        out_specs=pl.BlockSpec((1, tq, DIFF_V_DIM), lambda bb, h, i: (bb, i, h)),
        out_shape=jax.ShapeDtypeStruct((b, s, DIFF_V_WIDTH), BF16),
        scratch_shapes=[pltpu.VMEM((2, tq // 2, s), F32), pltpu.VMEM((2, tq // 2, s), BF16),
                        pltpu.VMEM((2, tq // 2, HEAD_DIM), F32)],
        compiler_params=_params("parallel", "parallel", "arbitrary"),
        name="differential_attention",
    )(proj, proj, proj, lq1.reshape(1, -1), lk1.reshape(1, -1), lq2.reshape(1, -1),
      lk2.reshape(1, -1), subln.reshape(1, -1))


def _gate_kernel(oa_ref, wa_ref, ob_ref, wb_ref, ga_ref, gb_ref, o_ref):
    ya = jnp.dot(oa_ref[...], wa_ref[...].astype(BF16), preferred_element_type=F32)
    yb = jnp.dot(ob_ref[...], wb_ref[...].astype(BF16), preferred_element_type=F32)
    o = (jax.nn.sigmoid(ga_ref[...].astype(F32)) * ya
         + jax.nn.sigmoid(gb_ref[...].astype(F32)) * yb)
    o_ref[...] = o.astype(o_ref.dtype)


def gated_merge(out_a, w_a, out_b, w_b, proj2d, tm=1024, tn=512):
    t = out_a.shape[0]
    n = w_a.shape[1]
    ca = P_GA // tn
    cb = (P_GA + n) // tn
    return pl.pallas_call(
        _gate_kernel,
        grid=(t // tm, n // tn),
        in_specs=[pl.BlockSpec((tm, out_a.shape[1]), lambda i, j: (i, 0)),
                  pl.BlockSpec((w_a.shape[0], tn), lambda i, j: (0, j)),
                  pl.BlockSpec((tm, out_b.shape[1]), lambda i, j: (i, 0)),
                  pl.BlockSpec((w_b.shape[0], tn), lambda i, j: (0, j)),
                  pl.BlockSpec((tm, tn), lambda i, j: (i, ca + j)),
                  pl.BlockSpec((tm, tn), lambda i, j: (i, cb + j))],
        out_specs=pl.BlockSpec((tm, tn), lambda i, j: (i, j)),
        out_shape=jax.ShapeDtypeStruct((t, n), BF16),
        compiler_params=_params("parallel", "arbitrary"),
        name="gated_merge",
    )(out_a, w_a, out_b, w_b, proj2d, proj2d)


def _resnorm_kernel(y_ref, x_ref, gp_ref, gn_ref, xo_ref, ho_ref):
    xn = x_ref[...] + _rms(y_ref[...].astype(F32), gp_ref[...])
    xo_ref[...] = xn
    ho_ref[...] = _rms(xn, gn_ref[...]).astype(ho_ref.dtype)


def _resnorm_last_kernel(y_ref, x_ref, gp_ref, xo_ref):
    xo_ref[...] = x_ref[...] + _rms(y_ref[...].astype(F32), gp_ref[...])


def residual_norm(y, x, g_post, g_next=None, tr=256):
    t, d = x.shape
    row = pl.BlockSpec((tr, d), lambda i: (i, 0))
    vec = pl.BlockSpec((1, d), lambda i: (0, 0))
    if g_next is None:
        return pl.pallas_call(
            _resnorm_last_kernel,
            grid=(t // tr,),
            in_specs=[row, row, vec],
            out_specs=row,
            out_shape=jax.ShapeDtypeStruct((t, d), F32),
            compiler_params=_params("parallel"),
            name="residual_norm_last",
        )(y, x, g_post.reshape(1, d))
    return pl.pallas_call(
        _resnorm_kernel,
        grid=(t // tr,),
        in_specs=[row, row, vec, vec],
        out_specs=[row, row],
        out_shape=[jax.ShapeDtypeStruct((t, d), F32), jax.ShapeDtypeStruct((t, d), BF16)],
        compiler_params=_params("parallel"),
        name="residual_norm",
    )(y, x, g_post.reshape(1, d), g_next.reshape(1, d))


def _mem_block_kernel(y_ref, x_ref, gp1_ref, gpre_ref, wq_ref, kv_ref, wo_ref, gp2_ref, gn_ref,
                      x2_ref, hn_ref, o_scr):
    x1 = x_ref[...] + _rms(y_ref[...].astype(F32), gp1_ref[...])
    h = _rms(x1, gpre_ref[...]).astype(BF16)
    q = (jnp.dot(h, wq_ref[...], preferred_element_type=F32) * Q_PRESCALE).astype(BF16)
    for hh, sl in enumerate(_heads(MEM_WIDTH)):
        slv = slice(MEM_WIDTH + hh * HEAD_DIM, MEM_WIDTH + (hh + 1) * HEAD_DIM)
        s = _qkt(q[:, sl], kv_ref[0, :, sl])
        p = jnp.exp2(s - jnp.max(s, axis=-1, keepdims=True))
        l = jnp.sum(p, axis=-1, keepdims=True)
        o = jnp.dot(p.astype(BF16), kv_ref[0, :, slv], preferred_element_type=F32) / l
        o_scr[:, sl] = o.astype(o_scr.dtype)
    y2 = jnp.dot(o_scr[...], wo_ref[...], preferred_element_type=F32)
    x2 = x1 + _rms(y2, gp2_ref[...])
    x2_ref[...] = x2
    hn_ref[...] = _rms(x2, gn_ref[...]).astype(hn_ref.dtype)


def memory_block(y, x, g_post1, g_pre, w_q, kv, w_o, g_post2, g_next, seq_len, tr=256):
    t, d = x.shape
    tiles_per_seq = seq_len // tr
    row = pl.BlockSpec((tr, d), lambda i: (i, 0))
    vec = pl.BlockSpec((1, d), lambda i: (0, 0))
    once = pl.Buffered(1)
    return pl.pallas_call(
        _mem_block_kernel,
        grid=(t // tr,),
        in_specs=[row, row, vec, vec,
                  pl.BlockSpec(w_q.shape, lambda i: (0, 0), pipeline_mode=once),
                  pl.BlockSpec((1,) + kv.shape[1:], lambda i: (i // tiles_per_seq, 0, 0)),
                  pl.BlockSpec(w_o.shape, lambda i: (0, 0), pipeline_mode=once),
                  vec, vec],
        out_specs=[row, row],
        out_shape=[jax.ShapeDtypeStruct((t, d), F32), jax.ShapeDtypeStruct((t, d), BF16)],
        scratch_shapes=[pltpu.VMEM((tr, MEM_WIDTH), BF16)],
        compiler_params=_params("arbitrary"),
        name="memory_block",
    )(y, x, g_post1.reshape(1, d), g_pre.reshape(1, d), w_q, kv, w_o,
      g_post2.reshape(1, d), g_next.reshape(1, d))


def kernel(x, mem, positions, norm_mix_pre, w_in, w_a, w_b, w_mix_out, norm_mix_post,
           lambda_q1, lambda_k1, lambda_q2, lambda_k2, diff_subln,
           norm_mem_pre, norm_mem_kv, w_mem_q, w_mem_kv, w_mem_o, norm_mem_post,
           norm_mlp_pre, w_mlp_up, w_mlp_down, norm_mlp_post):
    b, s, d = x.shape
    t = b * s
    depth = w_in.shape[0]
    m_len = mem.shape[1]
    xt = x.reshape(t, d)
    memt = mem.reshape(b * m_len, d)
    cos, sin = rope_tables(positions)

    h = rmsnorm(xt, norm_mix_pre[0])
    for layer in range(depth):
        lam_init = 0.8 - 0.6 * float(np.exp(-0.3 * layer))

        w_in_b = w_in[layer]
        proj = in_projection(h, w_in_b, cos, sin)
        proj3 = proj.reshape(b, s, -1)
        tile = DIL_GROUP_WIDTH
        outs = [dilated_group(proj3.reshape(b, 1, s, -1), P_QA // tile, P_KA // tile, P_VA // tile)]
        for g, (_, dil) in enumerate(DIL_CONFIGS):
            if dil > 1:
                qkv = in_projection_dilated(h, w_in_b, cos, sin, g, dil, b)
                outs.append(dilated_group(qkv, 0, 1, 2))
        out_a = dilated_merge(outs).reshape(t, -1)
        out_b = differential_attention(proj3, lambda_q1[layer], lambda_k1[layer],
                                       lambda_q2[layer], lambda_k2[layer],
                                       diff_subln[layer], lam_init).reshape(t, -1)
        merged = gated_merge(out_a, w_a[layer], out_b, w_b[layer], proj)
        y = matmul(merged, w_mix_out[layer], BF16, 1024, 512, name="mix_out")

        mn = rmsnorm(memt, norm_mem_kv[layer])
        kv = matmul(mn, w_mem_kv[layer], BF16, b * m_len, 512, name="mem_kv")
        xt, h = memory_block(y, xt, norm_mix_post[layer], norm_mem_pre[layer],
                             w_mem_q[layer].astype(BF16), kv.reshape(b, m_len, -1),
                             w_mem_o[layer].astype(BF16), norm_mem_post[layer],
                             norm_mlp_pre[layer], s)

        u = matmul(h, w_mlp_up[layer], BF16, 1024, 512, epilogue="relu2", name="mlp_up")
        y = matmul_kgrid(u, w_mlp_down[layer], BF16, 2048, 1024, 1024, name="mlp_down")
        if layer + 1 < depth:
            xt, h = residual_norm(y, xt, norm_mlp_post[layer], norm_mix_pre[layer + 1])
        else:
            xt = residual_norm(y, xt, norm_mlp_post[layer])
    return xt.reshape(b, s, d)
```

```python
import functools

import numpy as np
import jax
import jax.numpy as jnp
from jax import lax
from jax.experimental import pallas as pl
from jax.experimental.pallas import tpu as pltpu

F32 = jnp.float32
BF16 = jnp.bfloat16

HEAD_DIM = 128
BF16_ROWS = 16
SOFTMAX_CHUNK = 1024
DIL_CONFIGS = ((128, 1), (512, 4), (2048, 16))
N_DIL_GROUPS = 3
DIL_HEADS = 8
DIL_GROUP_WIDTH = DIL_HEADS * HEAD_DIM
DIL_WIDTH = N_DIL_GROUPS * DIL_GROUP_WIDTH
DIFF_HEADS = 8
DIFF_QK_WIDTH = DIFF_HEADS * 2 * HEAD_DIM
DIFF_V_DIM = 2 * HEAD_DIM
DIFF_V_WIDTH = DIFF_HEADS * DIFF_V_DIM
MEM_HEADS = 4
MEM_WIDTH = MEM_HEADS * HEAD_DIM
ROPE_THETA = 10000.0
Q_BLOCK = 128
NORM_EPS = 1e-6
MASK_VALUE = -1e30
LOG2E = 1.4426950408889634
Q_PRESCALE = HEAD_DIM ** -0.5 * LOG2E

W_QA = 0
W_KA = W_QA + DIL_WIDTH
W_VA = W_KA + DIL_WIDTH
W_QB = W_VA + DIL_WIDTH

PROJ_TN = 512
P_QA = 0
P_KA = P_QA + DIL_GROUP_WIDTH
P_VA = P_KA + DIL_GROUP_WIDTH
P_QB = P_VA + DIL_GROUP_WIDTH
P_KB = P_QB + DIFF_QK_WIDTH
P_VB = P_KB + DIFF_QK_WIDTH
P_GA = P_VB + DIFF_V_WIDTH

V7X_VMEM_BYTES = 64 * 1024 * 1024
VMEM_LIMIT = V7X_VMEM_BYTES - 6 * 1024 * 1024


def _params(*sem):
    return pltpu.CompilerParams(dimension_semantics=sem, vmem_limit_bytes=VMEM_LIMIT)


def _rms(x, g):
    return x * lax.rsqrt(jnp.mean(x * x, axis=-1, keepdims=True) + NORM_EPS) * g


def _heads(width):
    return [slice(hh * HEAD_DIM, (hh + 1) * HEAD_DIM) for hh in range(width // HEAD_DIM)]


def _qkt(q, k):
    return lax.dot_general(q, k, (((1,), (1,)), ((), ())), preferred_element_type=F32)


def _rmsnorm_kernel(x_ref, g_ref, o_ref):
    o_ref[...] = _rms(x_ref[...], g_ref[...]).astype(o_ref.dtype)


def rmsnorm(x, g, tr=256):
    t, d = x.shape
    return pl.pallas_call(
        _rmsnorm_kernel,
        grid=(t // tr,),
        in_specs=[pl.BlockSpec((tr, d), lambda i: (i, 0)),
                  pl.BlockSpec((1, d), lambda i: (0, 0))],
        out_specs=pl.BlockSpec((tr, d), lambda i: (i, 0)),
        out_shape=jax.ShapeDtypeStruct((t, d), BF16),
        compiler_params=_params("parallel"),
        name="rmsnorm",
    )(x, g.reshape(1, d))


def _rmsnorm_rope_kernel(x_ref, g_ref, pos_ref, inv_ref, sign_ref, o_ref, cos_ref, sin_ref):
    o_ref[...] = _rms(x_ref[...], g_ref[...]).astype(o_ref.dtype)
    ang = pos_ref[...] * inv_ref[...]
    cos_ref[...] = jnp.cos(ang)
    sin_ref[...] = jnp.sin(ang) * sign_ref[...]


def rmsnorm_and_rope_tables(x, g, positions, tr=256):
    t, d = x.shape
    half = HEAD_DIM // 2
    inv = ROPE_THETA ** (-jnp.arange(0, HEAD_DIM, 2, dtype=F32) / HEAD_DIM)
    inv2 = jnp.concatenate([inv, inv]).reshape(1, HEAD_DIM)
    sign = jnp.concatenate([-jnp.ones((half,), F32), jnp.ones((half,), F32)]).reshape(1, HEAD_DIM)
    pos = positions.astype(F32).reshape(t, 1)
    vec = pl.BlockSpec((1, HEAD_DIM), lambda i: (0, 0))
    table = pl.BlockSpec((tr, HEAD_DIM), lambda i: (i, 0))
    return pl.pallas_call(
        _rmsnorm_rope_kernel,
        grid=(t // tr,),
        in_specs=[pl.BlockSpec((tr, d), lambda i: (i, 0)),
                  pl.BlockSpec((1, d), lambda i: (0, 0)),
                  pl.BlockSpec((tr, 1), lambda i: (i, 0)), vec, vec],
        out_specs=[pl.BlockSpec((tr, d), lambda i: (i, 0)), table, table],
        out_shape=[jax.ShapeDtypeStruct((t, d), BF16),
                   jax.ShapeDtypeStruct((t, HEAD_DIM), F32),
                   jax.ShapeDtypeStruct((t, HEAD_DIM), F32)],
        compiler_params=_params("parallel"),
        name="rmsnorm_rope_tables",
    )(x, g.reshape(1, d), pos, inv2, sign)


def _rope(xh, c, s):
    return xh * c + pltpu.roll(xh, HEAD_DIM // 2, 1) * s


def _inproj_kernel(h_ref, w_ref, cos_ref, sin_ref, o_ref):
    j = pl.program_id(1)
    acc = jnp.dot(h_ref[...], w_ref[...].astype(BF16), preferred_element_type=F32)
    is_q = (j < P_KA // PROJ_TN) | ((j >= P_QB // PROJ_TN) & (j < P_KB // PROJ_TN))
    is_k = (((j >= P_KA // PROJ_TN) & (j < P_VA // PROJ_TN))
            | ((j >= P_KB // PROJ_TN) & (j < P_VB // PROJ_TN)))

    @pl.when(is_q | is_k)
    def _():
        f = jnp.where(is_q, Q_PRESCALE, 1.0).astype(F32)
        c = cos_ref[...] * f
        s = sin_ref[...] * f
        for sl in _heads(PROJ_TN):
            o_ref[:, sl] = _rope(acc[:, sl], c, s).astype(o_ref.dtype)

    @pl.when(jnp.logical_not(is_q | is_k))
    def _():
        o_ref[...] = acc.astype(o_ref.dtype)


def in_projection(h, w, cos, sin, tm=1024):
    t, k = h.shape
    tn = PROJ_TN
    n_out = w.shape[1] - 3 * (DIL_WIDTH - DIL_GROUP_WIDTH)
    per = DIL_GROUP_WIDTH // tn
    n_group0 = 3 * per
    stride_a = DIL_WIDTH // tn
    skip = W_QB // tn - n_group0

    def w_map(i, j):
        return (0, jnp.where(j < n_group0, (j // per) * stride_a + j % per, j + skip))

    return pl.pallas_call(
        _inproj_kernel,
        grid=(t // tm, n_out // tn),
        in_specs=[pl.BlockSpec((tm, k), lambda i, j: (i, 0)),
                  pl.BlockSpec((k, tn), w_map),
                  pl.BlockSpec((tm, HEAD_DIM), lambda i, j: (i, 0)),
                  pl.BlockSpec((tm, HEAD_DIM), lambda i, j: (i, 0))],
        out_specs=pl.BlockSpec((tm, tn), lambda i, j: (i, j)),
        out_shape=jax.ShapeDtypeStruct((t, n_out), BF16),
        compiler_params=_params("parallel", "arbitrary"),
        name="in_projection",
    )(h, w, cos, sin)


def _inproj_dilated_kernel(h_ref, w_ref, cos_ref, sin_ref, o_ref, slab_ref, *, dil):
    seg = pl.program_id(1) // (DIL_GROUP_WIDTH // PROJ_TN)
    tm = h_ref.shape[0]
    acc = jnp.dot(h_ref[...], w_ref[...].astype(BF16), preferred_element_type=F32)
    f = jnp.where(seg == 0, Q_PRESCALE, 1.0).astype(F32)
    c = jnp.where(seg < 2, cos_ref[...] * f, 1.0)
    s = jnp.where(seg < 2, sin_ref[...] * f, 0.0)
    for hh, sl in enumerate(_heads(PROJ_TN)):
        slab_ref[hh] = _rope(acc[:, sl], c, s)
        for r in range(dil):
            o_ref[0, r, :, sl] = slab_ref[hh, pl.ds(r, tm // dil, stride=dil), :].astype(o_ref.dtype)


def in_projection_dilated(h, w, cos, sin, g, dil, batch, tm=1024):
    t, k = h.shape
    tn = PROJ_TN
    s_len = t // batch
    tiles_per_seq = s_len // tm
    per = DIL_GROUP_WIDTH // tn
    stride_a = DIL_WIDTH // tn
    return pl.pallas_call(
        functools.partial(_inproj_dilated_kernel, dil=dil),
        grid=(t // tm, 3 * per),
        in_specs=[pl.BlockSpec((tm, k), lambda i, j: (i, 0)),
                  pl.BlockSpec((k, tn), lambda i, j: (0, (j // per) * stride_a + g * per + j % per)),
                  pl.BlockSpec((tm, HEAD_DIM), lambda i, j: (i, 0)),
                  pl.BlockSpec((tm, HEAD_DIM), lambda i, j: (i, 0))],
        out_specs=pl.BlockSpec((1, dil, tm // dil, tn),
                               lambda i, j: (i // tiles_per_seq, 0, i % tiles_per_seq, j)),
        out_shape=jax.ShapeDtypeStruct((batch, dil, s_len // dil, 3 * DIL_GROUP_WIDTH), BF16),
        scratch_shapes=[pltpu.VMEM((tn // HEAD_DIM, tm, HEAD_DIM), F32)],
        compiler_params=_params("parallel", "arbitrary"),
        name=f"in_projection_d{dil}",
    )(h, w, cos, sin)


def _mm_kernel(a_ref, w_ref, o_ref, *, epilogue):
    acc = jnp.dot(a_ref[...], w_ref[...].astype(BF16), preferred_element_type=F32)
    if epilogue == "relu2":
        acc = jnp.square(jnp.maximum(acc, 0.0))
    o_ref[...] = acc.astype(o_ref.dtype)


def matmul(a, w, out_dtype, tm, tn, epilogue=None, name="matmul"):
    m, k = a.shape
    n = w.shape[1]
    return pl.pallas_call(
        functools.partial(_mm_kernel, epilogue=epilogue),
        grid=(m // tm, n // tn),
        in_specs=[pl.BlockSpec((tm, k), lambda i, j: (i, 0)),
                  pl.BlockSpec((k, tn), lambda i, j: (0, j))],
        out_specs=pl.BlockSpec((tm, tn), lambda i, j: (i, j)),
        out_shape=jax.ShapeDtypeStruct((m, n), out_dtype),
        compiler_params=_params("parallel", "arbitrary"),
        name=name,
    )(a, w)


def _mm_kgrid_kernel(a_ref, w_ref, o_ref, acc_ref):
    kk = pl.program_id(2)

    @pl.when(kk == 0)
    def _():
        acc_ref[...] = jnp.zeros_like(acc_ref)

    acc_ref[...] += jnp.dot(a_ref[...], w_ref[...].astype(BF16), preferred_element_type=F32)

    @pl.when(kk == pl.num_programs(2) - 1)
    def _():
        o_ref[...] = acc_ref[...].astype(o_ref.dtype)


def matmul_kgrid(a, w, out_dtype, tm, tn, tk, name="matmul_kgrid"):
    m, k = a.shape
    n = w.shape[1]
    return pl.pallas_call(
        _mm_kgrid_kernel,
        grid=(m // tm, n // tn, k // tk),
        in_specs=[pl.BlockSpec((tm, tk), lambda i, j, kk: (i, kk)),
                  pl.BlockSpec((tk, tn), lambda i, j, kk: (kk, j))],
        out_specs=pl.BlockSpec((tm, tn), lambda i, j, kk: (i, j)),
        out_shape=jax.ShapeDtypeStruct((m, n), out_dtype),
        scratch_shapes=[pltpu.VMEM((tm, tn), F32)],
        compiler_params=_params("parallel", "parallel", "arbitrary"),
        name=name,
    )(a, w)


DIL_STEP = 2 * Q_BLOCK


def _dilated_kernel(q_ref, kp_ref, kc_ref, kn_ref, vp_ref, vc_ref, vn_ref,
                    o_ref, lse_ref, *, class_len):
    i = pl.program_id(2)
    half = Q_BLOCK // 2
    nk = 2 * Q_BLOCK
    row = lax.broadcasted_iota(jnp.int32, (Q_BLOCK, nk), 0)
    col = lax.broadcasted_iota(jnp.int32, (Q_BLOCK, nk), 1)
    lane = lax.broadcasted_iota(jnp.int32, (Q_BLOCK, HEAD_DIM), 1)
    rel = col - half - row
    in_band = (rel >= -half) & (rel <= half)
    for sb in range(DIL_STEP // Q_BLOCK):
        q0 = sb * Q_BLOCK
        kpos = i * DIL_STEP + q0 - half + col
        valid = in_band & (kpos >= 0) & (kpos < class_len)
        lse_tile = jnp.zeros((Q_BLOCK, HEAD_DIM), F32)
        for hh, sl in enumerate(_heads(DIL_GROUP_WIDTH)):
            q = q_ref[0, 0, q0:q0 + Q_BLOCK, sl]
            if sb == 0:
                k = jnp.concatenate([kp_ref[0, 0, :, sl], kc_ref[0, 0, :nk - half, sl]], axis=0)
                v = jnp.concatenate([vp_ref[0, 0, :, sl], vc_ref[0, 0, :nk - half, sl]], axis=0)
            else:
                k = jnp.concatenate([kc_ref[0, 0, half:, sl], kn_ref[0, 0, :, sl]], axis=0)
                v = jnp.concatenate([vc_ref[0, 0, half:, sl], vn_ref[0, 0, :, sl]], axis=0)
            s = jnp.where(valid, _qkt(q, k), MASK_VALUE)
            m = jnp.max(s, axis=-1, keepdims=True)
            p = jnp.exp2(s - m)
            l = jnp.sum(p, axis=-1, keepdims=True)
            o = jnp.dot(p.astype(BF16), v, preferred_element_type=F32) / l
            o_ref[0, 0, q0:q0 + Q_BLOCK, sl] = o.astype(o_ref.dtype)
            lse_tile = jnp.where(lane == hh, m + jnp.log2(l), lse_tile)
        lse_ref[0, 0, q0:q0 + Q_BLOCK, :] = lse_tile


def dilated_group(qkv, cq, ck, cv):
    b, dil, class_len, _ = qkv.shape
    half = Q_BLOCK // 2
    per_step = DIL_STEP // half
    n_half_blocks = class_len // half
    w = DIL_GROUP_WIDTH

    def cur(c):
        return pl.BlockSpec((1, 1, DIL_STEP, w), lambda bb, r, i: (bb, r, i, c))

    def prev(c):
        return pl.BlockSpec((1, 1, half, w),
                            lambda bb, r, i: (bb, r, jnp.maximum(per_step * i - 1, 0), c))

    def nxt(c):
        return pl.BlockSpec((1, 1, half, w),
                            lambda bb, r, i: (bb, r, jnp.minimum(per_step * (i + 1), n_half_blocks - 1), c))

    return pl.pallas_call(
        functools.partial(_dilated_kernel, class_len=class_len),
        grid=(b, dil, class_len // DIL_STEP),
        in_specs=[cur(cq), prev(ck), cur(ck), nxt(ck), prev(cv), cur(cv), nxt(cv)],
        out_specs=[pl.BlockSpec((1, 1, DIL_STEP, w), lambda bb, r, i: (bb, r, i, 0)),
                   pl.BlockSpec((1, 1, DIL_STEP, HEAD_DIM), lambda bb, r, i: (bb, r, i, 0))],
        out_shape=[jax.ShapeDtypeStruct((b, dil, class_len, w), BF16),
                   jax.ShapeDtypeStruct((b, dil, class_len, HEAD_DIM), F32)],
        compiler_params=_params("parallel", "parallel", "arbitrary"),
        name=f"dilated_attention_d{dil}",
    )(qkv, qkv, qkv, qkv, qkv, qkv, qkv)


def _dil_merge_kernel(o0_ref, l0_ref, o1_ref, l1_ref, o2_ref, l2_ref, out_ref,
                      so1, sl1, so2, sl2):
    for src_o, src_l, dst_o, dst_l in ((o1_ref, l1_ref, so1, sl1), (o2_ref, l2_ref, so2, sl2)):
        dil, n = src_o.shape[1], src_o.shape[2]
        for r in range(dil):
            dst_l[pl.ds(r, n, stride=dil), :] = src_l[0, r]
            for hh, sl in enumerate(_heads(DIL_GROUP_WIDTH)):
                dst_o[hh, pl.ds(r, n, stride=dil), :] = src_o[0, r, :, sl].astype(F32)
    a0, a1, a2 = l0_ref[0, 0], sl1[...], sl2[...]
    m = jnp.maximum(jnp.maximum(a0, a1), a2)
    e0, e1, e2 = jnp.exp2(a0 - m), jnp.exp2(a1 - m), jnp.exp2(a2 - m)
    inv = 1.0 / (e0 + e1 + e2)
    w0, w1, w2 = e0 * inv, e1 * inv, e2 * inv
    for hh, sl in enumerate(_heads(DIL_GROUP_WIDTH)):
        h1 = slice(hh, hh + 1)
        out = (w0[:, h1] * o0_ref[0, 0, :, sl].astype(F32) + w1[:, h1] * so1[hh]
               + w2[:, h1] * so2[hh])
        out_ref[0, :, sl] = out.astype(out_ref.dtype)


def dilated_merge(outs, tr=512):
    (o0, l0), (o1, l1), (o2, l2) = outs
    b, _, s_len, w = o0.shape

    def spec(a):
        dil = a.shape[1]
        return pl.BlockSpec((1, dil, tr // dil, a.shape[3]), lambda bb, i: (bb, 0, i, 0))

    slab_o = pltpu.VMEM((w // HEAD_DIM, tr, HEAD_DIM), F32)
    slab_l = pltpu.VMEM((tr, HEAD_DIM), F32)
    return pl.pallas_call(
        _dil_merge_kernel,
        grid=(b, s_len // tr),
        in_specs=[spec(o0), spec(l0), spec(o1), spec(l1), spec(o2), spec(l2)],
        out_specs=pl.BlockSpec((1, tr, w), lambda bb, i: (bb, i, 0)),
        out_shape=jax.ShapeDtypeStruct((b, s_len, w), BF16),
        scratch_shapes=[slab_o, slab_l, slab_o, slab_l],
        compiler_params=_params("parallel", "arbitrary"),
        name="dilated_merge",
    )(o0, l0, o1, l1, o2, l2)


def _diff_kernel(q_ref, k_ref, v_ref, lq1, lk1, lq2, lk2, g_ref, o_ref,
                 s_scr, p_scr, l_scr, *, lam_init):
    lam = (jnp.exp(jnp.sum(lq1[...] * lk1[...], axis=-1, keepdims=True))
           - jnp.exp(jnp.sum(lq2[...] * lk2[...], axis=-1, keepdims=True)) + lam_init)
    th = q_ref.shape[1] // 2
    n_keys = k_ref.shape[1]

    def scores(half, c, buf):
        sl = slice(c * HEAD_DIM, (c + 1) * HEAD_DIM)
        s_scr[buf] = _qkt(q_ref[0, half * th:(half + 1) * th, sl], k_ref[0, :, sl])

    def softmax(buf):
        for r in range(th // BF16_ROWS):
            rows = slice(r * BF16_ROWS, (r + 1) * BF16_ROWS)
            m = jnp.max(s_scr[buf, rows, :], axis=-1, keepdims=True)
            l = jnp.zeros((BF16_ROWS, 1), F32)
            for c0 in range(0, n_keys, SOFTMAX_CHUNK):
                cols = slice(c0, c0 + SOFTMAX_CHUNK)
                p = jnp.exp2(s_scr[buf, rows, cols] - m)
                l = l + jnp.sum(p, axis=-1, keepdims=True)
                p_scr[buf, rows, cols] = p.astype(BF16)
            l_scr[buf, rows, :] = jnp.broadcast_to(l, (BF16_ROWS, HEAD_DIM))

    def values(buf):
        return jnp.dot(p_scr[buf], v_ref[0], preferred_element_type=F32) / l_scr[buf, :, :1]

    def finish(half, o1, o2):
        o = o1 - lam * o2
        o_ref[0, half * th:(half + 1) * th, :] = (
            _rms(o, g_ref[...]) * (1.0 - lam_init)).astype(o_ref.dtype)

    scores(0, 0, 0)
    scores(0, 1, 1)
    softmax(0)
    scores(1, 0, 0)
    softmax(1)
    o_a1 = values(0)
    scores(1, 1, 1)
    softmax(0)
    o_a2 = values(1)
    finish(0, o_a1, o_a2)
    softmax(1)
    o_b1 = values(0)
    o_b2 = values(1)
    finish(1, o_b1, o_b2)


def differential_attention(proj, lq1, lk1, lq2, lk2, subln, lam_init, tq=1024):
    b, s, _ = proj.shape
    vec = pl.BlockSpec((1, HEAD_DIM), lambda bb, h, i: (0, 0))
    cq, ck, cv = P_QB // DIFF_V_DIM, P_KB // DIFF_V_DIM, P_VB // DIFF_V_DIM
    return pl.pallas_call(
        functools.partial(_diff_kernel, lam_init=lam_init),
        grid=(b, DIFF_HEADS, s // tq),
        in_specs=[pl.BlockSpec((1, tq, DIFF_V_DIM), lambda bb, h, i: (bb, i, cq + h)),
                  pl.BlockSpec((1, s, DIFF_V_DIM), lambda bb, h, i: (bb, 0, ck + h)),
                  pl.BlockSpec((1, s, DIFF_V_DIM), lambda bb, h, i: (bb, 0, cv + h)),
                  vec, vec, vec, vec,
                  pl.BlockSpec((1, DIFF_V_DIM), lambda bb, h, i: (0, 0))],
        out_specs=pl.BlockSpec((1, tq, DIFF_V_DIM), lambda bb, h, i: (bb, i, h)),
        out_shape=jax.ShapeDtypeStruct((b, s, DIFF_V_WIDTH), BF16),
        scratch_shapes=[pltpu.VMEM((2, tq // 2, s), F32), pltpu.VMEM((2, tq // 2, s), BF16),
                        pltpu.VMEM((2, tq // 2, HEAD_DIM), F32)],
        compiler_params=_params("parallel", "parallel", "arbitrary"),
        name="differential_attention",
    )(proj, proj, proj, lq1.reshape(1, -1), lk1.reshape(1, -1), lq2.reshape(1, -1),
      lk2.reshape(1, -1), subln.reshape(1, -1))


def _gate_kernel(oa_ref, wa_ref, ob_ref, wb_ref, ga_ref, gb_ref, o_ref):
    ya = jnp.dot(oa_ref[...], wa_ref[...].astype(BF16), preferred_element_type=F32)
    yb = jnp.dot(ob_ref[...], wb_ref[...].astype(BF16), preferred_element_type=F32)
    o = (jax.nn.sigmoid(ga_ref[...].astype(F32)) * ya
         + jax.nn.sigmoid(gb_ref[...].astype(F32)) * yb)
    o_ref[...] = o.astype(o_ref.dtype)


def gated_merge(out_a, w_a, out_b, w_b, proj2d, tm=1024, tn=512):
    t = out_a.shape[0]
    n = w_a.shape[1]
    ca = P_GA // tn
    cb = (P_GA + n) // tn
    return pl.pallas_call(
        _gate_kernel,
        grid=(t // tm, n // tn),
        in_specs=[pl.BlockSpec((tm, out_a.shape[1]), lambda i, j: (i, 0)),
                  pl.BlockSpec((w_a.shape[0], tn), lambda i, j: (0, j)),
                  pl.BlockSpec((tm, out_b.shape[1]), lambda i, j: (i, 0)),
                  pl.BlockSpec((w_b.shape[0], tn), lambda i, j: (0, j)),
                  pl.BlockSpec((tm, tn), lambda i, j: (i, ca + j)),
                  pl.BlockSpec((tm, tn), lambda i, j: (i, cb + j))],
        out_specs=pl.BlockSpec((tm, tn), lambda i, j: (i, j)),
        out_shape=jax.ShapeDtypeStruct((t, n), BF16),
        compiler_params=_params("parallel", "arbitrary"),
        name="gated_merge",
    )(out_a, w_a, out_b, w_b, proj2d, proj2d)


def _resnorm_kernel(y_ref, x_ref, gp_ref, gn_ref, xo_ref, ho_ref):
    xn = x_ref[...] + _rms(y_ref[...].astype(F32), gp_ref[...])
    xo_ref[...] = xn
    ho_ref[...] = _rms(xn, gn_ref[...]).astype(ho_ref.dtype)


def _resnorm_last_kernel(y_ref, x_ref, gp_ref, xo_ref):
    xo_ref[...] = x_ref[...] + _rms(y_ref[...].astype(F32), gp_ref[...])


def residual_norm(y, x, g_post, g_next=None, tr=256):
    t, d = x.shape
    row = pl.BlockSpec((tr, d), lambda i: (i, 0))
    vec = pl.BlockSpec((1, d), lambda i: (0, 0))
    if g_next is None:
        return pl.pallas_call(
            _resnorm_last_kernel,
            grid=(t // tr,),
            in_specs=[row, row, vec],
            out_specs=row,
            out_shape=jax.ShapeDtypeStruct((t, d), F32),
            compiler_params=_params("parallel"),
            name="residual_norm_last",
        )(y, x, g_post.reshape(1, d))
    return pl.pallas_call(
        _resnorm_kernel,
        grid=(t // tr,),
        in_specs=[row, row, vec, vec],
        out_specs=[row, row],
        out_shape=[jax.ShapeDtypeStruct((t, d), F32), jax.ShapeDtypeStruct((t, d), BF16)],
        compiler_params=_params("parallel"),
        name="residual_norm",
    )(y, x, g_post.reshape(1, d), g_next.reshape(1, d))


def _mem_block_kernel(y_ref, x_ref, gp1_ref, gpre_ref, wq_ref, kv_ref, wo_ref, gp2_ref, gn_ref,
                      x2_ref, hn_ref, o_scr):
    x1 = x_ref[...] + _rms(y_ref[...].astype(F32), gp1_ref[...])
    h = _rms(x1, gpre_ref[...]).astype(BF16)
    q = (jnp.dot(h, wq_ref[...], preferred_element_type=F32) * Q_PRESCALE).astype(BF16)
    for hh, sl in enumerate(_heads(MEM_WIDTH)):
        slv = slice(MEM_WIDTH + hh * HEAD_DIM, MEM_WIDTH + (hh + 1) * HEAD_DIM)
        s = _qkt(q[:, sl], kv_ref[0, :, sl])
        p = jnp.exp2(s - jnp.max(s, axis=-1, keepdims=True))
        l = jnp.sum(p, axis=-1, keepdims=True)
        o = jnp.dot(p.astype(BF16), kv_ref[0, :, slv], preferred_element_type=F32) / l
        o_scr[:, sl] = o.astype(o_scr.dtype)
    y2 = jnp.dot(o_scr[...], wo_ref[...], preferred_element_type=F32)
    x2 = x1 + _rms(y2, gp2_ref[...])
    x2_ref[...] = x2
    hn_ref[...] = _rms(x2, gn_ref[...]).astype(hn_ref.dtype)


def memory_block(y, x, g_post1, g_pre, w_q, kv, w_o, g_post2, g_next, seq_len, tr=256):
    t, d = x.shape
    tiles_per_seq = seq_len // tr
    row = pl.BlockSpec((tr, d), lambda i: (i, 0))
    vec = pl.BlockSpec((1, d), lambda i: (0, 0))
    once = pl.Buffered(1)
    return pl.pallas_call(
        _mem_block_kernel,
        grid=(t // tr,),
        in_specs=[row, row, vec, vec,
                  pl.BlockSpec(w_q.shape, lambda i: (0, 0), pipeline_mode=once),
                  pl.BlockSpec((1,) + kv.shape[1:], lambda i: (i // tiles_per_seq, 0, 0)),
                  pl.BlockSpec(w_o.shape, lambda i: (0, 0), pipeline_mode=once),
                  vec, vec],
        out_specs=[row, row],
        out_shape=[jax.ShapeDtypeStruct((t, d), F32), jax.ShapeDtypeStruct((t, d), BF16)],
        scratch_shapes=[pltpu.VMEM((tr, MEM_WIDTH), BF16)],
        compiler_params=_params("arbitrary"),
        name="memory_block",
    )(y, x, g_post1.reshape(1, d), g_pre.reshape(1, d), w_q, kv, w_o,
      g_post2.reshape(1, d), g_next.reshape(1, d))


def kernel(x, mem, positions, norm_mix_pre, w_in, w_a, w_b, w_mix_out, norm_mix_post,
           lambda_q1, lambda_k1, lambda_q2, lambda_k2, diff_subln,
           norm_mem_pre, norm_mem_kv, w_mem_q, w_mem_kv, w_mem_o, norm_mem_post,
           norm_mlp_pre, w_mlp_up, w_mlp_down, norm_mlp_post):
    b, s, d = x.shape
    t = b * s
    depth = w_in.shape[0]
    m_len = mem.shape[1]
    xt = x.reshape(t, d)
    memt = mem.reshape(b * m_len, d)
    h, cos, sin = rmsnorm_and_rope_tables(xt, norm_mix_pre[0], positions)
    for layer in range(depth):
        lam_init = 0.8 - 0.6 * float(np.exp(-0.3 * layer))

        w_in_b = w_in[layer]
        proj = in_projection(h, w_in_b, cos, sin)
        proj3 = proj.reshape(b, s, -1)
        tile = DIL_GROUP_WIDTH
        outs = [dilated_group(proj3.reshape(b, 1, s, -1), P_QA // tile, P_KA // tile, P_VA // tile)]
        for g, (_, dil) in enumerate(DIL_CONFIGS):
            if dil > 1:
                qkv = in_projection_dilated(h, w_in_b, cos, sin, g, dil, b)
                outs.append(dilated_group(qkv, 0, 1, 2))
        out_a = dilated_merge(outs).reshape(t, -1)
        out_b = differential_attention(proj3, lambda_q1[layer], lambda_k1[layer],
                                       lambda_q2[layer], lambda_k2[layer],
                                       diff_subln[layer], lam_init).reshape(t, -1)
        merged = gated_merge(out_a, w_a[layer], out_b, w_b[layer], proj)
        y = matmul(merged, w_mix_out[layer], BF16, 1024, 512, name="mix_out")

        mn = rmsnorm(memt, norm_mem_kv[layer])
        kv = matmul(mn, w_mem_kv[layer], BF16, b * m_len, 512, name="mem_kv")
        xt, h = memory_block(y, xt, norm_mix_post[layer], norm_mem_pre[layer],
                             w_mem_q[layer].astype(BF16), kv.reshape(b, m_len, -1),
                             w_mem_o[layer].astype(BF16), norm_mem_post[layer],
                             norm_mlp_pre[layer], s)

        u = matmul(h, w_mlp_up[layer], BF16, 1024, 512, epilogue="relu2", name="mlp_up")
        y = matmul_kgrid(u, w_mlp_down[layer], BF16, 1024, 1024, 2048, name="mlp_down")
        if layer + 1 < depth:
            xt, h = residual_norm(y, xt, norm_mlp_post[layer], norm_mix_pre[layer + 1])
        else:
            xt = residual_norm(y, xt, norm_mlp_post[layer])
    return xt.reshape(b, s, d)
```

```python
import functools

import numpy as np
import jax
import jax.numpy as jnp
from jax import lax
from jax.experimental import pallas as pl
from jax.experimental.pallas import tpu as pltpu

F32 = jnp.float32
BF16 = jnp.bfloat16

HEAD_DIM = 128
BF16_ROWS = 16
SOFTMAX_CHUNK = 1024
DIL_CONFIGS = ((128, 1), (512, 4), (2048, 16))
N_DIL_GROUPS = 3
DIL_HEADS = 8
DIL_GROUP_WIDTH = DIL_HEADS * HEAD_DIM
DIL_WIDTH = N_DIL_GROUPS * DIL_GROUP_WIDTH
DIFF_HEADS = 8
DIFF_QK_WIDTH = DIFF_HEADS * 2 * HEAD_DIM
DIFF_V_DIM = 2 * HEAD_DIM
DIFF_V_WIDTH = DIFF_HEADS * DIFF_V_DIM
MEM_HEADS = 4
MEM_WIDTH = MEM_HEADS * HEAD_DIM
ROPE_THETA = 10000.0
Q_BLOCK = 128
NORM_EPS = 1e-6
MASK_VALUE = -1e30
LOG2E = 1.4426950408889634
Q_PRESCALE = HEAD_DIM ** -0.5 * LOG2E

W_QA = 0
W_KA = W_QA + DIL_WIDTH
W_VA = W_KA + DIL_WIDTH
W_QB = W_VA + DIL_WIDTH

PROJ_TN = 512
P_QA = 0
P_KA = P_QA + DIL_GROUP_WIDTH
P_VA = P_KA + DIL_GROUP_WIDTH
P_QB = P_VA + DIL_GROUP_WIDTH
P_KB = P_QB + DIFF_QK_WIDTH
P_VB = P_KB + DIFF_QK_WIDTH
P_GA = P_VB + DIFF_V_WIDTH

V7X_VMEM_BYTES = 64 * 1024 * 1024
VMEM_LIMIT = V7X_VMEM_BYTES - 6 * 1024 * 1024


def _params(*sem):
    return pltpu.CompilerParams(dimension_semantics=sem, vmem_limit_bytes=VMEM_LIMIT)


def _rms(x, g):
    return x * lax.rsqrt(jnp.mean(x * x, axis=-1, keepdims=True) + NORM_EPS) * g


def _heads(width):
    return [slice(hh * HEAD_DIM, (hh + 1) * HEAD_DIM) for hh in range(width // HEAD_DIM)]


def _qkt(q, k):
    return lax.dot_general(q, k, (((1,), (1,)), ((), ())), preferred_element_type=F32)


def _serpentine(i, j, n):
    return jnp.where(i % 2 == 0, j, n - 1 - j)


def _rmsnorm_kernel(x_ref, g_ref, o_ref):
    o_ref[...] = _rms(x_ref[...], g_ref[...]).astype(o_ref.dtype)


def rmsnorm(x, g, tr=256):
    t, d = x.shape
    return pl.pallas_call(
        _rmsnorm_kernel,
        grid=(t // tr,),
        in_specs=[pl.BlockSpec((tr, d), lambda i: (i, 0)),
                  pl.BlockSpec((1, d), lambda i: (0, 0))],
        out_specs=pl.BlockSpec((tr, d), lambda i: (i, 0)),
        out_shape=jax.ShapeDtypeStruct((t, d), BF16),
        compiler_params=_params("parallel"),
        name="rmsnorm",
    )(x, g.reshape(1, d))


def _rmsnorm_rope_kernel(x_ref, g_ref, pos_ref, inv_ref, sign_ref, o_ref, cos_ref, sin_ref):
    o_ref[...] = _rms(x_ref[...], g_ref[...]).astype(o_ref.dtype)
    ang = pos_ref[...] * inv_ref[...]
    cos_ref[...] = jnp.cos(ang)
    sin_ref[...] = jnp.sin(ang) * sign_ref[...]


def rmsnorm_and_rope_tables(x, g, positions, tr=256):
    t, d = x.shape
    half = HEAD_DIM // 2
    inv = ROPE_THETA ** (-jnp.arange(0, HEAD_DIM, 2, dtype=F32) / HEAD_DIM)
    inv2 = jnp.concatenate([inv, inv]).reshape(1, HEAD_DIM)
    sign = jnp.concatenate([-jnp.ones((half,), F32), jnp.ones((half,), F32)]).reshape(1, HEAD_DIM)
    pos = positions.astype(F32).reshape(t, 1)
    vec = pl.BlockSpec((1, HEAD_DIM), lambda i: (0, 0))
    table = pl.BlockSpec((tr, HEAD_DIM), lambda i: (i, 0))
    return pl.pallas_call(
        _rmsnorm_rope_kernel,
        grid=(t // tr,),
        in_specs=[pl.BlockSpec((tr, d), lambda i: (i, 0)),
                  pl.BlockSpec((1, d), lambda i: (0, 0)),
                  pl.BlockSpec((tr, 1), lambda i: (i, 0)), vec, vec],
        out_specs=[pl.BlockSpec((tr, d), lambda i: (i, 0)), table, table],
        out_shape=[jax.ShapeDtypeStruct((t, d), BF16),
                   jax.ShapeDtypeStruct((t, HEAD_DIM), F32),
                   jax.ShapeDtypeStruct((t, HEAD_DIM), F32)],
        compiler_params=_params("parallel"),
        name="rmsnorm_rope_tables",
    )(x, g.reshape(1, d), pos, inv2, sign)


def _rope(xh, c, s):
    return xh * c + pltpu.roll(xh, HEAD_DIM // 2, 1) * s


def _inproj_kernel(h_ref, w_ref, cos_ref, sin_ref, o_ref):
    j = _serpentine(pl.program_id(0), pl.program_id(1), pl.num_programs(1))
    acc = jnp.dot(h_ref[...], w_ref[...].astype(BF16), preferred_element_type=F32)
    is_q = (j < P_KA // PROJ_TN) | ((j >= P_QB // PROJ_TN) & (j < P_KB // PROJ_TN))
    is_k = (((j >= P_KA // PROJ_TN) & (j < P_VA // PROJ_TN))
            | ((j >= P_KB // PROJ_TN) & (j < P_VB // PROJ_TN)))

    @pl.when(is_q | is_k)
    def _():
        f = jnp.where(is_q, Q_PRESCALE, 1.0).astype(F32)
        c = cos_ref[...] * f
        s = sin_ref[...] * f
        for sl in _heads(PROJ_TN):
            o_ref[:, sl] = _rope(acc[:, sl], c, s).astype(o_ref.dtype)

    @pl.when(jnp.logical_not(is_q | is_k))
    def _():
        o_ref[...] = acc.astype(o_ref.dtype)


def in_projection(h, w, cos, sin, tm=1024):
    t, k = h.shape
    tn = PROJ_TN
    n_out = w.shape[1] - 3 * (DIL_WIDTH - DIL_GROUP_WIDTH)
    per = DIL_GROUP_WIDTH // tn
    n_group0 = 3 * per
    stride_a = DIL_WIDTH // tn
    skip = W_QB // tn - n_group0

    n_tiles = n_out // tn

    def w_map(i, j):
        jc = _serpentine(i, j, n_tiles)
        return (0, jnp.where(jc < n_group0, (jc // per) * stride_a + jc % per, jc + skip))

    return pl.pallas_call(
        _inproj_kernel,
        grid=(t // tm, n_tiles),
        in_specs=[pl.BlockSpec((tm, k), lambda i, j: (i, 0)),
                  pl.BlockSpec((k, tn), w_map),
                  pl.BlockSpec((tm, HEAD_DIM), lambda i, j: (i, 0)),
                  pl.BlockSpec((tm, HEAD_DIM), lambda i, j: (i, 0))],
        out_specs=pl.BlockSpec((tm, tn), lambda i, j: (i, _serpentine(i, j, n_tiles))),
        out_shape=jax.ShapeDtypeStruct((t, n_out), BF16),
        compiler_params=_params("parallel", "arbitrary"),
        name="in_projection",
    )(h, w, cos, sin)


def _inproj_dilated_kernel(h_ref, w_ref, cos_ref, sin_ref, o_ref, slab_ref, *, dil):
    j = _serpentine(pl.program_id(0), pl.program_id(1), pl.num_programs(1))
    seg = j // (DIL_GROUP_WIDTH // PROJ_TN)
    tm = h_ref.shape[0]
    acc = jnp.dot(h_ref[...], w_ref[...].astype(BF16), preferred_element_type=F32)
    f = jnp.where(seg == 0, Q_PRESCALE, 1.0).astype(F32)
    c = jnp.where(seg < 2, cos_ref[...] * f, 1.0)
    s = jnp.where(seg < 2, sin_ref[...] * f, 0.0)
    for hh, sl in enumerate(_heads(PROJ_TN)):
        slab_ref[hh] = _rope(acc[:, sl], c, s)
        for r in range(dil):
            o_ref[0, r, :, sl] = slab_ref[hh, pl.ds(r, tm // dil, stride=dil), :].astype(o_ref.dtype)


def in_projection_dilated(h, w, cos, sin, g, dil, batch, tm=1024):
    t, k = h.shape
    tn = PROJ_TN
    s_len = t // batch
    tiles_per_seq = s_len // tm
    per = DIL_GROUP_WIDTH // tn
    stride_a = DIL_WIDTH // tn
    n_tiles = 3 * per

    def w_map(i, j):
        jc = _serpentine(i, j, n_tiles)
        return (0, (jc // per) * stride_a + g * per + jc % per)

    return pl.pallas_call(
        functools.partial(_inproj_dilated_kernel, dil=dil),
        grid=(t // tm, n_tiles),
        in_specs=[pl.BlockSpec((tm, k), lambda i, j: (i, 0)),
                  pl.BlockSpec((k, tn), w_map),
                  pl.BlockSpec((tm, HEAD_DIM), lambda i, j: (i, 0)),
                  pl.BlockSpec((tm, HEAD_DIM), lambda i, j: (i, 0))],
        out_specs=pl.BlockSpec((1, dil, tm // dil, tn),
                               lambda i, j: (i // tiles_per_seq, 0, i % tiles_per_seq,
                                             _serpentine(i, j, n_tiles))),
        out_shape=jax.ShapeDtypeStruct((batch, dil, s_len // dil, 3 * DIL_GROUP_WIDTH), BF16),
        scratch_shapes=[pltpu.VMEM((tn // HEAD_DIM, tm, HEAD_DIM), F32)],
        compiler_params=_params("parallel", "arbitrary"),
        name=f"in_projection_d{dil}",
    )(h, w, cos, sin)


def _mm_kernel(a_ref, w_ref, o_ref, *, epilogue):
    acc = jnp.dot(a_ref[...], w_ref[...].astype(BF16), preferred_element_type=F32)
    if epilogue == "relu2":
        acc = jnp.square(jnp.maximum(acc, 0.0))
    o_ref[...] = acc.astype(o_ref.dtype)


def matmul(a, w, out_dtype, tm, tn, epilogue=None, name="matmul"):
    m, k = a.shape
    n = w.shape[1]
    n_tiles = n // tn
    return pl.pallas_call(
        functools.partial(_mm_kernel, epilogue=epilogue),
        grid=(m // tm, n_tiles),
        in_specs=[pl.BlockSpec((tm, k), lambda i, j: (i, 0)),
                  pl.BlockSpec((k, tn), lambda i, j: (0, _serpentine(i, j, n_tiles)))],
        out_specs=pl.BlockSpec((tm, tn), lambda i, j: (i, _serpentine(i, j, n_tiles))),
        out_shape=jax.ShapeDtypeStruct((m, n), out_dtype),
        compiler_params=_params("parallel", "arbitrary"),
        name=name,
    )(a, w)


def _mm_kgrid_kernel(a_ref, w_ref, o_ref, acc_ref):
    kk = pl.program_id(2)

    @pl.when(kk == 0)
    def _():
        acc_ref[...] = jnp.zeros_like(acc_ref)

    acc_ref[...] += jnp.dot(a_ref[...], w_ref[...].astype(BF16), preferred_element_type=F32)

    @pl.when(kk == pl.num_programs(2) - 1)
    def _():
        o_ref[...] = acc_ref[...].astype(o_ref.dtype)


def matmul_kgrid(a, w, out_dtype, tm, tn, tk, name="matmul_kgrid"):
    m, k = a.shape
    n = w.shape[1]
    return pl.pallas_call(
        _mm_kgrid_kernel,
        grid=(m // tm, n // tn, k // tk),
        in_specs=[pl.BlockSpec((tm, tk), lambda i, j, kk: (i, kk)),
                  pl.BlockSpec((tk, tn), lambda i, j, kk: (kk, j))],
        out_specs=pl.BlockSpec((tm, tn), lambda i, j, kk: (i, j)),
        out_shape=jax.ShapeDtypeStruct((m, n), out_dtype),
        scratch_shapes=[pltpu.VMEM((tm, tn), F32)],
        compiler_params=_params("parallel", "parallel", "arbitrary"),
        name=name,
    )(a, w)


DIL_STEP = 2 * Q_BLOCK


def _dilated_kernel(q_ref, kp_ref, kc_ref, kn_ref, vp_ref, vc_ref, vn_ref,
                    o_ref, lse_ref, *, class_len):
    i = pl.program_id(2)
    half = Q_BLOCK // 2
    nk = 2 * Q_BLOCK
    row = lax.broadcasted_iota(jnp.int32, (Q_BLOCK, nk), 0)
    col = lax.broadcasted_iota(jnp.int32, (Q_BLOCK, nk), 1)
    lane = lax.broadcasted_iota(jnp.int32, (Q_BLOCK, HEAD_DIM), 1)
    rel = col - half - row
    in_band = (rel >= -half) & (rel <= half)
    for sb in range(DIL_STEP // Q_BLOCK):
        q0 = sb * Q_BLOCK
        kpos = i * DIL_STEP + q0 - half + col
        valid = in_band & (kpos >= 0) & (kpos < class_len)
        lse_tile = jnp.zeros((Q_BLOCK, HEAD_DIM), F32)
        for hh, sl in enumerate(_heads(DIL_GROUP_WIDTH)):
            q = q_ref[0, 0, q0:q0 + Q_BLOCK, sl]
            if sb == 0:
                k = jnp.concatenate([kp_ref[0, 0, :, sl], kc_ref[0, 0, :nk - half, sl]], axis=0)
                v = jnp.concatenate([vp_ref[0, 0, :, sl], vc_ref[0, 0, :nk - half, sl]], axis=0)
            else:
                k = jnp.concatenate([kc_ref[0, 0, half:, sl], kn_ref[0, 0, :, sl]], axis=0)
                v = jnp.concatenate([vc_ref[0, 0, half:, sl], vn_ref[0, 0, :, sl]], axis=0)
            s = jnp.where(valid, _qkt(q, k), MASK_VALUE)
            m = jnp.max(s, axis=-1, keepdims=True)
            p = jnp.exp2(s - m)
            l = jnp.sum(p, axis=-1, keepdims=True)
            o = jnp.dot(p.astype(BF16), v, preferred_element_type=F32) / l
            o_ref[0, 0, q0:q0 + Q_BLOCK, sl] = o.astype(o_ref.dtype)
            lse_tile = jnp.where(lane == hh, m + jnp.log2(l), lse_tile)
        lse_ref[0, 0, q0:q0 + Q_BLOCK, :] = lse_tile


def dilated_group(qkv, cq, ck, cv):
    b, dil, class_len, _ = qkv.shape
    half = Q_BLOCK // 2
    per_step = DIL_STEP // half
    n_half_blocks = class_len // half
    w = DIL_GROUP_WIDTH

    def cur(c):
        return pl.BlockSpec((1, 1, DIL_STEP, w), lambda bb, r, i: (bb, r, i, c))

    def prev(c):
        return pl.BlockSpec((1, 1, half, w),
                            lambda bb, r, i: (bb, r, jnp.maximum(per_step * i - 1, 0), c))

    def nxt(c):
        return pl.BlockSpec((1, 1, half, w),
                            lambda bb, r, i: (bb, r, jnp.minimum(per_step * (i + 1), n_half_blocks - 1), c))

    return pl.pallas_call(
        functools.partial(_dilated_kernel, class_len=class_len),
        grid=(b, dil, class_len // DIL_STEP),
        in_specs=[cur(cq), prev(ck), cur(ck), nxt(ck), prev(cv), cur(cv), nxt(cv)],
        out_specs=[pl.BlockSpec((1, 1, DIL_STEP, w), lambda bb, r, i: (bb, r, i, 0)),
                   pl.BlockSpec((1, 1, DIL_STEP, HEAD_DIM), lambda bb, r, i: (bb, r, i, 0))],
        out_shape=[jax.ShapeDtypeStruct((b, dil, class_len, w), BF16),
                   jax.ShapeDtypeStruct((b, dil, class_len, HEAD_DIM), F32)],
        compiler_params=_params("parallel", "parallel", "arbitrary"),
        name=f"dilated_attention_d{dil}",
    )(qkv, qkv, qkv, qkv, qkv, qkv, qkv)


def _dil_merge_kernel(o0_ref, l0_ref, o1_ref, l1_ref, o2_ref, l2_ref, out_ref,
                      so1, sl1, so2, sl2):
    for src_o, src_l, dst_o, dst_l in ((o1_ref, l1_ref, so1, sl1), (o2_ref, l2_ref, so2, sl2)):
        dil, n = src_o.shape[1], src_o.shape[2]
        for r in range(dil):
            dst_l[pl.ds(r, n, stride=dil), :] = src_l[0, r]
            for hh, sl in enumerate(_heads(DIL_GROUP_WIDTH)):
                dst_o[hh, pl.ds(r, n, stride=dil), :] = src_o[0, r, :, sl].astype(F32)
    a0, a1, a2 = l0_ref[0, 0], sl1[...], sl2[...]
    m = jnp.maximum(jnp.maximum(a0, a1), a2)
    e0, e1, e2 = jnp.exp2(a0 - m), jnp.exp2(a1 - m), jnp.exp2(a2 - m)
    inv = 1.0 / (e0 + e1 + e2)
    w0, w1, w2 = e0 * inv, e1 * inv, e2 * inv
    for hh, sl in enumerate(_heads(DIL_GROUP_WIDTH)):
        h1 = slice(hh, hh + 1)
        out = (w0[:, h1] * o0_ref[0, 0, :, sl].astype(F32) + w1[:, h1] * so1[hh]
               + w2[:, h1] * so2[hh])
        out_ref[0, :, sl] = out.astype(out_ref.dtype)


def dilated_merge(outs, tr=512):
    (o0, l0), (o1, l1), (o2, l2) = outs
    b, _, s_len, w = o0.shape

    def spec(a):
        dil = a.shape[1]
        return pl.BlockSpec((1, dil, tr // dil, a.shape[3]), lambda bb, i: (bb, 0, i, 0))

    slab_o = pltpu.VMEM((w // HEAD_DIM, tr, HEAD_DIM), F32)
    slab_l = pltpu.VMEM((tr, HEAD_DIM), F32)
    return pl.pallas_call(
        _dil_merge_kernel,
        grid=(b, s_len // tr),
        in_specs=[spec(o0), spec(l0), spec(o1), spec(l1), spec(o2), spec(l2)],
        out_specs=pl.BlockSpec((1, tr, w), lambda bb, i: (bb, i, 0)),
        out_shape=jax.ShapeDtypeStruct((b, s_len, w), BF16),
        scratch_shapes=[slab_o, slab_l, slab_o, slab_l],
        compiler_params=_params("parallel", "arbitrary"),
        name="dilated_merge",
    )(o0, l0, o1, l1, o2, l2)


def _diff_kernel(q_ref, k_ref, v_ref, lq1, lk1, lq2, lk2, g_ref, o_ref,
                 s_scr, p_scr, l_scr, *, lam_init):
    lam = (jnp.exp(jnp.sum(lq1[...] * lk1[...], axis=-1, keepdims=True))
           - jnp.exp(jnp.sum(lq2[...] * lk2[...], axis=-1, keepdims=True)) + lam_init)
    th = q_ref.shape[1] // 2
    n_keys = k_ref.shape[1]

    def scores(half, c, buf):
        sl = slice(c * HEAD_DIM, (c + 1) * HEAD_DIM)
        s_scr[buf] = _qkt(q_ref[0, half * th:(half + 1) * th, sl], k_ref[0, :, sl])

    def softmax(buf):
        for r in range(th // BF16_ROWS):
            rows = slice(r * BF16_ROWS, (r + 1) * BF16_ROWS)
            m = jnp.max(s_scr[buf, rows, :], axis=-1, keepdims=True)
            l = jnp.zeros((BF16_ROWS, 1), F32)
            for c0 in range(0, n_keys, SOFTMAX_CHUNK):
                cols = slice(c0, c0 + SOFTMAX_CHUNK)
                p = jnp.exp2(s_scr[buf, rows, cols] - m)
                l = l + jnp.sum(p, axis=-1, keepdims=True)
                p_scr[buf, rows, cols] = p.astype(BF16)
            l_scr[buf, rows, :] = jnp.broadcast_to(l, (BF16_ROWS, HEAD_DIM))

    def values(buf):
        return jnp.dot(p_scr[buf], v_ref[0], preferred_element_type=F32) / l_scr[buf, :, :1]

    def finish(half, o1, o2):
        o = o1 - lam * o2
        o_ref[0, half * th:(half + 1) * th, :] = (
            _rms(o, g_ref[...]) * (1.0 - lam_init)).astype(o_ref.dtype)

    scores(0, 0, 0)
    scores(0, 1, 1)
    softmax(0)
    scores(1, 0, 0)
    softmax(1)
    o_a1 = values(0)
    scores(1, 1, 1)
    softmax(0)
    o_a2 = values(1)
    finish(0, o_a1, o_a2)
    softmax(1)
    o_b1 = values(0)
    o_b2 = values(1)
    finish(1, o_b1, o_b2)


def differential_attention(proj, lq1, lk1, lq2, lk2, subln, lam_init, tq=1024):
    b, s, _ = proj.shape
    vec = pl.BlockSpec((1, HEAD_DIM), lambda bb, h, i: (0, 0))
    cq, ck, cv = P_QB // DIFF_V_DIM, P_KB // DIFF_V_DIM, P_VB // DIFF_V_DIM
    return pl.pallas_call(
        functools.partial(_diff_kernel, lam_init=lam_init),
        grid=(b, DIFF_HEADS, s // tq),
        in_specs=[pl.BlockSpec((1, tq, DIFF_V_DIM), lambda bb, h, i: (bb, i, cq + h)),
                  pl.BlockSpec((1, s, DIFF_V_DIM), lambda bb, h, i: (bb, 0, ck + h)),
                  pl.BlockSpec((1, s, DIFF_V_DIM), lambda bb, h, i: (bb, 0, cv + h)),
                  vec, vec, vec, vec,
                  pl.BlockSpec((1, DIFF_V_DIM), lambda bb, h, i: (0, 0))],
        out_specs=pl.BlockSpec((1, tq, DIFF_V_DIM), lambda bb, h, i: (bb, i, h)),
        out_shape=jax.ShapeDtypeStruct((b, s, DIFF_V_WIDTH), BF16),
        scratch_shapes=[pltpu.VMEM((2, tq // 2, s), F32), pltpu.VMEM((2, tq // 2, s), BF16),
                        pltpu.VMEM((2, tq // 2, HEAD_DIM), F32)],
        compiler_params=_params("parallel", "parallel", "arbitrary"),
        name="differential_attention",
    )(proj, proj, proj, lq1.reshape(1, -1), lk1.reshape(1, -1), lq2.reshape(1, -1),
      lk2.reshape(1, -1), subln.reshape(1, -1))


def _gate_kernel(oa_ref, wa_ref, ob_ref, wb_ref, ga_ref, gb_ref, o_ref):
    ya = jnp.dot(oa_ref[...], wa_ref[...].astype(BF16), preferred_element_type=F32)
    yb = jnp.dot(ob_ref[...], wb_ref[...].astype(BF16), preferred_element_type=F32)
    o = (jax.nn.sigmoid(ga_ref[...].astype(F32)) * ya
         + jax.nn.sigmoid(gb_ref[...].astype(F32)) * yb)
    o_ref[...] = o.astype(o_ref.dtype)


def gated_merge(out_a, w_a, out_b, w_b, proj2d, tm=1024, tn=512):
    t = out_a.shape[0]
    n = w_a.shape[1]
    ca = P_GA // tn
    cb = (P_GA + n) // tn
    n_tiles = n // tn

    def col(i, j):
        return _serpentine(i, j, n_tiles)

    return pl.pallas_call(
        _gate_kernel,
        grid=(t // tm, n_tiles),
        in_specs=[pl.BlockSpec((tm, out_a.shape[1]), lambda i, j: (i, 0)),
                  pl.BlockSpec((w_a.shape[0], tn), lambda i, j: (0, col(i, j))),
                  pl.BlockSpec((tm, out_b.shape[1]), lambda i, j: (i, 0)),
                  pl.BlockSpec((w_b.shape[0], tn), lambda i, j: (0, col(i, j))),
                  pl.BlockSpec((tm, tn), lambda i, j: (i, ca + col(i, j))),
                  pl.BlockSpec((tm, tn), lambda i, j: (i, cb + col(i, j)))],
        out_specs=pl.BlockSpec((tm, tn), lambda i, j: (i, col(i, j))),
        out_shape=jax.ShapeDtypeStruct((t, n), BF16),
        compiler_params=_params("parallel", "arbitrary"),
        name="gated_merge",
    )(out_a, w_a, out_b, w_b, proj2d, proj2d)


def _resnorm_kernel(y_ref, x_ref, gp_ref, gn_ref, xo_ref, ho_ref):
    xn = x_ref[...] + _rms(y_ref[...].astype(F32), gp_ref[...])
    xo_ref[...] = xn
    ho_ref[...] = _rms(xn, gn_ref[...]).astype(ho_ref.dtype)


def _resnorm_last_kernel(y_ref, x_ref, gp_ref, xo_ref):
    xo_ref[...] = x_ref[...] + _rms(y_ref[...].astype(F32), gp_ref[...])


def residual_norm(y, x, g_post, g_next=None, tr=256):
    t, d = x.shape
    row = pl.BlockSpec((tr, d), lambda i: (i, 0))
    vec = pl.BlockSpec((1, d), lambda i: (0, 0))
    if g_next is None:
        return pl.pallas_call(
            _resnorm_last_kernel,
            grid=(t // tr,),
            in_specs=[row, row, vec],
            out_specs=row,
            out_shape=jax.ShapeDtypeStruct((t, d), F32),
            compiler_params=_params("parallel"),
            name="residual_norm_last",
        )(y, x, g_post.reshape(1, d))
    return pl.pallas_call(
        _resnorm_kernel,
        grid=(t // tr,),
        in_specs=[row, row, vec, vec],
        out_specs=[row, row],
        out_shape=[jax.ShapeDtypeStruct((t, d), F32), jax.ShapeDtypeStruct((t, d), BF16)],
        compiler_params=_params("parallel"),
        name="residual_norm",
    )(y, x, g_post.reshape(1, d), g_next.reshape(1, d))


def _mem_block_kernel(y_ref, x_ref, gp1_ref, gpre_ref, wq_ref, kv_ref, wo_ref, gp2_ref, gn_ref,
                      x2_ref, hn_ref, o_scr):
    x1 = x_ref[...] + _rms(y_ref[...].astype(F32), gp1_ref[...])
    h = _rms(x1, gpre_ref[...]).astype(BF16)
    q = (jnp.dot(h, wq_ref[...], preferred_element_type=F32) * Q_PRESCALE).astype(BF16)
    for hh, sl in enumerate(_heads(MEM_WIDTH)):
        slv = slice(MEM_WIDTH + hh * HEAD_DIM, MEM_WIDTH + (hh + 1) * HEAD_DIM)
        s = _qkt(q[:, sl], kv_ref[0, :, sl])
        p = jnp.exp2(s - jnp.max(s, axis=-1, keepdims=True))
        l = jnp.sum(p, axis=-1, keepdims=True)
        o = jnp.dot(p.astype(BF16), kv_ref[0, :, slv], preferred_element_type=F32) / l
        o_scr[:, sl] = o.astype(o_scr.dtype)
    y2 = jnp.dot(o_scr[...], wo_ref[...], preferred_element_type=F32)
    x2 = x1 + _rms(y2, gp2_ref[...])
    x2_ref[...] = x2
    hn_ref[...] = _rms(x2, gn_ref[...]).astype(hn_ref.dtype)


def memory_block(y, x, g_post1, g_pre, w_q, kv, w_o, g_post2, g_next, seq_len, tr=256):
    t, d = x.shape
    tiles_per_seq = seq_len // tr
    row = pl.BlockSpec((tr, d), lambda i: (i, 0))
    vec = pl.BlockSpec((1, d), lambda i: (0, 0))
    once = pl.Buffered(1)
    return pl.pallas_call(
        _mem_block_kernel,
        grid=(t // tr,),
        in_specs=[row, row, vec, vec,
                  pl.BlockSpec(w_q.shape, lambda i: (0, 0), pipeline_mode=once),
                  pl.BlockSpec((1,) + kv.shape[1:], lambda i: (i // tiles_per_seq, 0, 0)),
                  pl.BlockSpec(w_o.shape, lambda i: (0, 0), pipeline_mode=once),
                  vec, vec],
        out_specs=[row, row],
        out_shape=[jax.ShapeDtypeStruct((t, d), F32), jax.ShapeDtypeStruct((t, d), BF16)],
        scratch_shapes=[pltpu.VMEM((tr, MEM_WIDTH), BF16)],
        compiler_params=_params("arbitrary"),
        name="memory_block",
    )(y, x, g_post1.reshape(1, d), g_pre.reshape(1, d), w_q, kv, w_o,
      g_post2.reshape(1, d), g_next.reshape(1, d))


def kernel(x, mem, positions, norm_mix_pre, w_in, w_a, w_b, w_mix_out, norm_mix_post,
           lambda_q1, lambda_k1, lambda_q2, lambda_k2, diff_subln,
           norm_mem_pre, norm_mem_kv, w_mem_q, w_mem_kv, w_mem_o, norm_mem_post,
           norm_mlp_pre, w_mlp_up, w_mlp_down, norm_mlp_post):
    b, s, d = x.shape
    t = b * s
    depth = w_in.shape[0]
    m_len = mem.shape[1]
    xt = x.reshape(t, d)
    memt = mem.reshape(b * m_len, d)
    h, cos, sin = rmsnorm_and_rope_tables(xt, norm_mix_pre[0], positions)
    for layer in range(depth):
        lam_init = 0.8 - 0.6 * float(np.exp(-0.3 * layer))

        w_in_b = w_in[layer]
        proj = in_projection(h, w_in_b, cos, sin)
        proj3 = proj.reshape(b, s, -1)
        tile = DIL_GROUP_WIDTH
        outs = [dilated_group(proj3.reshape(b, 1, s, -1), P_QA // tile, P_KA // tile, P_VA // tile)]
        for g, (_, dil) in enumerate(DIL_CONFIGS):
            if dil > 1:
                qkv = in_projection_dilated(h, w_in_b, cos, sin, g, dil, b)
                outs.append(dilated_group(qkv, 0, 1, 2))
        out_a = dilated_merge(outs).reshape(t, -1)
        out_b = differential_attention(proj3, lambda_q1[layer], lambda_k1[layer],
                                       lambda_q2[layer], lambda_k2[layer],
                                       diff_subln[layer], lam_init).reshape(t, -1)
        merged = gated_merge(out_a, w_a[layer], out_b, w_b[layer], proj)
        y = matmul(merged, w_mix_out[layer], BF16, 1024, 512, name="mix_out")

        mn = rmsnorm(memt, norm_mem_kv[layer])
        kv = matmul(mn, w_mem_kv[layer], BF16, b * m_len, 512, name="mem_kv")
        xt, h = memory_block(y, xt, norm_mix_post[layer], norm_mem_pre[layer],
                             w_mem_q[layer].astype(BF16), kv.reshape(b, m_len, -1),
                             w_mem_o[layer].astype(BF16), norm_mem_post[layer],
                             norm_mlp_pre[layer], s)

        u = matmul(h, w_mlp_up[layer], BF16, 1024, 512, epilogue="relu2", name="mlp_up")
        y = matmul_kgrid(u, w_mlp_down[layer], BF16, 1024, 1024, 2048, name="mlp_down")
        if layer + 1 < depth:
            xt, h = residual_norm(y, xt, norm_mlp_post[layer], norm_mix_pre[layer + 1])
        else:
            xt = residual_norm(y, xt, norm_mlp_post[layer])
    return xt.reshape(b, s, d)
```

```python
import functools

import numpy as np
import jax
import jax.numpy as jnp
from jax import lax
from jax.experimental import pallas as pl
from jax.experimental.pallas import tpu as pltpu

F32 = jnp.float32
BF16 = jnp.bfloat16

HEAD_DIM = 128
BF16_ROWS = 16
SOFTMAX_CHUNK = 1024
DIL_CONFIGS = ((128, 1), (512, 4), (2048, 16))
N_DIL_GROUPS = 3
DIL_HEADS = 8
DIL_GROUP_WIDTH = DIL_HEADS * HEAD_DIM
DIL_WIDTH = N_DIL_GROUPS * DIL_GROUP_WIDTH
DIFF_HEADS = 8
DIFF_QK_WIDTH = DIFF_HEADS * 2 * HEAD_DIM
DIFF_V_DIM = 2 * HEAD_DIM
DIFF_V_WIDTH = DIFF_HEADS * DIFF_V_DIM
MEM_HEADS = 4
MEM_WIDTH = MEM_HEADS * HEAD_DIM
ROPE_THETA = 10000.0
Q_BLOCK = 128
NORM_EPS = 1e-6
MASK_VALUE = -1e30
LOG2E = 1.4426950408889634
Q_PRESCALE = HEAD_DIM ** -0.5 * LOG2E

W_QA = 0
W_KA = W_QA + DIL_WIDTH
W_VA = W_KA + DIL_WIDTH
W_QB = W_VA + DIL_WIDTH

PROJ_TN = 512
P_QA = 0
P_KA = P_QA + DIL_GROUP_WIDTH
P_VA = P_KA + DIL_GROUP_WIDTH
P_QB = P_VA + DIL_GROUP_WIDTH
P_KB = P_QB + DIFF_QK_WIDTH
P_VB = P_KB + DIFF_QK_WIDTH
P_GA = P_VB + DIFF_V_WIDTH

V7X_VMEM_BYTES = 64 * 1024 * 1024
VMEM_LIMIT = V7X_VMEM_BYTES - 6 * 1024 * 1024
MAX_DOUBLE_BUFFERED_ROW_TILE = V7X_VMEM_BYTES // 8


def _params(*sem):
    return pltpu.CompilerParams(dimension_semantics=sem, vmem_limit_bytes=VMEM_LIMIT)


def _rms(x, g):
    return x * lax.rsqrt(jnp.mean(x * x, axis=-1, keepdims=True) + NORM_EPS) * g


def _heads(width):
    return [slice(hh * HEAD_DIM, (hh + 1) * HEAD_DIM) for hh in range(width // HEAD_DIM)]


def _qkt(q, k):
    return lax.dot_general(q, k, (((1,), (1,)), ((), ())), preferred_element_type=F32)


def _row_operand_spec(tm, k, dtype):
    nbytes = tm * k * jnp.dtype(dtype).itemsize
    mode = pl.Buffered(1) if nbytes > MAX_DOUBLE_BUFFERED_ROW_TILE else None
    return pl.BlockSpec((tm, k), lambda i, j: (i, 0), pipeline_mode=mode)


def _serpentine(i, j, n):
    return jnp.where(i % 2 == 0, j, n - 1 - j)


def _rmsnorm_kernel(x_ref, g_ref, o_ref):
    o_ref[...] = _rms(x_ref[...], g_ref[...]).astype(o_ref.dtype)


def rmsnorm(x, g, tr=256):
    t, d = x.shape
    return pl.pallas_call(
        _rmsnorm_kernel,
        grid=(t // tr,),
        in_specs=[pl.BlockSpec((tr, d), lambda i: (i, 0)),
                  pl.BlockSpec((1, d), lambda i: (0, 0))],
        out_specs=pl.BlockSpec((tr, d), lambda i: (i, 0)),
        out_shape=jax.ShapeDtypeStruct((t, d), BF16),
        compiler_params=_params("parallel"),
        name="rmsnorm",
    )(x, g.reshape(1, d))


def _rmsnorm_rope_kernel(x_ref, g_ref, pos_ref, inv_ref, sign_ref, o_ref, cos_ref, sin_ref):
    o_ref[...] = _rms(x_ref[...], g_ref[...]).astype(o_ref.dtype)
    ang = pos_ref[...] * inv_ref[...]
    cos_ref[...] = jnp.cos(ang)
    sin_ref[...] = jnp.sin(ang) * sign_ref[...]


def rmsnorm_and_rope_tables(x, g, positions, tr=256):
    t, d = x.shape
    half = HEAD_DIM // 2
    inv = ROPE_THETA ** (-jnp.arange(0, HEAD_DIM, 2, dtype=F32) / HEAD_DIM)
    inv2 = jnp.concatenate([inv, inv]).reshape(1, HEAD_DIM)
    sign = jnp.concatenate([-jnp.ones((half,), F32), jnp.ones((half,), F32)]).reshape(1, HEAD_DIM)
    pos = positions.astype(F32).reshape(t, 1)
    vec = pl.BlockSpec((1, HEAD_DIM), lambda i: (0, 0))
    table = pl.BlockSpec((tr, HEAD_DIM), lambda i: (i, 0))
    return pl.pallas_call(
        _rmsnorm_rope_kernel,
        grid=(t // tr,),
        in_specs=[pl.BlockSpec((tr, d), lambda i: (i, 0)),
                  pl.BlockSpec((1, d), lambda i: (0, 0)),
                  pl.BlockSpec((tr, 1), lambda i: (i, 0)), vec, vec],
        out_specs=[pl.BlockSpec((tr, d), lambda i: (i, 0)), table, table],
        out_shape=[jax.ShapeDtypeStruct((t, d), BF16),
                   jax.ShapeDtypeStruct((t, HEAD_DIM), F32),
                   jax.ShapeDtypeStruct((t, HEAD_DIM), F32)],
        compiler_params=_params("parallel"),
        name="rmsnorm_rope_tables",
    )(x, g.reshape(1, d), pos, inv2, sign)


def _rope(xh, c, s):
    return xh * c + pltpu.roll(xh, HEAD_DIM // 2, 1) * s


def _inproj_kernel(h_ref, w_ref, cos_ref, sin_ref, o_ref):
    j = _serpentine(pl.program_id(0), pl.program_id(1), pl.num_programs(1))
    acc = jnp.dot(h_ref[...], w_ref[...].astype(BF16), preferred_element_type=F32)
    is_q = (j < P_KA // PROJ_TN) | ((j >= P_QB // PROJ_TN) & (j < P_KB // PROJ_TN))
    is_k = (((j >= P_KA // PROJ_TN) & (j < P_VA // PROJ_TN))
            | ((j >= P_KB // PROJ_TN) & (j < P_VB // PROJ_TN)))

    @pl.when(is_q | is_k)
    def _():
        f = jnp.where(is_q, Q_PRESCALE, 1.0).astype(F32)
        c = cos_ref[...] * f
        s = sin_ref[...] * f
        for sl in _heads(PROJ_TN):
            o_ref[:, sl] = _rope(acc[:, sl], c, s).astype(o_ref.dtype)

    @pl.when(jnp.logical_not(is_q | is_k))
    def _():
        o_ref[...] = acc.astype(o_ref.dtype)


def in_projection(h, w, cos, sin, tm=2048):
    t, k = h.shape
    tn = PROJ_TN
    n_out = w.shape[1] - 3 * (DIL_WIDTH - DIL_GROUP_WIDTH)
    per = DIL_GROUP_WIDTH // tn
    n_group0 = 3 * per
    stride_a = DIL_WIDTH // tn
    skip = W_QB // tn - n_group0

    n_tiles = n_out // tn

    def w_map(i, j):
        jc = _serpentine(i, j, n_tiles)
        return (0, jnp.where(jc < n_group0, (jc // per) * stride_a + jc % per, jc + skip))

    return pl.pallas_call(
        _inproj_kernel,
        grid=(t // tm, n_tiles),
        in_specs=[_row_operand_spec(tm, k, h.dtype),
                  pl.BlockSpec((k, tn), w_map),
                  pl.BlockSpec((tm, HEAD_DIM), lambda i, j: (i, 0)),
                  pl.BlockSpec((tm, HEAD_DIM), lambda i, j: (i, 0))],
        out_specs=pl.BlockSpec((tm, tn), lambda i, j: (i, _serpentine(i, j, n_tiles))),
        out_shape=jax.ShapeDtypeStruct((t, n_out), BF16),
        compiler_params=_params("parallel", "arbitrary"),
        name="in_projection",
    )(h, w, cos, sin)


def _inproj_dilated_kernel(h_ref, w_ref, cos_ref, sin_ref, o_ref, slab_ref, *, dil):
    j = _serpentine(pl.program_id(0), pl.program_id(1), pl.num_programs(1))
    seg = j // (DIL_GROUP_WIDTH // PROJ_TN)
    tm = h_ref.shape[0]
    acc = jnp.dot(h_ref[...], w_ref[...].astype(BF16), preferred_element_type=F32)
    f = jnp.where(seg == 0, Q_PRESCALE, 1.0).astype(F32)
    c = jnp.where(seg < 2, cos_ref[...] * f, 1.0)
    s = jnp.where(seg < 2, sin_ref[...] * f, 0.0)
    for hh, sl in enumerate(_heads(PROJ_TN)):
        slab_ref[hh] = _rope(acc[:, sl], c, s)
        for r in range(dil):
            o_ref[0, r, :, sl] = slab_ref[hh, pl.ds(r, tm // dil, stride=dil), :].astype(o_ref.dtype)


def in_projection_dilated(h, w, cos, sin, g, dil, batch, tm=1024):
    t, k = h.shape
    tn = PROJ_TN
    s_len = t // batch
    tiles_per_seq = s_len // tm
    per = DIL_GROUP_WIDTH // tn
    stride_a = DIL_WIDTH // tn
    n_tiles = 3 * per

    def w_map(i, j):
        jc = _serpentine(i, j, n_tiles)
        return (0, (jc // per) * stride_a + g * per + jc % per)

    return pl.pallas_call(
        functools.partial(_inproj_dilated_kernel, dil=dil),
        grid=(t // tm, n_tiles),
        in_specs=[pl.BlockSpec((tm, k), lambda i, j: (i, 0)),
                  pl.BlockSpec((k, tn), w_map),
                  pl.BlockSpec((tm, HEAD_DIM), lambda i, j: (i, 0)),
                  pl.BlockSpec((tm, HEAD_DIM), lambda i, j: (i, 0))],
        out_specs=pl.BlockSpec((1, dil, tm // dil, tn),
                               lambda i, j: (i // tiles_per_seq, 0, i % tiles_per_seq,
                                             _serpentine(i, j, n_tiles))),
        out_shape=jax.ShapeDtypeStruct((batch, dil, s_len // dil, 3 * DIL_GROUP_WIDTH), BF16),
        scratch_shapes=[pltpu.VMEM((tn // HEAD_DIM, tm, HEAD_DIM), F32)],
        compiler_params=_params("parallel", "arbitrary"),
        name=f"in_projection_d{dil}",
    )(h, w, cos, sin)


def _mm_kernel(a_ref, w_ref, o_ref, *, epilogue):
    acc = jnp.dot(a_ref[...], w_ref[...].astype(BF16), preferred_element_type=F32)
    if epilogue == "relu2":
        acc = jnp.square(jnp.maximum(acc, 0.0))
    o_ref[...] = acc.astype(o_ref.dtype)


def matmul(a, w, out_dtype, tm, tn, epilogue=None, name="matmul"):
    m, k = a.shape
    n = w.shape[1]
    n_tiles = n // tn
    return pl.pallas_call(
        functools.partial(_mm_kernel, epilogue=epilogue),
        grid=(m // tm, n_tiles),
        in_specs=[_row_operand_spec(tm, k, a.dtype),
                  pl.BlockSpec((k, tn), lambda i, j: (0, _serpentine(i, j, n_tiles)))],
        out_specs=pl.BlockSpec((tm, tn), lambda i, j: (i, _serpentine(i, j, n_tiles))),
        out_shape=jax.ShapeDtypeStruct((m, n), out_dtype),
        compiler_params=_params("parallel", "arbitrary"),
        name=name,
    )(a, w)


def _mm_kgrid_kernel(a_ref, w_ref, o_ref, acc_ref):
    kk = pl.program_id(2)

    @pl.when(kk == 0)
    def _():
        acc_ref[...] = jnp.zeros_like(acc_ref)

    acc_ref[...] += jnp.dot(a_ref[...], w_ref[...].astype(BF16), preferred_element_type=F32)

    @pl.when(kk == pl.num_programs(2) - 1)
    def _():
        o_ref[...] = acc_ref[...].astype(o_ref.dtype)


def matmul_kgrid(a, w, out_dtype, tm, tn, tk, name="matmul_kgrid"):
    m, k = a.shape
    n = w.shape[1]
    return pl.pallas_call(
        _mm_kgrid_kernel,
        grid=(m // tm, n // tn, k // tk),
        in_specs=[pl.BlockSpec((tm, tk), lambda i, j, kk: (i, kk)),
                  pl.BlockSpec((tk, tn), lambda i, j, kk: (kk, j))],
        out_specs=pl.BlockSpec((tm, tn), lambda i, j, kk: (i, j)),
        out_shape=jax.ShapeDtypeStruct((m, n), out_dtype),
        scratch_shapes=[pltpu.VMEM((tm, tn), F32)],
        compiler_params=_params("parallel", "parallel", "arbitrary"),
        name=name,
    )(a, w)


DIL_STEP = 2 * Q_BLOCK


def _dilated_kernel(q_ref, kp_ref, kc_ref, kn_ref, vp_ref, vc_ref, vn_ref,
                    o_ref, lse_ref, *, class_len):
    i = pl.program_id(2)
    half = Q_BLOCK // 2
    nk = 2 * Q_BLOCK
    row = lax.broadcasted_iota(jnp.int32, (Q_BLOCK, nk), 0)
    col = lax.broadcasted_iota(jnp.int32, (Q_BLOCK, nk), 1)
    lane = lax.broadcasted_iota(jnp.int32, (Q_BLOCK, HEAD_DIM), 1)
    rel = col - half - row
    in_band = (rel >= -half) & (rel <= half)
    for sb in range(DIL_STEP // Q_BLOCK):
        q0 = sb * Q_BLOCK
        kpos = i * DIL_STEP + q0 - half + col
        valid = in_band & (kpos >= 0) & (kpos < class_len)
        lse_tile = jnp.zeros((Q_BLOCK, HEAD_DIM), F32)
        for hh, sl in enumerate(_heads(DIL_GROUP_WIDTH)):
            q = q_ref[0, 0, q0:q0 + Q_BLOCK, sl]
            if sb == 0:
                k = jnp.concatenate([kp_ref[0, 0, :, sl], kc_ref[0, 0, :nk - half, sl]], axis=0)
                v = jnp.concatenate([vp_ref[0, 0, :, sl], vc_ref[0, 0, :nk - half, sl]], axis=0)
            else:
                k = jnp.concatenate([kc_ref[0, 0, half:, sl], kn_ref[0, 0, :, sl]], axis=0)
                v = jnp.concatenate([vc_ref[0, 0, half:, sl], vn_ref[0, 0, :, sl]], axis=0)
            s = jnp.where(valid, _qkt(q, k), MASK_VALUE)
            m = jnp.max(s, axis=-1, keepdims=True)
            p = jnp.exp2(s - m)
            l = jnp.sum(p, axis=-1, keepdims=True)
            o = jnp.dot(p.astype(BF16), v, preferred_element_type=F32) / l
            o_ref[0, 0, q0:q0 + Q_BLOCK, sl] = o.astype(o_ref.dtype)
            lse_tile = jnp.where(lane == hh, m + jnp.log2(l), lse_tile)
        lse_ref[0, 0, q0:q0 + Q_BLOCK, :] = lse_tile


def dilated_group(qkv, cq, ck, cv):
    b, dil, class_len, _ = qkv.shape
    half = Q_BLOCK // 2
    per_step = DIL_STEP // half
    n_half_blocks = class_len // half
    w = DIL_GROUP_WIDTH

    def cur(c):
        return pl.BlockSpec((1, 1, DIL_STEP, w), lambda bb, r, i: (bb, r, i, c))

    def prev(c):
        return pl.BlockSpec((1, 1, half, w),
                            lambda bb, r, i: (bb, r, jnp.maximum(per_step * i - 1, 0), c))

    def nxt(c):
        return pl.BlockSpec((1, 1, half, w),
                            lambda bb, r, i: (bb, r, jnp.minimum(per_step * (i + 1), n_half_blocks - 1), c))

    return pl.pallas_call(
        functools.partial(_dilated_kernel, class_len=class_len),
        grid=(b, dil, class_len // DIL_STEP),
        in_specs=[cur(cq), prev(ck), cur(ck), nxt(ck), prev(cv), cur(cv), nxt(cv)],
        out_specs=[pl.BlockSpec((1, 1, DIL_STEP, w), lambda bb, r, i: (bb, r, i, 0)),
                   pl.BlockSpec((1, 1, DIL_STEP, HEAD_DIM), lambda bb, r, i: (bb, r, i, 0))],
        out_shape=[jax.ShapeDtypeStruct((b, dil, class_len, w), BF16),
                   jax.ShapeDtypeStruct((b, dil, class_len, HEAD_DIM), F32)],
        compiler_params=_params("parallel", "parallel", "arbitrary"),
        name=f"dilated_attention_d{dil}",
    )(qkv, qkv, qkv, qkv, qkv, qkv, qkv)


def _dil_merge_kernel(o0_ref, l0_ref, o1_ref, l1_ref, o2_ref, l2_ref, out_ref,
                      so1, sl1, so2, sl2):
    for src_o, src_l, dst_o, dst_l in ((o1_ref, l1_ref, so1, sl1), (o2_ref, l2_ref, so2, sl2)):
        dil, n = src_o.shape[1], src_o.shape[2]
        for r in range(dil):
            dst_l[pl.ds(r, n, stride=dil), :] = src_l[0, r]
            for hh, sl in enumerate(_heads(DIL_GROUP_WIDTH)):
                dst_o[hh, pl.ds(r, n, stride=dil), :] = src_o[0, r, :, sl].astype(F32)
    a0, a1, a2 = l0_ref[0, 0], sl1[...], sl2[...]
    m = jnp.maximum(jnp.maximum(a0, a1), a2)
    e0, e1, e2 = jnp.exp2(a0 - m), jnp.exp2(a1 - m), jnp.exp2(a2 - m)
    inv = 1.0 / (e0 + e1 + e2)
    w0, w1, w2 = e0 * inv, e1 * inv, e2 * inv
    for hh, sl in enumerate(_heads(DIL_GROUP_WIDTH)):
        h1 = slice(hh, hh + 1)
        out = (w0[:, h1] * o0_ref[0, 0, :, sl].astype(F32) + w1[:, h1] * so1[hh]
               + w2[:, h1] * so2[hh])
        out_ref[0, :, sl] = out.astype(out_ref.dtype)


def dilated_merge(outs, tr=512):
    (o0, l0), (o1, l1), (o2, l2) = outs
    b, _, s_len, w = o0.shape

    def spec(a):
        dil = a.shape[1]
        return pl.BlockSpec((1, dil, tr // dil, a.shape[3]), lambda bb, i: (bb, 0, i, 0))

    slab_o = pltpu.VMEM((w // HEAD_DIM, tr, HEAD_DIM), F32)
    slab_l = pltpu.VMEM((tr, HEAD_DIM), F32)
    return pl.pallas_call(
        _dil_merge_kernel,
        grid=(b, s_len // tr),
        in_specs=[spec(o0), spec(l0), spec(o1), spec(l1), spec(o2), spec(l2)],
        out_specs=pl.BlockSpec((1, tr, w), lambda bb, i: (bb, i, 0)),
        out_shape=jax.ShapeDtypeStruct((b, s_len, w), BF16),
        scratch_shapes=[slab_o, slab_l, slab_o, slab_l],
        compiler_params=_params("parallel", "arbitrary"),
        name="dilated_merge",
    )(o0, l0, o1, l1, o2, l2)


def _diff_kernel(q_ref, k_ref, v_ref, lq1, lk1, lq2, lk2, g_ref, o_ref,
                 s_scr, p_scr, l_scr, *, lam_init):
    lam = (jnp.exp(jnp.sum(lq1[...] * lk1[...], axis=-1, keepdims=True))
           - jnp.exp(jnp.sum(lq2[...] * lk2[...], axis=-1, keepdims=True)) + lam_init)
    th = q_ref.shape[1] // 2
    n_keys = k_ref.shape[1]

    def scores(half, c, buf):
        sl = slice(c * HEAD_DIM, (c + 1) * HEAD_DIM)
        s_scr[buf] = _qkt(q_ref[0, half * th:(half + 1) * th, sl], k_ref[0, :, sl])

    def softmax(buf):
        for r in range(th // BF16_ROWS):
            rows = slice(r * BF16_ROWS, (r + 1) * BF16_ROWS)
            m = jnp.max(s_scr[buf, rows, :], axis=-1, keepdims=True)
            l = jnp.zeros((BF16_ROWS, 1), F32)
            for c0 in range(0, n_keys, SOFTMAX_CHUNK):
                cols = slice(c0, c0 + SOFTMAX_CHUNK)
                p = jnp.exp2(s_scr[buf, rows, cols] - m)
                l = l + jnp.sum(p, axis=-1, keepdims=True)
                p_scr[buf, rows, cols] = p.astype(BF16)
            l_scr[buf, rows, :] = jnp.broadcast_to(l, (BF16_ROWS, HEAD_DIM))

    def values(buf):
        return jnp.dot(p_scr[buf], v_ref[0], preferred_element_type=F32) / l_scr[buf, :, :1]

    def finish(half, o1, o2):
        o = o1 - lam * o2
        o_ref[0, half * th:(half + 1) * th, :] = (
            _rms(o, g_ref[...]) * (1.0 - lam_init)).astype(o_ref.dtype)

    scores(0, 0, 0)
    scores(0, 1, 1)
    softmax(0)
    scores(1, 0, 0)
    softmax(1)
    o_a1 = values(0)
    scores(1, 1, 1)
    softmax(0)
    o_a2 = values(1)
    finish(0, o_a1, o_a2)
    softmax(1)
    o_b1 = values(0)
    o_b2 = values(1)
    finish(1, o_b1, o_b2)


def differential_attention(proj, lq1, lk1, lq2, lk2, subln, lam_init, tq=1024):
    b, s, _ = proj.shape
    vec = pl.BlockSpec((1, HEAD_DIM), lambda bb, h, i: (0, 0))
    cq, ck, cv = P_QB // DIFF_V_DIM, P_KB // DIFF_V_DIM, P_VB // DIFF_V_DIM
    return pl.pallas_call(
        functools.partial(_diff_kernel, lam_init=lam_init),
        grid=(b, DIFF_HEADS, s // tq),
        in_specs=[pl.BlockSpec((1, tq, DIFF_V_DIM), lambda bb, h, i: (bb, i, cq + h)),
                  pl.BlockSpec((1, s, DIFF_V_DIM), lambda bb, h, i: (bb, 0, ck + h)),
                  pl.BlockSpec((1, s, DIFF_V_DIM), lambda bb, h, i: (bb, 0, cv + h)),
                  vec, vec, vec, vec,
                  pl.BlockSpec((1, DIFF_V_DIM), lambda bb, h, i: (0, 0))],
        out_specs=pl.BlockSpec((1, tq, DIFF_V_DIM), lambda bb, h, i: (bb, i, h)),
        out_shape=jax.ShapeDtypeStruct((b, s, DIFF_V_WIDTH), BF16),
        scratch_shapes=[pltpu.VMEM((2, tq // 2, s), F32), pltpu.VMEM((2, tq // 2, s), BF16),
                        pltpu.VMEM((2, tq // 2, HEAD_DIM), F32)],
        compiler_params=_params("parallel", "parallel", "arbitrary"),
        name="differential_attention",
    )(proj, proj, proj, lq1.reshape(1, -1), lk1.reshape(1, -1), lq2.reshape(1, -1),
      lk2.reshape(1, -1), subln.reshape(1, -1))


def _gate_kernel(oa_ref, wa_ref, ob_ref, wb_ref, ga_ref, gb_ref, o_ref):
    ya = jnp.dot(oa_ref[...], wa_ref[...].astype(BF16), preferred_element_type=F32)
    yb = jnp.dot(ob_ref[...], wb_ref[...].astype(BF16), preferred_element_type=F32)
    o = (jax.nn.sigmoid(ga_ref[...].astype(F32)) * ya
         + jax.nn.sigmoid(gb_ref[...].astype(F32)) * yb)
    o_ref[...] = o.astype(o_ref.dtype)


def gated_merge(out_a, w_a, out_b, w_b, proj2d, tm=1024, tn=512):
    t = out_a.shape[0]
    n = w_a.shape[1]
    ca = P_GA // tn
    cb = (P_GA + n) // tn
    n_tiles = n // tn

    def col(i, j):
        return _serpentine(i, j, n_tiles)

    return pl.pallas_call(
        _gate_kernel,
        grid=(t // tm, n_tiles),
        in_specs=[pl.BlockSpec((tm, out_a.shape[1]), lambda i, j: (i, 0)),
                  pl.BlockSpec((w_a.shape[0], tn), lambda i, j: (0, col(i, j))),
                  pl.BlockSpec((tm, out_b.shape[1]), lambda i, j: (i, 0)),
                  pl.BlockSpec((w_b.shape[0], tn), lambda i, j: (0, col(i, j))),
                  pl.BlockSpec((tm, tn), lambda i, j: (i, ca + col(i, j))),
                  pl.BlockSpec((tm, tn), lambda i, j: (i, cb + col(i, j)))],
        out_specs=pl.BlockSpec((tm, tn), lambda i, j: (i, col(i, j))),
        out_shape=jax.ShapeDtypeStruct((t, n), BF16),
        compiler_params=_params("parallel", "arbitrary"),
        name="gated_merge",
    )(out_a, w_a, out_b, w_b, proj2d, proj2d)


def _resnorm_kernel(y_ref, x_ref, gp_ref, gn_ref, xo_ref, ho_ref):
    xn = x_ref[...] + _rms(y_ref[...].astype(F32), gp_ref[...])
    xo_ref[...] = xn
    ho_ref[...] = _rms(xn, gn_ref[...]).astype(ho_ref.dtype)


def _resnorm_last_kernel(y_ref, x_ref, gp_ref, xo_ref):
    xo_ref[...] = x_ref[...] + _rms(y_ref[...].astype(F32), gp_ref[...])


def residual_norm(y, x, g_post, g_next=None, tr=256):
    t, d = x.shape
    row = pl.BlockSpec((tr, d), lambda i: (i, 0))
    vec = pl.BlockSpec((1, d), lambda i: (0, 0))
    if g_next is None:
        return pl.pallas_call(
            _resnorm_last_kernel,
            grid=(t // tr,),
            in_specs=[row, row, vec],
            out_specs=row,
            out_shape=jax.ShapeDtypeStruct((t, d), F32),
            compiler_params=_params("parallel"),
            name="residual_norm_last",
        )(y, x, g_post.reshape(1, d))
    return pl.pallas_call(
        _resnorm_kernel,
        grid=(t // tr,),
        in_specs=[row, row, vec, vec],
        out_specs=[row, row],
        out_shape=[jax.ShapeDtypeStruct((t, d), F32), jax.ShapeDtypeStruct((t, d), BF16)],
        compiler_params=_params("parallel"),
        name="residual_norm",
    )(y, x, g_post.reshape(1, d), g_next.reshape(1, d))


def _mem_block_kernel(y_ref, x_ref, gp1_ref, gpre_ref, wq_ref, kv_ref, wo_ref, gp2_ref, gn_ref,
                      x2_ref, hn_ref, o_scr):
    x1 = x_ref[...] + _rms(y_ref[...].astype(F32), gp1_ref[...])
    h = _rms(x1, gpre_ref[...]).astype(BF16)
    q = (jnp.dot(h, wq_ref[...], preferred_element_type=F32) * Q_PRESCALE).astype(BF16)
    for hh, sl in enumerate(_heads(MEM_WIDTH)):
        slv = slice(MEM_WIDTH + hh * HEAD_DIM, MEM_WIDTH + (hh + 1) * HEAD_DIM)
        s = _qkt(q[:, sl], kv_ref[0, :, sl])
        p = jnp.exp2(s - jnp.max(s, axis=-1, keepdims=True))
        l = jnp.sum(p, axis=-1, keepdims=True)
        o = jnp.dot(p.astype(BF16), kv_ref[0, :, slv], preferred_element_type=F32) / l
        o_scr[:, sl] = o.astype(o_scr.dtype)
    y2 = jnp.dot(o_scr[...], wo_ref[...], preferred_element_type=F32)
    x2 = x1 + _rms(y2, gp2_ref[...])
    x2_ref[...] = x2
    hn_ref[...] = _rms(x2, gn_ref[...]).astype(hn_ref.dtype)


def memory_block(y, x, g_post1, g_pre, w_q, kv, w_o, g_post2, g_next, seq_len, tr=256):
    t, d = x.shape
    tiles_per_seq = seq_len // tr
    row = pl.BlockSpec((tr, d), lambda i: (i, 0))
    vec = pl.BlockSpec((1, d), lambda i: (0, 0))
    once = pl.Buffered(1)
    return pl.pallas_call(
        _mem_block_kernel,
        grid=(t // tr,),
        in_specs=[row, row, vec, vec,
                  pl.BlockSpec(w_q.shape, lambda i: (0, 0), pipeline_mode=once),
                  pl.BlockSpec((1,) + kv.shape[1:], lambda i: (i // tiles_per_seq, 0, 0)),
                  pl.BlockSpec(w_o.shape, lambda i: (0, 0), pipeline_mode=once),
                  vec, vec],
        out_specs=[row, row],
        out_shape=[jax.ShapeDtypeStruct((t, d), F32), jax.ShapeDtypeStruct((t, d), BF16)],
        scratch_shapes=[pltpu.VMEM((tr, MEM_WIDTH), BF16)],
        compiler_params=_params("arbitrary"),
        name="memory_block",
    )(y, x, g_post1.reshape(1, d), g_pre.reshape(1, d), w_q, kv, w_o,
      g_post2.reshape(1, d), g_next.reshape(1, d))


def kernel(x, mem, positions, norm_mix_pre, w_in, w_a, w_b, w_mix_out, norm_mix_post,
           lambda_q1, lambda_k1, lambda_q2, lambda_k2, diff_subln,
           norm_mem_pre, norm_mem_kv, w_mem_q, w_mem_kv, w_mem_o, norm_mem_post,
           norm_mlp_pre, w_mlp_up, w_mlp_down, norm_mlp_post):
    b, s, d = x.shape
    t = b * s
    depth = w_in.shape[0]
    m_len = mem.shape[1]
    xt = x.reshape(t, d)
    memt = mem.reshape(b * m_len, d)
    h, cos, sin = rmsnorm_and_rope_tables(xt, norm_mix_pre[0], positions)
    for layer in range(depth):
        lam_init = 0.8 - 0.6 * float(np.exp(-0.3 * layer))

        w_in_b = w_in[layer]
        proj = in_projection(h, w_in_b, cos, sin)
        proj3 = proj.reshape(b, s, -1)
        tile = DIL_GROUP_WIDTH
        outs = [dilated_group(proj3.reshape(b, 1, s, -1), P_QA // tile, P_KA // tile, P_VA // tile)]
        for g, (_, dil) in enumerate(DIL_CONFIGS):
            if dil > 1:
                qkv = in_projection_dilated(h, w_in_b, cos, sin, g, dil, b)
                outs.append(dilated_group(qkv, 0, 1, 2))
        out_a = dilated_merge(outs).reshape(t, -1)
        out_b = differential_attention(proj3, lambda_q1[layer], lambda_k1[layer],
                                       lambda_q2[layer], lambda_k2[layer],
                                       diff_subln[layer], lam_init).reshape(t, -1)
        merged = gated_merge(out_a, w_a[layer], out_b, w_b[layer], proj)
        y = matmul(merged, w_mix_out[layer], BF16, 1024, 512, name="mix_out")

        mn = rmsnorm(memt, norm_mem_kv[layer])
        kv = matmul(mn, w_mem_kv[layer], BF16, b * m_len, 512, name="mem_kv")
        xt, h = memory_block(y, xt, norm_mix_post[layer], norm_mem_pre[layer],
                             w_mem_q[layer].astype(BF16), kv.reshape(b, m_len, -1),
                             w_mem_o[layer].astype(BF16), norm_mem_post[layer],
                             norm_mlp_pre[layer], s)

        u = matmul(h, w_mlp_up[layer], BF16, 2048, 512, epilogue="relu2", name="mlp_up")
        y = matmul_kgrid(u, w_mlp_down[layer], BF16, 1024, 1024, 2048, name="mlp_down")
        if layer + 1 < depth:
            xt, h = residual_norm(y, xt, norm_mlp_post[layer], norm_mix_pre[layer + 1])
        else:
            xt = residual_norm(y, xt, norm_mlp_post[layer])
    return xt.reshape(b, s, d)
```

```python
import functools

import numpy as np
import jax
import jax.numpy as jnp
from jax import lax
from jax.experimental import pallas as pl
from jax.experimental.pallas import tpu as pltpu

F32 = jnp.float32
BF16 = jnp.bfloat16

HEAD_DIM = 128
BF16_ROWS = 16
SOFTMAX_CHUNK = 512
DIL_CONFIGS = ((128, 1), (512, 4), (2048, 16))
N_DIL_GROUPS = 3
DIL_HEADS = 8
DIL_GROUP_WIDTH = DIL_HEADS * HEAD_DIM
DIL_WIDTH = N_DIL_GROUPS * DIL_GROUP_WIDTH
DIFF_HEADS = 8
DIFF_QK_WIDTH = DIFF_HEADS * 2 * HEAD_DIM
DIFF_V_DIM = 2 * HEAD_DIM
DIFF_V_WIDTH = DIFF_HEADS * DIFF_V_DIM
MEM_HEADS = 4
MEM_WIDTH = MEM_HEADS * HEAD_DIM
ROPE_THETA = 10000.0
Q_BLOCK = 128
MAX_DIL_STEP = 4 * Q_BLOCK
NORM_EPS = 1e-6
MASK_VALUE = -1e30
LOG2E = 1.4426950408889634
Q_PRESCALE = HEAD_DIM ** -0.5 * LOG2E

W_QA = 0
W_KA = W_QA + DIL_WIDTH
W_VA = W_KA + DIL_WIDTH
W_QB = W_VA + DIL_WIDTH

PROJ_TN = 512
P_QA = 0
P_KA = P_QA + DIL_GROUP_WIDTH
P_VA = P_KA + DIL_GROUP_WIDTH
P_QB = P_VA + DIL_GROUP_WIDTH
P_KB = P_QB + DIFF_QK_WIDTH
P_VB = P_KB + DIFF_QK_WIDTH
P_GA = P_VB + DIFF_V_WIDTH

V7X_VMEM_BYTES = 64 * 1024 * 1024
VMEM_LIMIT = V7X_VMEM_BYTES - 6 * 1024 * 1024
MAX_DOUBLE_BUFFERED_ROW_TILE = V7X_VMEM_BYTES // 8


def _params(*sem):
    return pltpu.CompilerParams(dimension_semantics=sem, vmem_limit_bytes=VMEM_LIMIT)


def _rms(x, g):
    return x * lax.rsqrt(jnp.mean(x * x, axis=-1, keepdims=True) + NORM_EPS) * g


def _heads(width):
    return [slice(hh * HEAD_DIM, (hh + 1) * HEAD_DIM) for hh in range(width // HEAD_DIM)]


def _qkt(q, k):
    return lax.dot_general(q, k, (((1,), (1,)), ((), ())), preferred_element_type=F32)


def _row_operand_spec(tm, k, dtype):
    nbytes = tm * k * jnp.dtype(dtype).itemsize
    mode = pl.Buffered(1) if nbytes > MAX_DOUBLE_BUFFERED_ROW_TILE else None
    return pl.BlockSpec((tm, k), lambda i, j: (i, 0), pipeline_mode=mode)


def _serpentine(i, j, n):
    return jnp.where(i % 2 == 0, j, n - 1 - j)


def _rmsnorm_kernel(x_ref, g_ref, o_ref):
    o_ref[...] = _rms(x_ref[...], g_ref[...]).astype(o_ref.dtype)


def rmsnorm(x, g, tr=256):
    t, d = x.shape
    return pl.pallas_call(
        _rmsnorm_kernel,
        grid=(t // tr,),
        in_specs=[pl.BlockSpec((tr, d), lambda i: (i, 0)),
                  pl.BlockSpec((1, d), lambda i: (0, 0))],
        out_specs=pl.BlockSpec((tr, d), lambda i: (i, 0)),
        out_shape=jax.ShapeDtypeStruct((t, d), BF16),
        compiler_params=_params("parallel"),
        name="rmsnorm",
    )(x, g.reshape(1, d))


def _rmsnorm_rope_kernel(x_ref, g_ref, pos_ref, inv_ref, sign_ref, o_ref, cos_ref, sin_ref):
    o_ref[...] = _rms(x_ref[...], g_ref[...]).astype(o_ref.dtype)
    ang = pos_ref[...] * inv_ref[...]
    cos_ref[...] = jnp.cos(ang)
    sin_ref[...] = jnp.sin(ang) * sign_ref[...]


def rmsnorm_and_rope_tables(x, g, positions, tr=256):
    t, d = x.shape
    half = HEAD_DIM // 2
    inv = ROPE_THETA ** (-jnp.arange(0, HEAD_DIM, 2, dtype=F32) / HEAD_DIM)
    inv2 = jnp.concatenate([inv, inv]).reshape(1, HEAD_DIM)
    sign = jnp.concatenate([-jnp.ones((half,), F32), jnp.ones((half,), F32)]).reshape(1, HEAD_DIM)
    pos = positions.astype(F32).reshape(t, 1)
    vec = pl.BlockSpec((1, HEAD_DIM), lambda i: (0, 0))
    table = pl.BlockSpec((tr, HEAD_DIM), lambda i: (i, 0))
    return pl.pallas_call(
        _rmsnorm_rope_kernel,
        grid=(t // tr,),
        in_specs=[pl.BlockSpec((tr, d), lambda i: (i, 0)),
                  pl.BlockSpec((1, d), lambda i: (0, 0)),
                  pl.BlockSpec((tr, 1), lambda i: (i, 0)), vec, vec],
        out_specs=[pl.BlockSpec((tr, d), lambda i: (i, 0)), table, table],
        out_shape=[jax.ShapeDtypeStruct((t, d), BF16),
                   jax.ShapeDtypeStruct((t, HEAD_DIM), F32),
                   jax.ShapeDtypeStruct((t, HEAD_DIM), F32)],
        compiler_params=_params("parallel"),
        name="rmsnorm_rope_tables",
    )(x, g.reshape(1, d), pos, inv2, sign)


def _rope(xh, c, s):
    return xh * c + pltpu.roll(xh, HEAD_DIM // 2, 1) * s


def _inproj_kernel(h_ref, w_ref, cos_ref, sin_ref, o_ref):
    j = _serpentine(pl.program_id(0), pl.program_id(1), pl.num_programs(1))
    acc = jnp.dot(h_ref[...], w_ref[...].astype(BF16), preferred_element_type=F32)
    is_q = (j < P_KA // PROJ_TN) | ((j >= P_QB // PROJ_TN) & (j < P_KB // PROJ_TN))
    is_k = (((j >= P_KA // PROJ_TN) & (j < P_VA // PROJ_TN))
            | ((j >= P_KB // PROJ_TN) & (j < P_VB // PROJ_TN)))

    @pl.when(is_q | is_k)
    def _():
        f = jnp.where(is_q, Q_PRESCALE, 1.0).astype(F32)
        c = cos_ref[...] * f
        s = sin_ref[...] * f
        for sl in _heads(PROJ_TN):
            o_ref[:, sl] = _rope(acc[:, sl], c, s).astype(o_ref.dtype)

    @pl.when(jnp.logical_not(is_q | is_k))
    def _():
        o_ref[...] = acc.astype(o_ref.dtype)


def in_projection(h, w, cos, sin, tm=2048):
    t, k = h.shape
    tn = PROJ_TN
    n_out = w.shape[1] - 3 * (DIL_WIDTH - DIL_GROUP_WIDTH)
    per = DIL_GROUP_WIDTH // tn
    n_group0 = 3 * per
    stride_a = DIL_WIDTH // tn
    skip = W_QB // tn - n_group0

    n_tiles = n_out // tn

    def w_map(i, j):
        jc = _serpentine(i, j, n_tiles)
        return (0, jnp.where(jc < n_group0, (jc // per) * stride_a + jc % per, jc + skip))

    return pl.pallas_call(
        _inproj_kernel,
        grid=(t // tm, n_tiles),
        in_specs=[_row_operand_spec(tm, k, h.dtype),
                  pl.BlockSpec((k, tn), w_map),
                  pl.BlockSpec((tm, HEAD_DIM), lambda i, j: (i, 0)),
                  pl.BlockSpec((tm, HEAD_DIM), lambda i, j: (i, 0))],
        out_specs=pl.BlockSpec((tm, tn), lambda i, j: (i, _serpentine(i, j, n_tiles))),
        out_shape=jax.ShapeDtypeStruct((t, n_out), BF16),
        compiler_params=_params("parallel", "arbitrary"),
        name="in_projection",
    )(h, w, cos, sin)


def _inproj_dilated_kernel(h_ref, w_ref, cos_ref, sin_ref, o_ref, slab_ref, *, dil):
    j = _serpentine(pl.program_id(0), pl.program_id(1), pl.num_programs(1))
    seg = j // (DIL_GROUP_WIDTH // PROJ_TN)
    tm = h_ref.shape[0]
    acc = jnp.dot(h_ref[...], w_ref[...].astype(BF16), preferred_element_type=F32)

    def store_class_major(transform):
        for hh, sl in enumerate(_heads(PROJ_TN)):
            slab_ref[hh] = transform(acc[:, sl])
            for r in range(dil):
                o_ref[0, r, :, sl] = slab_ref[hh, pl.ds(r, tm // dil, stride=dil), :].astype(o_ref.dtype)

    @pl.when(seg < 2)
    def _():
        f = jnp.where(seg == 0, Q_PRESCALE, 1.0).astype(F32)
        c = cos_ref[...] * f
        s = sin_ref[...] * f
        store_class_major(lambda xh: _rope(xh, c, s))

    @pl.when(seg == 2)
    def _():
        store_class_major(lambda xh: xh)


def in_projection_dilated(h, w, cos, sin, g, dil, batch, tm=1024):
    t, k = h.shape
    tn = PROJ_TN
    s_len = t // batch
    tiles_per_seq = s_len // tm
    per = DIL_GROUP_WIDTH // tn
    stride_a = DIL_WIDTH // tn
    n_tiles = 3 * per

    def w_map(i, j):
        jc = _serpentine(i, j, n_tiles)
        return (0, (jc // per) * stride_a + g * per + jc % per)

    return pl.pallas_call(
        functools.partial(_inproj_dilated_kernel, dil=dil),
        grid=(t // tm, n_tiles),
        in_specs=[pl.BlockSpec((tm, k), lambda i, j: (i, 0)),
                  pl.BlockSpec((k, tn), w_map),
                  pl.BlockSpec((tm, HEAD_DIM), lambda i, j: (i, 0)),
                  pl.BlockSpec((tm, HEAD_DIM), lambda i, j: (i, 0))],
        out_specs=pl.BlockSpec((1, dil, tm // dil, tn),
                               lambda i, j: (i // tiles_per_seq, 0, i % tiles_per_seq,
                                             _serpentine(i, j, n_tiles))),
        out_shape=jax.ShapeDtypeStruct((batch, dil, s_len // dil, 3 * DIL_GROUP_WIDTH), BF16),
        scratch_shapes=[pltpu.VMEM((tn // HEAD_DIM, tm, HEAD_DIM), F32)],
        compiler_params=_params("parallel", "arbitrary"),
        name=f"in_projection_d{dil}",
    )(h, w, cos, sin)


def _mm_kernel(a_ref, w_ref, o_ref, *, epilogue):
    acc = jnp.dot(a_ref[...], w_ref[...].astype(BF16), preferred_element_type=F32)
    if epilogue == "relu2":
        acc = jnp.square(jnp.maximum(acc, 0.0))
    o_ref[...] = acc.astype(o_ref.dtype)


def matmul(a, w, out_dtype, tm, tn, epilogue=None, name="matmul"):
    m, k = a.shape
    n = w.shape[1]
    n_tiles = n // tn
    return pl.pallas_call(
        functools.partial(_mm_kernel, epilogue=epilogue),
        grid=(m // tm, n_tiles),
        in_specs=[_row_operand_spec(tm, k, a.dtype),
                  pl.BlockSpec((k, tn), lambda i, j: (0, _serpentine(i, j, n_tiles)))],
        out_specs=pl.BlockSpec((tm, tn), lambda i, j: (i, _serpentine(i, j, n_tiles))),
        out_shape=jax.ShapeDtypeStruct((m, n), out_dtype),
        compiler_params=_params("parallel", "arbitrary"),
        name=name,
    )(a, w)


def _mm_kgrid_kernel(a_ref, w_ref, o_ref, acc_ref):
    kk = pl.program_id(2)

    @pl.when(kk == 0)
    def _():
        acc_ref[...] = jnp.zeros_like(acc_ref)

    acc_ref[...] += jnp.dot(a_ref[...], w_ref[...].astype(BF16), preferred_element_type=F32)

    @pl.when(kk == pl.num_programs(2) - 1)
    def _():
        o_ref[...] = acc_ref[...].astype(o_ref.dtype)


def matmul_kgrid(a, w, out_dtype, tm, tn, tk, name="matmul_kgrid"):
    m, k = a.shape
    n = w.shape[1]
    return pl.pallas_call(
        _mm_kgrid_kernel,
        grid=(m // tm, n // tn, k // tk),
        in_specs=[pl.BlockSpec((tm, tk), lambda i, j, kk: (i, kk)),
                  pl.BlockSpec((tk, tn), lambda i, j, kk: (kk, j))],
        out_specs=pl.BlockSpec((tm, tn), lambda i, j, kk: (i, j)),
        out_shape=jax.ShapeDtypeStruct((m, n), out_dtype),
        scratch_shapes=[pltpu.VMEM((tm, tn), F32)],
        compiler_params=_params("parallel", "parallel", "arbitrary"),
        name=name,
    )(a, w)


def _dilated_kernel(q_ref, kp_ref, kc_ref, kn_ref, vp_ref, vc_ref, vn_ref,
                    o_ref, lse_ref, *, class_len):
    i = pl.program_id(2)
    step = q_ref.shape[2]
    half = Q_BLOCK // 2
    nk = 2 * Q_BLOCK
    row = lax.broadcasted_iota(jnp.int32, (Q_BLOCK, nk), 0)
    col = lax.broadcasted_iota(jnp.int32, (Q_BLOCK, nk), 1)
    lane = lax.broadcasted_iota(jnp.int32, (Q_BLOCK, HEAD_DIM), 1)
    rel = col - half - row
    in_band = (rel >= -half) & (rel <= half)

    def window(prev_ref, cur_ref, next_ref, q0, sl):
        parts = []
        if q0 == 0:
            parts.append(prev_ref[0, 0, :, sl])
        lo, hi = max(q0 - half, 0), min(q0 + Q_BLOCK + half, step)
        parts.append(cur_ref[0, 0, lo:hi, sl])
        if q0 + Q_BLOCK == step:
            parts.append(next_ref[0, 0, :, sl])
        return jnp.concatenate(parts, axis=0)

    for sb in range(step // Q_BLOCK):
        q0 = sb * Q_BLOCK
        kpos = i * step + q0 - half + col
        valid = in_band & (kpos >= 0) & (kpos < class_len)
        lse_tile = jnp.zeros((Q_BLOCK, HEAD_DIM), F32)
        for hh, sl in enumerate(_heads(DIL_GROUP_WIDTH)):
            q = q_ref[0, 0, q0:q0 + Q_BLOCK, sl]
            k, v = (window(kp_ref, kc_ref, kn_ref, q0, sl), window(vp_ref, vc_ref, vn_ref, q0, sl))
            s = jnp.where(valid, _qkt(q, k), MASK_VALUE)
            m = jnp.max(s, axis=-1, keepdims=True)
            p = jnp.exp2(s - m)
            l = jnp.sum(p, axis=-1, keepdims=True)
            o = jnp.dot(p.astype(BF16), v, preferred_element_type=F32) / l
            o_ref[0, 0, q0:q0 + Q_BLOCK, sl] = o.astype(o_ref.dtype)
            lse_tile = jnp.where(lane == hh, m + jnp.log2(l), lse_tile)
        lse_ref[0, 0, q0:q0 + Q_BLOCK, :] = lse_tile


def dilated_group(qkv, cq, ck, cv):
    b, dil, class_len, _ = qkv.shape
    half = Q_BLOCK // 2
    step = min(class_len, MAX_DIL_STEP)
    per_step = step // half
    n_half_blocks = class_len // half
    w = DIL_GROUP_WIDTH

    def cur(c):
        return pl.BlockSpec((1, 1, step, w), lambda bb, r, i: (bb, r, i, c))

    def prev(c):
        return pl.BlockSpec((1, 1, half, w),
                            lambda bb, r, i: (bb, r, jnp.maximum(per_step * i - 1, 0), c))

    def nxt(c):
        return pl.BlockSpec((1, 1, half, w),
                            lambda bb, r, i: (bb, r, jnp.minimum(per_step * (i + 1), n_half_blocks - 1), c))

    return pl.pallas_call(
        functools.partial(_dilated_kernel, class_len=class_len),
        grid=(b, dil, class_len // step),
        in_specs=[cur(cq), prev(ck), cur(ck), nxt(ck), prev(cv), cur(cv), nxt(cv)],
        out_specs=[pl.BlockSpec((1, 1, step, w), lambda bb, r, i: (bb, r, i, 0)),
                   pl.BlockSpec((1, 1, step, HEAD_DIM), lambda bb, r, i: (bb, r, i, 0))],
        out_shape=[jax.ShapeDtypeStruct((b, dil, class_len, w), BF16),
                   jax.ShapeDtypeStruct((b, dil, class_len, HEAD_DIM), F32)],
        compiler_params=_params("parallel", "parallel", "arbitrary"),
        name=f"dilated_attention_d{dil}",
    )(qkv, qkv, qkv, qkv, qkv, qkv, qkv)


def _dil_merge_kernel(o0_ref, l0_ref, o1_ref, l1_ref, o2_ref, l2_ref, out_ref,
                      so1, sl1, so2, sl2):
    for src_o, src_l, dst_o, dst_l in ((o1_ref, l1_ref, so1, sl1), (o2_ref, l2_ref, so2, sl2)):
        dil, n = src_o.shape[1], src_o.shape[2]
        for r in range(dil):
            dst_l[pl.ds(r, n, stride=dil), :] = src_l[0, r]
            for hh, sl in enumerate(_heads(DIL_GROUP_WIDTH)):
                dst_o[hh, pl.ds(r, n, stride=dil), :] = src_o[0, r, :, sl].astype(F32)
    a0, a1, a2 = l0_ref[0, 0], sl1[...], sl2[...]
    m = jnp.maximum(jnp.maximum(a0, a1), a2)
    e0, e1, e2 = jnp.exp2(a0 - m), jnp.exp2(a1 - m), jnp.exp2(a2 - m)
    inv = 1.0 / (e0 + e1 + e2)
    w0, w1, w2 = e0 * inv, e1 * inv, e2 * inv
    for hh, sl in enumerate(_heads(DIL_GROUP_WIDTH)):
        h1 = slice(hh, hh + 1)
        out = (w0[:, h1] * o0_ref[0, 0, :, sl].astype(F32) + w1[:, h1] * so1[hh]
               + w2[:, h1] * so2[hh])
        out_ref[0, :, sl] = out.astype(out_ref.dtype)


def dilated_merge(outs, tr=512):
    (o0, l0), (o1, l1), (o2, l2) = outs
    b, _, s_len, w = o0.shape

    def spec(a):
        dil = a.shape[1]
        return pl.BlockSpec((1, dil, tr // dil, a.shape[3]), lambda bb, i: (bb, 0, i, 0))

    slab_o = pltpu.VMEM((w // HEAD_DIM, tr, HEAD_DIM), F32)
    slab_l = pltpu.VMEM((tr, HEAD_DIM), F32)
    return pl.pallas_call(
        _dil_merge_kernel,
        grid=(b, s_len // tr),
        in_specs=[spec(o0), spec(l0), spec(o1), spec(l1), spec(o2), spec(l2)],
        out_specs=pl.BlockSpec((1, tr, w), lambda bb, i: (bb, i, 0)),
        out_shape=jax.ShapeDtypeStruct((b, s_len, w), BF16),
        scratch_shapes=[slab_o, slab_l, slab_o, slab_l],
        compiler_params=_params("parallel", "arbitrary"),
        name="dilated_merge",
    )(o0, l0, o1, l1, o2, l2)


def _diff_kernel(q_ref, k_ref, v_ref, lq1, lk1, lq2, lk2, g_ref, o_ref,
                 s_scr, p_scr, l_scr, *, lam_init):
    lam = (jnp.exp(jnp.sum(lq1[...] * lk1[...], axis=-1, keepdims=True))
           - jnp.exp(jnp.sum(lq2[...] * lk2[...], axis=-1, keepdims=True)) + lam_init)
    th = q_ref.shape[1] // 2
    n_keys = k_ref.shape[1]

    def scores(half, c, buf):
        sl = slice(c * HEAD_DIM, (c + 1) * HEAD_DIM)
        s_scr[buf] = _qkt(q_ref[0, half * th:(half + 1) * th, sl], k_ref[0, :, sl])

    def softmax(buf):
        for r in range(th // BF16_ROWS):
            rows = slice(r * BF16_ROWS, (r + 1) * BF16_ROWS)
            m = jnp.max(s_scr[buf, rows, :], axis=-1, keepdims=True)
            l = jnp.zeros((BF16_ROWS, 1), F32)
            for c0 in range(0, n_keys, SOFTMAX_CHUNK):
                cols = slice(c0, c0 + SOFTMAX_CHUNK)
                p = jnp.exp2(s_scr[buf, rows, cols] - m)
                l = l + jnp.sum(p, axis=-1, keepdims=True)
                p_scr[buf, rows, cols] = p.astype(BF16)
            l_scr[buf, rows, :] = jnp.broadcast_to(l, (BF16_ROWS, HEAD_DIM))

    def values(buf):
        return jnp.dot(p_scr[buf], v_ref[0], preferred_element_type=F32) / l_scr[buf, :, :1]

    def finish(half, o1, o2):
        o = o1 - lam * o2
        o_ref[0, half * th:(half + 1) * th, :] = (
            _rms(o, g_ref[...]) * (1.0 - lam_init)).astype(o_ref.dtype)

    scores(0, 0, 0)
    scores(0, 1, 1)
    softmax(0)
    scores(1, 0, 0)
    softmax(1)
    o_a1 = values(0)
    scores(1, 1, 1)
    softmax(0)
    o_a2 = values(1)
    finish(0, o_a1, o_a2)
    softmax(1)
    o_b1 = values(0)
    o_b2 = values(1)
    finish(1, o_b1, o_b2)


def differential_attention(proj, lq1, lk1, lq2, lk2, subln, lam_init, tq=1024):
    b, s, _ = proj.shape
    vec = pl.BlockSpec((1, HEAD_DIM), lambda bb, h, i: (0, 0))
    cq, ck, cv = P_QB // DIFF_V_DIM, P_KB // DIFF_V_DIM, P_VB // DIFF_V_DIM
    return pl.pallas_call(
        functools.partial(_diff_kernel, lam_init=lam_init),
        grid=(b, DIFF_HEADS, s // tq),
        in_specs=[pl.BlockSpec((1, tq, DIFF_V_DIM), lambda bb, h, i: (bb, i, cq + h)),
                  pl.BlockSpec((1, s, DIFF_V_DIM), lambda bb, h, i: (bb, 0, ck + h)),
                  pl.BlockSpec((1, s, DIFF_V_DIM), lambda bb, h, i: (bb, 0, cv + h)),
                  vec, vec, vec, vec,
                  pl.BlockSpec((1, DIFF_V_DIM), lambda bb, h, i: (0, 0))],
        out_specs=pl.BlockSpec((1, tq, DIFF_V_DIM), lambda bb, h, i: (bb, i, h)),
        out_shape=jax.ShapeDtypeStruct((b, s, DIFF_V_WIDTH), BF16),
        scratch_shapes=[pltpu.VMEM((2, tq // 2, s), F32), pltpu.VMEM((2, tq // 2, s), BF16),
                        pltpu.VMEM((2, tq // 2, HEAD_DIM), F32)],
        compiler_params=_params("parallel", "parallel", "arbitrary"),
        name="differential_attention",
    )(proj, proj, proj, lq1.reshape(1, -1), lk1.reshape(1, -1), lq2.reshape(1, -1),
      lk2.reshape(1, -1), subln.reshape(1, -1))


def _gate_kernel(oa_ref, wa_ref, ob_ref, wb_ref, ga_ref, gb_ref, o_ref):
    ya = jnp.dot(oa_ref[...], wa_ref[...].astype(BF16), preferred_element_type=F32)
    yb = jnp.dot(ob_ref[...], wb_ref[...].astype(BF16), preferred_element_type=F32)
    o = (jax.nn.sigmoid(ga_ref[...].astype(F32)) * ya
         + jax.nn.sigmoid(gb_ref[...].astype(F32)) * yb)
    o_ref[...] = o.astype(o_ref.dtype)


def gated_merge(out_a, w_a, out_b, w_b, proj2d, tm=1024, tn=512):
    t = out_a.shape[0]
    n = w_a.shape[1]
    ca = P_GA // tn
    cb = (P_GA + n) // tn
    n_tiles = n // tn

    def col(i, j):
        return _serpentine(i, j, n_tiles)

    return pl.pallas_call(
        _gate_kernel,
        grid=(t // tm, n_tiles),
        in_specs=[pl.BlockSpec((tm, out_a.shape[1]), lambda i, j: (i, 0)),
                  pl.BlockSpec((w_a.shape[0], tn), lambda i, j: (0, col(i, j))),
                  pl.BlockSpec((tm, out_b.shape[1]), lambda i, j: (i, 0)),
                  pl.BlockSpec((w_b.shape[0], tn), lambda i, j: (0, col(i, j))),
                  pl.BlockSpec((tm, tn), lambda i, j: (i, ca + col(i, j))),
                  pl.BlockSpec((tm, tn), lambda i, j: (i, cb + col(i, j)))],
        out_specs=pl.BlockSpec((tm, tn), lambda i, j: (i, col(i, j))),
        out_shape=jax.ShapeDtypeStruct((t, n), BF16),
        compiler_params=_params("parallel", "arbitrary"),
        name="gated_merge",
    )(out_a, w_a, out_b, w_b, proj2d, proj2d)


def _resnorm_kernel(y_ref, x_ref, gp_ref, gn_ref, xo_ref, ho_ref):
    xn = x_ref[...] + _rms(y_ref[...].astype(F32), gp_ref[...])
    xo_ref[...] = xn
    ho_ref[...] = _rms(xn, gn_ref[...]).astype(ho_ref.dtype)


def _resnorm_last_kernel(y_ref, x_ref, gp_ref, xo_ref):
    xo_ref[...] = x_ref[...] + _rms(y_ref[...].astype(F32), gp_ref[...])


def residual_norm(y, x, g_post, g_next=None, tr=256):
    t, d = x.shape
    row = pl.BlockSpec((tr, d), lambda i: (i, 0))
    vec = pl.BlockSpec((1, d), lambda i: (0, 0))
    if g_next is None:
        return pl.pallas_call(
            _resnorm_last_kernel,
            grid=(t // tr,),
            in_specs=[row, row, vec],
            out_specs=row,
            out_shape=jax.ShapeDtypeStruct((t, d), F32),
            compiler_params=_params("parallel"),
            name="residual_norm_last",
        )(y, x, g_post.reshape(1, d))
    return pl.pallas_call(
        _resnorm_kernel,
        grid=(t // tr,),
        in_specs=[row, row, vec, vec],
        out_specs=[row, row],
        out_shape=[jax.ShapeDtypeStruct((t, d), F32), jax.ShapeDtypeStruct((t, d), BF16)],
        compiler_params=_params("parallel"),
        name="residual_norm",
    )(y, x, g_post.reshape(1, d), g_next.reshape(1, d))


def _mem_block_kernel(y_ref, x_ref, gp1_ref, gpre_ref, wq_ref, kv_ref, wo_ref, gp2_ref, gn_ref,
                      x2_ref, hn_ref, o_scr):
    x1 = x_ref[...] + _rms(y_ref[...].astype(F32), gp1_ref[...])
    h = _rms(x1, gpre_ref[...]).astype(BF16)
    q = (jnp.dot(h, wq_ref[...], preferred_element_type=F32) * Q_PRESCALE).astype(BF16)
    for hh, sl in enumerate(_heads(MEM_WIDTH)):
        slv = slice(MEM_WIDTH + hh * HEAD_DIM, MEM_WIDTH + (hh + 1) * HEAD_DIM)
        s = _qkt(q[:, sl], kv_ref[0, :, sl])
        p = jnp.exp2(s - jnp.max(s, axis=-1, keepdims=True))
        l = jnp.sum(p, axis=-1, keepdims=True)
        o = jnp.dot(p.astype(BF16), kv_ref[0, :, slv], preferred_element_type=F32) / l
        o_scr[:, sl] = o.astype(o_scr.dtype)
    y2 = jnp.dot(o_scr[...], wo_ref[...], preferred_element_type=F32)
    x2 = x1 + _rms(y2, gp2_ref[...])
    x2_ref[...] = x2
    hn_ref[...] = _rms(x2, gn_ref[...]).astype(hn_ref.dtype)


def memory_block(y, x, g_post1, g_pre, w_q, kv, w_o, g_post2, g_next, seq_len, tr=256):
    t, d = x.shape
    tiles_per_seq = seq_len // tr
    row = pl.BlockSpec((tr, d), lambda i: (i, 0))
    vec = pl.BlockSpec((1, d), lambda i: (0, 0))
    once = pl.Buffered(1)
    return pl.pallas_call(
        _mem_block_kernel,
        grid=(t // tr,),
        in_specs=[row, row, vec, vec,
                  pl.BlockSpec(w_q.shape, lambda i: (0, 0), pipeline_mode=once),
                  pl.BlockSpec((1,) + kv.shape[1:], lambda i: (i // tiles_per_seq, 0, 0)),
                  pl.BlockSpec(w_o.shape, lambda i: (0, 0), pipeline_mode=once),
                  vec, vec],
        out_specs=[row, row],
        out_shape=[jax.ShapeDtypeStruct((t, d), F32), jax.ShapeDtypeStruct((t, d), BF16)],
        scratch_shapes=[pltpu.VMEM((tr, MEM_WIDTH), BF16)],
        compiler_params=_params("arbitrary"),
        name="memory_block",
    )(y, x, g_post1.reshape(1, d), g_pre.reshape(1, d), w_q, kv, w_o,
      g_post2.reshape(1, d), g_next.reshape(1, d))


def kernel(x, mem, positions, norm_mix_pre, w_in, w_a, w_b, w_mix_out, norm_mix_post,
           lambda_q1, lambda_k1, lambda_q2, lambda_k2, diff_subln,
           norm_mem_pre, norm_mem_kv, w_mem_q, w_mem_kv, w_mem_o, norm_mem_post,
           norm_mlp_pre, w_mlp_up, w_mlp_down, norm_mlp_post):
    b, s, d = x.shape
    t = b * s
    depth = w_in.shape[0]
    m_len = mem.shape[1]
    xt = x.reshape(t, d)
    memt = mem.reshape(b * m_len, d)
    h, cos, sin = rmsnorm_and_rope_tables(xt, norm_mix_pre[0], positions)
    for layer in range(depth):
        lam_init = 0.8 - 0.6 * float(np.exp(-0.3 * layer))

        w_in_b = w_in[layer]
        proj = in_projection(h, w_in_b, cos, sin)
        proj3 = proj.reshape(b, s, -1)
        tile = DIL_GROUP_WIDTH
        outs = [dilated_group(proj3.reshape(b, 1, s, -1), P_QA // tile, P_KA // tile, P_VA // tile)]
        for g, (_, dil) in enumerate(DIL_CONFIGS):
            if dil > 1:
                qkv = in_projection_dilated(h, w_in_b, cos, sin, g, dil, b)
                outs.append(dilated_group(qkv, 0, 1, 2))
        out_a = dilated_merge(outs).reshape(t, -1)
        out_b = differential_attention(proj3, lambda_q1[layer], lambda_k1[layer],
                                       lambda_q2[layer], lambda_k2[layer],
                                       diff_subln[layer], lam_init).reshape(t, -1)
        merged = gated_merge(out_a, w_a[layer], out_b, w_b[layer], proj)
        y = matmul(merged, w_mix_out[layer], BF16, 1024, 512, name="mix_out")

        mn = rmsnorm(memt, norm_mem_kv[layer])
        kv = matmul(mn, w_mem_kv[layer], BF16, b * m_len, 512, name="mem_kv")
        xt, h = memory_block(y, xt, norm_mix_post[layer], norm_mem_pre[layer],
                             w_mem_q[layer].astype(BF16), kv.reshape(b, m_len, -1),
                             w_mem_o[layer].astype(BF16), norm_mem_post[layer],
                             norm_mlp_pre[layer], s)

        u = matmul(h, w_mlp_up[layer], BF16, 2048, 512, epilogue="relu2", name="mlp_up")
        y = matmul_kgrid(u, w_mlp_down[layer], BF16, 1024, 1024, 2048, name="mlp_down")
        if layer + 1 < depth:
            xt, h = residual_norm(y, xt, norm_mlp_post[layer], norm_mix_pre[layer + 1])
        else:
            xt = residual_norm(y, xt, norm_mlp_post[layer])
    return xt.reshape(b, s, d)
```

```python
import functools

import numpy as np
import jax
import jax.numpy as jnp
from jax import lax
from jax.experimental import pallas as pl
from jax.experimental.pallas import tpu as pltpu

F32 = jnp.float32
BF16 = jnp.bfloat16

HEAD_DIM = 128
BF16_ROWS = 16
SOFTMAX_CHUNK = 512
DIL_CONFIGS = ((128, 1), (512, 4), (2048, 16))
N_DIL_GROUPS = 3
DIL_HEADS = 8
DIL_GROUP_WIDTH = DIL_HEADS * HEAD_DIM
DIL_WIDTH = N_DIL_GROUPS * DIL_GROUP_WIDTH
DIFF_HEADS = 8
DIFF_QK_WIDTH = DIFF_HEADS * 2 * HEAD_DIM
DIFF_V_DIM = 2 * HEAD_DIM
DIFF_V_WIDTH = DIFF_HEADS * DIFF_V_DIM
MEM_HEADS = 4
MEM_WIDTH = MEM_HEADS * HEAD_DIM
ROPE_THETA = 10000.0
Q_BLOCK = 128
MAX_DIL_STEP = 4 * Q_BLOCK
NORM_EPS = 1e-6
MASK_VALUE = -1e30
LOG2E = 1.4426950408889634
Q_PRESCALE = HEAD_DIM ** -0.5 * LOG2E

W_QA = 0
W_KA = W_QA + DIL_WIDTH
W_VA = W_KA + DIL_WIDTH
W_QB = W_VA + DIL_WIDTH

PROJ_TN = 512
P_QA = 0
P_KA = P_QA + DIL_GROUP_WIDTH
P_VA = P_KA + DIL_GROUP_WIDTH
P_QB = P_VA + DIL_GROUP_WIDTH
P_KB = P_QB + DIFF_QK_WIDTH
P_VB = P_KB + DIFF_QK_WIDTH
P_GA = P_VB + DIFF_V_WIDTH

V7X_VMEM_BYTES = 64 * 1024 * 1024
VMEM_LIMIT = V7X_VMEM_BYTES - 6 * 1024 * 1024
MAX_DOUBLE_BUFFERED_ROW_TILE = V7X_VMEM_BYTES // 8


def _params(*sem):
    return pltpu.CompilerParams(dimension_semantics=sem, vmem_limit_bytes=VMEM_LIMIT)


def _rms(x, g):
    return x * lax.rsqrt(jnp.mean(x * x, axis=-1, keepdims=True) + NORM_EPS) * g


def _heads(width):
    return [slice(hh * HEAD_DIM, (hh + 1) * HEAD_DIM) for hh in range(width // HEAD_DIM)]


def _qkt(q, k):
    return lax.dot_general(q, k, (((1,), (1,)), ((), ())), preferred_element_type=F32)


def _row_operand_spec(tm, k, dtype):
    nbytes = tm * k * jnp.dtype(dtype).itemsize
    mode = pl.Buffered(1) if nbytes > MAX_DOUBLE_BUFFERED_ROW_TILE else None
    return pl.BlockSpec((tm, k), lambda i, j: (i, 0), pipeline_mode=mode)


def _serpentine(i, j, n):
    return jnp.where(i % 2 == 0, j, n - 1 - j)


def _rmsnorm_kernel(x_ref, g_ref, o_ref):
    o_ref[...] = _rms(x_ref[...], g_ref[...]).astype(o_ref.dtype)


def rmsnorm(x, g, tr=256):
    t, d = x.shape
    return pl.pallas_call(
        _rmsnorm_kernel,
        grid=(t // tr,),
        in_specs=[pl.BlockSpec((tr, d), lambda i: (i, 0)),
                  pl.BlockSpec((1, d), lambda i: (0, 0))],
        out_specs=pl.BlockSpec((tr, d), lambda i: (i, 0)),
        out_shape=jax.ShapeDtypeStruct((t, d), BF16),
        compiler_params=_params("parallel"),
        name="rmsnorm",
    )(x, g.reshape(1, d))


def _rmsnorm_rope_kernel(x_ref, g_ref, pos_ref, inv_ref, sign_ref, o_ref, cos_ref, sin_ref):
    o_ref[...] = _rms(x_ref[...], g_ref[...]).astype(o_ref.dtype)
    ang = pos_ref[...] * inv_ref[...]
    cos_ref[...] = jnp.cos(ang)
    sin_ref[...] = jnp.sin(ang) * sign_ref[...]


def rmsnorm_and_rope_tables(x, g, positions, tr=256):
    t, d = x.shape
    half = HEAD_DIM // 2
    inv = ROPE_THETA ** (-jnp.arange(0, HEAD_DIM, 2, dtype=F32) / HEAD_DIM)
    inv2 = jnp.concatenate([inv, inv]).reshape(1, HEAD_DIM)
    sign = jnp.concatenate([-jnp.ones((half,), F32), jnp.ones((half,), F32)]).reshape(1, HEAD_DIM)
    pos = positions.astype(F32).reshape(t, 1)
    vec = pl.BlockSpec((1, HEAD_DIM), lambda i: (0, 0))
    table = pl.BlockSpec((tr, HEAD_DIM), lambda i: (i, 0))
    return pl.pallas_call(
        _rmsnorm_rope_kernel,
        grid=(t // tr,),
        in_specs=[pl.BlockSpec((tr, d), lambda i: (i, 0)),
                  pl.BlockSpec((1, d), lambda i: (0, 0)),
                  pl.BlockSpec((tr, 1), lambda i: (i, 0)), vec, vec],
        out_specs=[pl.BlockSpec((tr, d), lambda i: (i, 0)), table, table],
        out_shape=[jax.ShapeDtypeStruct((t, d), BF16),
                   jax.ShapeDtypeStruct((t, HEAD_DIM), F32),
                   jax.ShapeDtypeStruct((t, HEAD_DIM), F32)],
        compiler_params=_params("parallel"),
        name="rmsnorm_rope_tables",
    )(x, g.reshape(1, d), pos, inv2, sign)


def _rope(xh, c, s):
    return xh * c + pltpu.roll(xh, HEAD_DIM // 2, 1) * s


def _inproj_kernel(h_ref, w_ref, cos_ref, sin_ref, o_ref):
    j = _serpentine(pl.program_id(0), pl.program_id(1), pl.num_programs(1))
    acc = jnp.dot(h_ref[...], w_ref[...].astype(BF16), preferred_element_type=F32)
    is_q = (j < P_KA // PROJ_TN) | ((j >= P_QB // PROJ_TN) & (j < P_KB // PROJ_TN))
    is_k = (((j >= P_KA // PROJ_TN) & (j < P_VA // PROJ_TN))
            | ((j >= P_KB // PROJ_TN) & (j < P_VB // PROJ_TN)))

    @pl.when(is_q | is_k)
    def _():
        f = jnp.where(is_q, Q_PRESCALE, 1.0).astype(F32)
        c = cos_ref[...] * f
        s = sin_ref[...] * f
        for sl in _heads(PROJ_TN):
            o_ref[:, sl] = _rope(acc[:, sl], c, s).astype(o_ref.dtype)

    @pl.when(jnp.logical_not(is_q | is_k))
    def _():
        o_ref[...] = acc.astype(o_ref.dtype)


def in_projection(h, w, cos, sin, tm=2048):
    t, k = h.shape
    tn = PROJ_TN
    n_out = w.shape[1] - 3 * (DIL_WIDTH - DIL_GROUP_WIDTH)
    per = DIL_GROUP_WIDTH // tn
    n_group0 = 3 * per
    stride_a = DIL_WIDTH // tn
    skip = W_QB // tn - n_group0

    n_tiles = n_out // tn

    def w_map(i, j):
        jc = _serpentine(i, j, n_tiles)
        return (0, jnp.where(jc < n_group0, (jc // per) * stride_a + jc % per, jc + skip))

    return pl.pallas_call(
        _inproj_kernel,
        grid=(t // tm, n_tiles),
        in_specs=[_row_operand_spec(tm, k, h.dtype),
                  pl.BlockSpec((k, tn), w_map),
                  pl.BlockSpec((tm, HEAD_DIM), lambda i, j: (i, 0)),
                  pl.BlockSpec((tm, HEAD_DIM), lambda i, j: (i, 0))],
        out_specs=pl.BlockSpec((tm, tn), lambda i, j: (i, _serpentine(i, j, n_tiles))),
        out_shape=jax.ShapeDtypeStruct((t, n_out), BF16),
        compiler_params=_params("parallel", "arbitrary"),
        name="in_projection",
    )(h, w, cos, sin)


def _inproj_dilated_kernel(h_ref, w_ref, cos_ref, sin_ref, o_ref, slab_ref, *, dil):
    j = _serpentine(pl.program_id(0), pl.program_id(1), pl.num_programs(1))
    seg = j // (DIL_GROUP_WIDTH // PROJ_TN)
    tm = h_ref.shape[0]
    acc = jnp.dot(h_ref[...], w_ref[...].astype(BF16), preferred_element_type=F32)
    f = jnp.where(seg == 0, Q_PRESCALE, 1.0).astype(F32)
    c = jnp.where(seg < 2, cos_ref[...] * f, 1.0)
    s = jnp.where(seg < 2, sin_ref[...] * f, 0.0)
    for hh, sl in enumerate(_heads(PROJ_TN)):
        slab_ref[hh] = _rope(acc[:, sl], c, s)
        for r in range(dil):
            o_ref[0, r, :, sl] = slab_ref[hh, pl.ds(r, tm // dil, stride=dil), :].astype(o_ref.dtype)


def in_projection_dilated(h, w, cos, sin, g, dil, batch, tm=1024):
    t, k = h.shape
    tn = PROJ_TN
    s_len = t // batch
    tiles_per_seq = s_len // tm
    per = DIL_GROUP_WIDTH // tn
    stride_a = DIL_WIDTH // tn
    n_tiles = 3 * per

    def w_map(i, j):
        jc = _serpentine(i, j, n_tiles)
        return (0, (jc // per) * stride_a + g * per + jc % per)

    return pl.pallas_call(
        functools.partial(_inproj_dilated_kernel, dil=dil),
        grid=(t // tm, n_tiles),
        in_specs=[pl.BlockSpec((tm, k), lambda i, j: (i, 0)),
                  pl.BlockSpec((k, tn), w_map),
                  pl.BlockSpec((tm, HEAD_DIM), lambda i, j: (i, 0)),
                  pl.BlockSpec((tm, HEAD_DIM), lambda i, j: (i, 0))],
        out_specs=pl.BlockSpec((1, dil, tm // dil, tn),
                               lambda i, j: (i // tiles_per_seq, 0, i % tiles_per_seq,
                                             _serpentine(i, j, n_tiles))),
        out_shape=jax.ShapeDtypeStruct((batch, dil, s_len // dil, 3 * DIL_GROUP_WIDTH), BF16),
        scratch_shapes=[pltpu.VMEM((tn // HEAD_DIM, tm, HEAD_DIM), F32)],
        compiler_params=_params("parallel", "arbitrary"),
        name=f"in_projection_d{dil}",
    )(h, w, cos, sin)


def _mm_kernel(a_ref, w_ref, o_ref, *, epilogue):
    acc = jnp.dot(a_ref[...], w_ref[...].astype(BF16), preferred_element_type=F32)
    if epilogue == "relu2":
        acc = jnp.square(jnp.maximum(acc, 0.0))
    o_ref[...] = acc.astype(o_ref.dtype)


def matmul(a, w, out_dtype, tm, tn, epilogue=None, name="matmul"):
    m, k = a.shape
    n = w.shape[1]
    n_tiles = n // tn
    return pl.pallas_call(
        functools.partial(_mm_kernel, epilogue=epilogue),
        grid=(m // tm, n_tiles),
        in_specs=[_row_operand_spec(tm, k, a.dtype),
                  pl.BlockSpec((k, tn), lambda i, j: (0, _serpentine(i, j, n_tiles)))],
        out_specs=pl.BlockSpec((tm, tn), lambda i, j: (i, _serpentine(i, j, n_tiles))),
        out_shape=jax.ShapeDtypeStruct((m, n), out_dtype),
        compiler_params=_params("parallel", "arbitrary"),
        name=name,
    )(a, w)


def _mm_kgrid_kernel(a_ref, w_ref, o_ref, acc_ref):
    kk = pl.program_id(2)

    @pl.when(kk == 0)
    def _():
        acc_ref[...] = jnp.zeros_like(acc_ref)

    acc_ref[...] += jnp.dot(a_ref[...], w_ref[...].astype(BF16), preferred_element_type=F32)

    @pl.when(kk == pl.num_programs(2) - 1)
    def _():
        o_ref[...] = acc_ref[...].astype(o_ref.dtype)


def matmul_kgrid(a, w, out_dtype, tm, tn, tk, name="matmul_kgrid"):
    m, k = a.shape
    n = w.shape[1]
    return pl.pallas_call(
        _mm_kgrid_kernel,
        grid=(m // tm, n // tn, k // tk),
        in_specs=[pl.BlockSpec((tm, tk), lambda i, j, kk: (i, kk)),
                  pl.BlockSpec((tk, tn), lambda i, j, kk: (kk, j))],
        out_specs=pl.BlockSpec((tm, tn), lambda i, j, kk: (i, j)),
        out_shape=jax.ShapeDtypeStruct((m, n), out_dtype),
        scratch_shapes=[pltpu.VMEM((tm, tn), F32)],
        compiler_params=_params("parallel", "parallel", "arbitrary"),
        name=name,
    )(a, w)


def _dilated_kernel(q_ref, kp_ref, kc_ref, kn_ref, vp_ref, vc_ref, vn_ref,
                    o_ref, lse_ref, *, class_len):
    i = pl.program_id(2)
    step = q_ref.shape[2]
    half = Q_BLOCK // 2
    nk = 2 * Q_BLOCK
    row = lax.broadcasted_iota(jnp.int32, (Q_BLOCK, nk), 0)
    col = lax.broadcasted_iota(jnp.int32, (Q_BLOCK, nk), 1)
    lane = lax.broadcasted_iota(jnp.int32, (Q_BLOCK, HEAD_DIM), 1)
    rel = col - half - row
    in_band = (rel >= -half) & (rel <= half)

    def window(prev_ref, cur_ref, next_ref, q0, sl):
        parts = []
        if q0 == 0:
            parts.append(prev_ref[0, 0, :, sl])
        lo, hi = max(q0 - half, 0), min(q0 + Q_BLOCK + half, step)
        parts.append(cur_ref[0, 0, lo:hi, sl])
        if q0 + Q_BLOCK == step:
            parts.append(next_ref[0, 0, :, sl])
        return jnp.concatenate(parts, axis=0)

    for sb in range(step // Q_BLOCK):
        q0 = sb * Q_BLOCK
        kpos = i * step + q0 - half + col
        valid = in_band & (kpos >= 0) & (kpos < class_len)
        lse_tile = jnp.zeros((Q_BLOCK, HEAD_DIM), F32)
        for hh, sl in enumerate(_heads(DIL_GROUP_WIDTH)):
            q = q_ref[0, 0, q0:q0 + Q_BLOCK, sl]
            k, v = (window(kp_ref, kc_ref, kn_ref, q0, sl), window(vp_ref, vc_ref, vn_ref, q0, sl))
            s = jnp.where(valid, _qkt(q, k), MASK_VALUE)
            m = jnp.max(s, axis=-1, keepdims=True)
            p = jnp.exp2(s - m)
            l = jnp.sum(p, axis=-1, keepdims=True)
            o = jnp.dot(p.astype(BF16), v, preferred_element_type=F32) / l
            o_ref[0, 0, q0:q0 + Q_BLOCK, sl] = o.astype(o_ref.dtype)
            lse_tile = jnp.where(lane == hh, m + jnp.log2(l), lse_tile)
        lse_ref[0, 0, q0:q0 + Q_BLOCK, :] = lse_tile


def dilated_group(qkv, cq, ck, cv):
    b, dil, class_len, _ = qkv.shape
    half = Q_BLOCK // 2
    step = min(class_len, MAX_DIL_STEP)
    per_step = step // half
    n_half_blocks = class_len // half
    w = DIL_GROUP_WIDTH

    def cur(c):
        return pl.BlockSpec((1, 1, step, w), lambda bb, r, i: (bb, r, i, c))

    def prev(c):
        return pl.BlockSpec((1, 1, half, w),
                            lambda bb, r, i: (bb, r, jnp.maximum(per_step * i - 1, 0), c))

    def nxt(c):
        return pl.BlockSpec((1, 1, half, w),
                            lambda bb, r, i: (bb, r, jnp.minimum(per_step * (i + 1), n_half_blocks - 1), c))

    return pl.pallas_call(
        functools.partial(_dilated_kernel, class_len=class_len),
        grid=(b, dil, class_len // step),
        in_specs=[cur(cq), prev(ck), cur(ck), nxt(ck), prev(cv), cur(cv), nxt(cv)],
        out_specs=[pl.BlockSpec((1, 1, step, w), lambda bb, r, i: (bb, r, i, 0)),
                   pl.BlockSpec((1, 1, step, HEAD_DIM), lambda bb, r, i: (bb, r, i, 0))],
        out_shape=[jax.ShapeDtypeStruct((b, dil, class_len, w), BF16),
                   jax.ShapeDtypeStruct((b, dil, class_len, HEAD_DIM), F32)],
        compiler_params=_params("parallel", "parallel", "arbitrary"),
        name=f"dilated_attention_d{dil}",
    )(qkv, qkv, qkv, qkv, qkv, qkv, qkv)


def _dil_merge_kernel(o0_ref, l0_ref, o1_ref, l1_ref, o2_ref, l2_ref, out_ref,
                      so1, sl1, so2, sl2):
    for src_o, src_l, dst_o, dst_l in ((o1_ref, l1_ref, so1, sl1), (o2_ref, l2_ref, so2, sl2)):
        dil, n = src_o.shape[1], src_o.shape[2]
        for r in range(dil):
            dst_l[pl.ds(r, n, stride=dil), :] = src_l[0, r]
            for hh, sl in enumerate(_heads(DIL_GROUP_WIDTH)):
                dst_o[hh, pl.ds(r, n, stride=dil), :] = src_o[0, r, :, sl].astype(F32)
    a0, a1, a2 = l0_ref[0, 0], sl1[...], sl2[...]
    m = jnp.maximum(jnp.maximum(a0, a1), a2)
    e0, e1, e2 = jnp.exp2(a0 - m), jnp.exp2(a1 - m), jnp.exp2(a2 - m)
    inv = 1.0 / (e0 + e1 + e2)
    w0, w1, w2 = e0 * inv, e1 * inv, e2 * inv
    for hh, sl in enumerate(_heads(DIL_GROUP_WIDTH)):
        h1 = slice(hh, hh + 1)
        out = (w0[:, h1] * o0_ref[0, 0, :, sl].astype(F32) + w1[:, h1] * so1[hh]
               + w2[:, h1] * so2[hh])
        out_ref[0, :, sl] = out.astype(out_ref.dtype)


def dilated_merge(outs, tr=512):
    (o0, l0), (o1, l1), (o2, l2) = outs
    b, _, s_len, w = o0.shape

    def spec(a):
        dil = a.shape[1]
        return pl.BlockSpec((1, dil, tr // dil, a.shape[3]), lambda bb, i: (bb, 0, i, 0))

    slab_o = pltpu.VMEM((w // HEAD_DIM, tr, HEAD_DIM), F32)
    slab_l = pltpu.VMEM((tr, HEAD_DIM), F32)
    return pl.pallas_call(
        _dil_merge_kernel,
        grid=(b, s_len // tr),
        in_specs=[spec(o0), spec(l0), spec(o1), spec(l1), spec(o2), spec(l2)],
        out_specs=pl.BlockSpec((1, tr, w), lambda bb, i: (bb, i, 0)),
        out_shape=jax.ShapeDtypeStruct((b, s_len, w), BF16),
        scratch_shapes=[slab_o, slab_l, slab_o, slab_l],
        compiler_params=_params("parallel", "arbitrary"),
        name="dilated_merge",
    )(o0, l0, o1, l1, o2, l2)


def _diff_kernel(q_ref, k_ref, v_ref, lq1, lk1, lq2, lk2, g_ref, o_ref,
                 s_scr, p_scr, l_scr, *, lam_init):
    lam = (jnp.exp(jnp.sum(lq1[...] * lk1[...], axis=-1, keepdims=True))
           - jnp.exp(jnp.sum(lq2[...] * lk2[...], axis=-1, keepdims=True)) + lam_init)
    th = q_ref.shape[1] // 2
    n_keys = k_ref.shape[1]

    def scores(half, c, buf):
        sl = slice(c * HEAD_DIM, (c + 1) * HEAD_DIM)
        s_scr[buf] = _qkt(q_ref[0, half * th:(half + 1) * th, sl], k_ref[0, :, sl])

    def softmax(buf):
        for r in range(th // BF16_ROWS):
            rows = slice(r * BF16_ROWS, (r + 1) * BF16_ROWS)
            m = jnp.max(s_scr[buf, rows, :], axis=-1, keepdims=True)
            for c0 in range(0, n_keys, SOFTMAX_CHUNK):
                cols = slice(c0, c0 + SOFTMAX_CHUNK)
                p_scr[buf, rows, cols] = jnp.exp2(s_scr[buf, rows, cols] - m).astype(BF16)
            l = jnp.sum(p_scr[buf, rows, :].astype(F32), axis=-1, keepdims=True)
            l_scr[buf, rows, :] = jnp.broadcast_to(l, (BF16_ROWS, HEAD_DIM))

    def values(buf):
        return jnp.dot(p_scr[buf], v_ref[0], preferred_element_type=F32) / l_scr[buf, :, :1]

    def finish(half, o1, o2):
        o = o1 - lam * o2
        o_ref[0, half * th:(half + 1) * th, :] = (
            _rms(o, g_ref[...]) * (1.0 - lam_init)).astype(o_ref.dtype)

    scores(0, 0, 0)
    scores(0, 1, 1)
    softmax(0)
    scores(1, 0, 0)
    softmax(1)
    o_a1 = values(0)
    scores(1, 1, 1)
    softmax(0)
    o_a2 = values(1)
    finish(0, o_a1, o_a2)
    softmax(1)
    o_b1 = values(0)
    o_b2 = values(1)
    finish(1, o_b1, o_b2)


def differential_attention(proj, lq1, lk1, lq2, lk2, subln, lam_init, tq=1024):
    b, s, _ = proj.shape
    vec = pl.BlockSpec((1, HEAD_DIM), lambda bb, h, i: (0, 0))
    cq, ck, cv = P_QB // DIFF_V_DIM, P_KB // DIFF_V_DIM, P_VB // DIFF_V_DIM
    return pl.pallas_call(
        functools.partial(_diff_kernel, lam_init=lam_init),
        grid=(b, DIFF_HEADS, s // tq),
        in_specs=[pl.BlockSpec((1, tq, DIFF_V_DIM), lambda bb, h, i: (bb, i, cq + h)),
                  pl.BlockSpec((1, s, DIFF_V_DIM), lambda bb, h, i: (bb, 0, ck + h)),
                  pl.BlockSpec((1, s, DIFF_V_DIM), lambda bb, h, i: (bb, 0, cv + h)),
                  vec, vec, vec, vec,
                  pl.BlockSpec((1, DIFF_V_DIM), lambda bb, h, i: (0, 0))],
        out_specs=pl.BlockSpec((1, tq, DIFF_V_DIM), lambda bb, h, i: (bb, i, h)),
        out_shape=jax.ShapeDtypeStruct((b, s, DIFF_V_WIDTH), BF16),
        scratch_shapes=[pltpu.VMEM((2, tq // 2, s), F32), pltpu.VMEM((2, tq // 2, s), BF16),
                        pltpu.VMEM((2, tq // 2, HEAD_DIM), F32)],
        compiler_params=_params("parallel", "parallel", "arbitrary"),
        name="differential_attention",
    )(proj, proj, proj, lq1.reshape(1, -1), lk1.reshape(1, -1), lq2.reshape(1, -1),
      lk2.reshape(1, -1), subln.reshape(1, -1))


def _gate_kernel(oa_ref, wa_ref, ob_ref, wb_ref, ga_ref, gb_ref, o_ref):
    ya = jnp.dot(oa_ref[...], wa_ref[...].astype(BF16), preferred_element_type=F32)
    yb = jnp.dot(ob_ref[...], wb_ref[...].astype(BF16), preferred_element_type=F32)
    o = (jax.nn.sigmoid(ga_ref[...].astype(F32)) * ya
         + jax.nn.sigmoid(gb_ref[...].astype(F32)) * yb)
    o_ref[...] = o.astype(o_ref.dtype)


def gated_merge(out_a, w_a, out_b, w_b, proj2d, tm=1024, tn=512):
    t = out_a.shape[0]
    n = w_a.shape[1]
    ca = P_GA // tn
    cb = (P_GA + n) // tn
    n_tiles = n // tn

    def col(i, j):
        return _serpentine(i, j, n_tiles)

    return pl.pallas_call(
        _gate_kernel,
        grid=(t // tm, n_tiles),
        in_specs=[pl.BlockSpec((tm, out_a.shape[1]), lambda i, j: (i, 0)),
                  pl.BlockSpec((w_a.shape[0], tn), lambda i, j: (0, col(i, j))),
                  pl.BlockSpec((tm, out_b.shape[1]), lambda i, j: (i, 0)),
                  pl.BlockSpec((w_b.shape[0], tn), lambda i, j: (0, col(i, j))),
                  pl.BlockSpec((tm, tn), lambda i, j: (i, ca + col(i, j))),
                  pl.BlockSpec((tm, tn), lambda i, j: (i, cb + col(i, j)))],
        out_specs=pl.BlockSpec((tm, tn), lambda i, j: (i, col(i, j))),
        out_shape=jax.ShapeDtypeStruct((t, n), BF16),
        compiler_params=_params("parallel", "arbitrary"),
        name="gated_merge",
    )(out_a, w_a, out_b, w_b, proj2d, proj2d)


def _resnorm_kernel(y_ref, x_ref, gp_ref, gn_ref, xo_ref, ho_ref):
    xn = x_ref[...] + _rms(y_ref[...].astype(F32), gp_ref[...])
    xo_ref[...] = xn
    ho_ref[...] = _rms(xn, gn_ref[...]).astype(ho_ref.dtype)


def _resnorm_last_kernel(y_ref, x_ref, gp_ref, xo_ref):
    xo_ref[...] = x_ref[...] + _rms(y_ref[...].astype(F32), gp_ref[...])


def residual_norm(y, x, g_post, g_next=None, tr=256):
    t, d = x.shape
    row = pl.BlockSpec((tr, d), lambda i: (i, 0))
    vec = pl.BlockSpec((1, d), lambda i: (0, 0))
    if g_next is None:
        return pl.pallas_call(
            _resnorm_last_kernel,
            grid=(t // tr,),
            in_specs=[row, row, vec],
            out_specs=row,
            out_shape=jax.ShapeDtypeStruct((t, d), F32),
            compiler_params=_params("parallel"),
            name="residual_norm_last",
        )(y, x, g_post.reshape(1, d))
    return pl.pallas_call(
        _resnorm_kernel,
        grid=(t // tr,),
        in_specs=[row, row, vec, vec],
        out_specs=[row, row],
        out_shape=[jax.ShapeDtypeStruct((t, d), F32), jax.ShapeDtypeStruct((t, d), BF16)],
        compiler_params=_params("parallel"),
        name="residual_norm",
    )(y, x, g_post.reshape(1, d), g_next.reshape(1, d))


def _mem_block_kernel(y_ref, x_ref, gp1_ref, gpre_ref, wq_ref, kv_ref, wo_ref, gp2_ref, gn_ref,
                      x2_ref, hn_ref, o_scr):
    x1 = x_ref[...] + _rms(y_ref[...].astype(F32), gp1_ref[...])
    h = _rms(x1, gpre_ref[...]).astype(BF16)
    q = (jnp.dot(h, wq_ref[...], preferred_element_type=F32) * Q_PRESCALE).astype(BF16)
    for hh, sl in enumerate(_heads(MEM_WIDTH)):
        slv = slice(MEM_WIDTH + hh * HEAD_DIM, MEM_WIDTH + (hh + 1) * HEAD_DIM)
        s = _qkt(q[:, sl], kv_ref[0, :, sl])
        p = jnp.exp2(s - jnp.max(s, axis=-1, keepdims=True))
        l = jnp.sum(p, axis=-1, keepdims=True)
        o = jnp.dot(p.astype(BF16), kv_ref[0, :, slv], preferred_element_type=F32) / l
        o_scr[:, sl] = o.astype(o_scr.dtype)
    y2 = jnp.dot(o_scr[...], wo_ref[...], preferred_element_type=F32)
    x2 = x1 + _rms(y2, gp2_ref[...])
    x2_ref[...] = x2
    hn_ref[...] = _rms(x2, gn_ref[...]).astype(hn_ref.dtype)


def memory_block(y, x, g_post1, g_pre, w_q, kv, w_o, g_post2, g_next, seq_len, tr=256):
    t, d = x.shape
    tiles_per_seq = seq_len // tr
    row = pl.BlockSpec((tr, d), lambda i: (i, 0))
    vec = pl.BlockSpec((1, d), lambda i: (0, 0))
    once = pl.Buffered(1)
    return pl.pallas_call(
        _mem_block_kernel,
        grid=(t // tr,),
        in_specs=[row, row, vec, vec,
                  pl.BlockSpec(w_q.shape, lambda i: (0, 0), pipeline_mode=once),
                  pl.BlockSpec((1,) + kv.shape[1:], lambda i: (i // tiles_per_seq, 0, 0)),
                  pl.BlockSpec(w_o.shape, lambda i: (0, 0), pipeline_mode=once),
                  vec, vec],
        out_specs=[row, row],
        out_shape=[jax.ShapeDtypeStruct((t, d), F32), jax.ShapeDtypeStruct((t, d), BF16)],
        scratch_shapes=[pltpu.VMEM((tr, MEM_WIDTH), BF16)],
        compiler_params=_params("arbitrary"),
        name="memory_block",
    )(y, x, g_post1.reshape(1, d), g_pre.reshape(1, d), w_q, kv, w_o,
      g_post2.reshape(1, d), g_next.reshape(1, d))


def kernel(x, mem, positions, norm_mix_pre, w_in, w_a, w_b, w_mix_out, norm_mix_post,
           lambda_q1, lambda_k1, lambda_q2, lambda_k2, diff_subln,
           norm_mem_pre, norm_mem_kv, w_mem_q, w_mem_kv, w_mem_o, norm_mem_post,
           norm_mlp_pre, w_mlp_up, w_mlp_down, norm_mlp_post):
    b, s, d = x.shape
    t = b * s
    depth = w_in.shape[0]
    m_len = mem.shape[1]
    xt = x.reshape(t, d)
    memt = mem.reshape(b * m_len, d)
    h, cos, sin = rmsnorm_and_rope_tables(xt, norm_mix_pre[0], positions)
    for layer in range(depth):
        lam_init = 0.8 - 0.6 * float(np.exp(-0.3 * layer))

        w_in_b = w_in[layer]
        proj = in_projection(h, w_in_b, cos, sin)
        proj3 = proj.reshape(b, s, -1)
        tile = DIL_GROUP_WIDTH
        outs = [dilated_group(proj3.reshape(b, 1, s, -1), P_QA // tile, P_KA // tile, P_VA // tile)]
        for g, (_, dil) in enumerate(DIL_CONFIGS):
            if dil > 1:
                qkv = in_projection_dilated(h, w_in_b, cos, sin, g, dil, b)
                outs.append(dilated_group(qkv, 0, 1, 2))
        out_a = dilated_merge(outs).reshape(t, -1)
        out_b = differential_attention(proj3, lambda_q1[layer], lambda_k1[layer],
                                       lambda_q2[layer], lambda_k2[layer],
                                       diff_subln[layer], lam_init).reshape(t, -1)
        merged = gated_merge(out_a, w_a[layer], out_b, w_b[layer], proj)
        y = matmul(merged, w_mix_out[layer], BF16, 1024, 512, name="mix_out")

        mn = rmsnorm(memt, norm_mem_kv[layer])
        kv = matmul(mn, w_mem_kv[layer], BF16, b * m_len, 512, name="mem_kv")
        xt, h = memory_block(y, xt, norm_mix_post[layer], norm_mem_pre[layer],
                             w_mem_q[layer].astype(BF16), kv.reshape(b, m_len, -1),
                             w_mem_o[layer].astype(BF16), norm_mem_post[layer],
                             norm_mlp_pre[layer], s)

        u = matmul(h, w_mlp_up[layer], BF16, 2048, 512, epilogue="relu2", name="mlp_up")
        y = matmul_kgrid(u, w_mlp_down[layer], BF16, 1024, 1024, 2048, name="mlp_down")
        if layer + 1 < depth:
            xt, h = residual_norm(y, xt, norm_mlp_post[layer], norm_mix_pre[layer + 1])
        else:
            xt = residual_norm(y, xt, norm_mlp_post[layer])
    return xt.reshape(b, s, d)
```

```python
import functools

import numpy as np
import jax
import jax.numpy as jnp
from jax import lax
from jax.experimental import pallas as pl
from jax.experimental.pallas import tpu as pltpu

F32 = jnp.float32
BF16 = jnp.bfloat16

HEAD_DIM = 128
BF16_ROWS = 16
SOFTMAX_CHUNK = 256
DIL_CONFIGS = ((128, 1), (512, 4), (2048, 16))
N_DIL_GROUPS = 3
DIL_HEADS = 8
DIL_GROUP_WIDTH = DIL_HEADS * HEAD_DIM
DIL_WIDTH = N_DIL_GROUPS * DIL_GROUP_WIDTH
DIFF_HEADS = 8
DIFF_QK_WIDTH = DIFF_HEADS * 2 * HEAD_DIM
DIFF_V_DIM = 2 * HEAD_DIM
DIFF_V_WIDTH = DIFF_HEADS * DIFF_V_DIM
MEM_HEADS = 4
MEM_WIDTH = MEM_HEADS * HEAD_DIM
ROPE_THETA = 10000.0
Q_BLOCK = 128
MAX_DIL_STEP = 4 * Q_BLOCK
NORM_EPS = 1e-6
MASK_VALUE = -1e30
LOG2E = 1.4426950408889634
Q_PRESCALE = HEAD_DIM ** -0.5 * LOG2E

W_QA = 0
W_KA = W_QA + DIL_WIDTH
W_VA = W_KA + DIL_WIDTH
W_QB = W_VA + DIL_WIDTH

PROJ_TN = 512
P_QA = 0
P_KA = P_QA + DIL_GROUP_WIDTH
P_VA = P_KA + DIL_GROUP_WIDTH
P_QB = P_VA + DIL_GROUP_WIDTH
P_KB = P_QB + DIFF_QK_WIDTH
P_VB = P_KB + DIFF_QK_WIDTH
P_GA = P_VB + DIFF_V_WIDTH

V7X_VMEM_BYTES = 64 * 1024 * 1024
VMEM_LIMIT = V7X_VMEM_BYTES - 6 * 1024 * 1024
MAX_DOUBLE_BUFFERED_ROW_TILE = V7X_VMEM_BYTES // 8


def _params(*sem):
    return pltpu.CompilerParams(dimension_semantics=sem, vmem_limit_bytes=VMEM_LIMIT)


def _rms(x, g):
    return x * lax.rsqrt(jnp.mean(x * x, axis=-1, keepdims=True) + NORM_EPS) * g


def _heads(width):
    return [slice(hh * HEAD_DIM, (hh + 1) * HEAD_DIM) for hh in range(width // HEAD_DIM)]


def _qkt(q, k):
    return lax.dot_general(q, k, (((1,), (1,)), ((), ())), preferred_element_type=F32)


def _row_operand_spec(tm, k, dtype):
    nbytes = tm * k * jnp.dtype(dtype).itemsize
    mode = pl.Buffered(1) if nbytes > MAX_DOUBLE_BUFFERED_ROW_TILE else None
    return pl.BlockSpec((tm, k), lambda i, j: (i, 0), pipeline_mode=mode)


def _serpentine(i, j, n):
    return jnp.where(i % 2 == 0, j, n - 1 - j)


def _rmsnorm_kernel(x_ref, g_ref, o_ref):
    o_ref[...] = _rms(x_ref[...], g_ref[...]).astype(o_ref.dtype)


def rmsnorm(x, g, tr=256):
    t, d = x.shape
    return pl.pallas_call(
        _rmsnorm_kernel,
        grid=(t // tr,),
        in_specs=[pl.BlockSpec((tr, d), lambda i: (i, 0)),
                  pl.BlockSpec((1, d), lambda i: (0, 0))],
        out_specs=pl.BlockSpec((tr, d), lambda i: (i, 0)),
        out_shape=jax.ShapeDtypeStruct((t, d), BF16),
        compiler_params=_params("parallel"),
        name="rmsnorm",
    )(x, g.reshape(1, d))


def _rmsnorm_rope_kernel(x_ref, g_ref, pos_ref, inv_ref, sign_ref, o_ref, cos_ref, sin_ref):
    o_ref[...] = _rms(x_ref[...], g_ref[...]).astype(o_ref.dtype)
    ang = pos_ref[...] * inv_ref[...]
    cos_ref[...] = jnp.cos(ang)
    sin_ref[...] = jnp.sin(ang) * sign_ref[...]


def rmsnorm_and_rope_tables(x, g, positions, tr=256):
    t, d = x.shape
    half = HEAD_DIM // 2
    inv = ROPE_THETA ** (-jnp.arange(0, HEAD_DIM, 2, dtype=F32) / HEAD_DIM)
    inv2 = jnp.concatenate([inv, inv]).reshape(1, HEAD_DIM)
    sign = jnp.concatenate([-jnp.ones((half,), F32), jnp.ones((half,), F32)]).reshape(1, HEAD_DIM)
    pos = positions.astype(F32).reshape(t, 1)
    vec = pl.BlockSpec((1, HEAD_DIM), lambda i: (0, 0))
    table = pl.BlockSpec((tr, HEAD_DIM), lambda i: (i, 0))
    return pl.pallas_call(
        _rmsnorm_rope_kernel,
        grid=(t // tr,),
        in_specs=[pl.BlockSpec((tr, d), lambda i: (i, 0)),
                  pl.BlockSpec((1, d), lambda i: (0, 0)),
                  pl.BlockSpec((tr, 1), lambda i: (i, 0)), vec, vec],
        out_specs=[pl.BlockSpec((tr, d), lambda i: (i, 0)), table, table],
        out_shape=[jax.ShapeDtypeStruct((t, d), BF16),
                   jax.ShapeDtypeStruct((t, HEAD_DIM), F32),
                   jax.ShapeDtypeStruct((t, HEAD_DIM), F32)],
        compiler_params=_params("parallel"),
        name="rmsnorm_rope_tables",
    )(x, g.reshape(1, d), pos, inv2, sign)


def _rope(xh, c, s):
    return xh * c + pltpu.roll(xh, HEAD_DIM // 2, 1) * s


def _inproj_kernel(h_ref, w_ref, cos_ref, sin_ref, o_ref):
    j = _serpentine(pl.program_id(0), pl.program_id(1), pl.num_programs(1))
    acc = jnp.dot(h_ref[...], w_ref[...].astype(BF16), preferred_element_type=F32)
    is_q = (j < P_KA // PROJ_TN) | ((j >= P_QB // PROJ_TN) & (j < P_KB // PROJ_TN))
    is_k = (((j >= P_KA // PROJ_TN) & (j < P_VA // PROJ_TN))
            | ((j >= P_KB // PROJ_TN) & (j < P_VB // PROJ_TN)))

    @pl.when(is_q | is_k)
    def _():
        f = jnp.where(is_q, Q_PRESCALE, 1.0).astype(F32)
        c = cos_ref[...] * f
        s = sin_ref[...] * f
        for sl in _heads(PROJ_TN):
            o_ref[:, sl] = _rope(acc[:, sl], c, s).astype(o_ref.dtype)

    @pl.when(jnp.logical_not(is_q | is_k))
    def _():
        o_ref[...] = acc.astype(o_ref.dtype)


def in_projection(h, w, cos, sin, tm=2048):
    t, k = h.shape
    tn = PROJ_TN
    n_out = w.shape[1] - 3 * (DIL_WIDTH - DIL_GROUP_WIDTH)
    per = DIL_GROUP_WIDTH // tn
    n_group0 = 3 * per
    stride_a = DIL_WIDTH // tn
    skip = W_QB // tn - n_group0

    n_tiles = n_out // tn

    def w_map(i, j):
        jc = _serpentine(i, j, n_tiles)
        return (0, jnp.where(jc < n_group0, (jc // per) * stride_a + jc % per, jc + skip))

    return pl.pallas_call(
        _inproj_kernel,
        grid=(t // tm, n_tiles),
        in_specs=[_row_operand_spec(tm, k, h.dtype),
                  pl.BlockSpec((k, tn), w_map),
                  pl.BlockSpec((tm, HEAD_DIM), lambda i, j: (i, 0)),
                  pl.BlockSpec((tm, HEAD_DIM), lambda i, j: (i, 0))],
        out_specs=pl.BlockSpec((tm, tn), lambda i, j: (i, _serpentine(i, j, n_tiles))),
        out_shape=jax.ShapeDtypeStruct((t, n_out), BF16),
        compiler_params=_params("parallel", "arbitrary"),
        name="in_projection",
    )(h, w, cos, sin)


def _inproj_dilated_kernel(h_ref, w_ref, cos_ref, sin_ref, o_ref, slab_ref, *, dil):
    j = _serpentine(pl.program_id(0), pl.program_id(1), pl.num_programs(1))
    seg = j // (DIL_GROUP_WIDTH // PROJ_TN)
    tm = h_ref.shape[0]
    acc = jnp.dot(h_ref[...], w_ref[...].astype(BF16), preferred_element_type=F32)
    f = jnp.where(seg == 0, Q_PRESCALE, 1.0).astype(F32)
    c = jnp.where(seg < 2, cos_ref[...] * f, 1.0)
    s = jnp.where(seg < 2, sin_ref[...] * f, 0.0)
    for hh, sl in enumerate(_heads(PROJ_TN)):
        slab_ref[hh] = _rope(acc[:, sl], c, s)
        for r in range(dil):
            o_ref[0, r, :, sl] = slab_ref[hh, pl.ds(r, tm // dil, stride=dil), :].astype(o_ref.dtype)


def in_projection_dilated(h, w, cos, sin, g, dil, batch, tm=1024):
    t, k = h.shape
    tn = PROJ_TN
    s_len = t // batch
    tiles_per_seq = s_len // tm
    per = DIL_GROUP_WIDTH // tn
    stride_a = DIL_WIDTH // tn
    n_tiles = 3 * per

    def w_map(i, j):
        jc = _serpentine(i, j, n_tiles)
        return (0, (jc // per) * stride_a + g * per + jc % per)

    return pl.pallas_call(
        functools.partial(_inproj_dilated_kernel, dil=dil),
        grid=(t // tm, n_tiles),
        in_specs=[pl.BlockSpec((tm, k), lambda i, j: (i, 0)),
                  pl.BlockSpec((k, tn), w_map),
                  pl.BlockSpec((tm, HEAD_DIM), lambda i, j: (i, 0)),
                  pl.BlockSpec((tm, HEAD_DIM), lambda i, j: (i, 0))],
        out_specs=pl.BlockSpec((1, dil, tm // dil, tn),
                               lambda i, j: (i // tiles_per_seq, 0, i % tiles_per_seq,
                                             _serpentine(i, j, n_tiles))),
        out_shape=jax.ShapeDtypeStruct((batch, dil, s_len // dil, 3 * DIL_GROUP_WIDTH), BF16),
        scratch_shapes=[pltpu.VMEM((tn // HEAD_DIM, tm, HEAD_DIM), F32)],
        compiler_params=_params("parallel", "arbitrary"),
        name=f"in_projection_d{dil}",
    )(h, w, cos, sin)


def _mm_kernel(a_ref, w_ref, o_ref, *, epilogue):
    acc = jnp.dot(a_ref[...], w_ref[...].astype(BF16), preferred_element_type=F32)
    if epilogue == "relu2":
        acc = jnp.square(jnp.maximum(acc, 0.0))
    o_ref[...] = acc.astype(o_ref.dtype)


def matmul(a, w, out_dtype, tm, tn, epilogue=None, name="matmul"):
    m, k = a.shape
    n = w.shape[1]
    n_tiles = n // tn
    return pl.pallas_call(
        functools.partial(_mm_kernel, epilogue=epilogue),
        grid=(m // tm, n_tiles),
        in_specs=[_row_operand_spec(tm, k, a.dtype),
                  pl.BlockSpec((k, tn), lambda i, j: (0, _serpentine(i, j, n_tiles)))],
        out_specs=pl.BlockSpec((tm, tn), lambda i, j: (i, _serpentine(i, j, n_tiles))),
        out_shape=jax.ShapeDtypeStruct((m, n), out_dtype),
        compiler_params=_params("parallel", "arbitrary"),
        name=name,
    )(a, w)


def _mm_kgrid_kernel(a_ref, w_ref, o_ref, acc_ref):
    kk = pl.program_id(2)

    @pl.when(kk == 0)
    def _():
        acc_ref[...] = jnp.zeros_like(acc_ref)

    acc_ref[...] += jnp.dot(a_ref[...], w_ref[...].astype(BF16), preferred_element_type=F32)

    @pl.when(kk == pl.num_programs(2) - 1)
    def _():
        o_ref[...] = acc_ref[...].astype(o_ref.dtype)


def matmul_kgrid(a, w, out_dtype, tm, tn, tk, name="matmul_kgrid"):
    m, k = a.shape
    n = w.shape[1]
    return pl.pallas_call(
        _mm_kgrid_kernel,
        grid=(m // tm, n // tn, k // tk),
        in_specs=[pl.BlockSpec((tm, tk), lambda i, j, kk: (i, kk)),
                  pl.BlockSpec((tk, tn), lambda i, j, kk: (kk, j))],
        out_specs=pl.BlockSpec((tm, tn), lambda i, j, kk: (i, j)),
        out_shape=jax.ShapeDtypeStruct((m, n), out_dtype),
        scratch_shapes=[pltpu.VMEM((tm, tn), F32)],
        compiler_params=_params("parallel", "parallel", "arbitrary"),
        name=name,
    )(a, w)


def _dilated_kernel(q_ref, kp_ref, kc_ref, kn_ref, vp_ref, vc_ref, vn_ref,
                    o_ref, lse_ref, *, class_len):
    i = pl.program_id(2)
    step = q_ref.shape[2]
    half = Q_BLOCK // 2
    nk = 2 * Q_BLOCK
    row = lax.broadcasted_iota(jnp.int32, (Q_BLOCK, nk), 0)
    col = lax.broadcasted_iota(jnp.int32, (Q_BLOCK, nk), 1)
    lane = lax.broadcasted_iota(jnp.int32, (Q_BLOCK, HEAD_DIM), 1)
    rel = col - half - row
    in_band = (rel >= -half) & (rel <= half)

    def window(prev_ref, cur_ref, next_ref, q0, sl):
        parts = []
        if q0 == 0:
            parts.append(prev_ref[0, 0, :, sl])
        lo, hi = max(q0 - half, 0), min(q0 + Q_BLOCK + half, step)
        parts.append(cur_ref[0, 0, lo:hi, sl])
        if q0 + Q_BLOCK == step:
            parts.append(next_ref[0, 0, :, sl])
        return jnp.concatenate(parts, axis=0)

    for sb in range(step // Q_BLOCK):
        q0 = sb * Q_BLOCK
        kpos = i * step + q0 - half + col
        valid = in_band & (kpos >= 0) & (kpos < class_len)
        lse_tile = jnp.zeros((Q_BLOCK, HEAD_DIM), F32)
        for hh, sl in enumerate(_heads(DIL_GROUP_WIDTH)):
            q = q_ref[0, 0, q0:q0 + Q_BLOCK, sl]
            k, v = (window(kp_ref, kc_ref, kn_ref, q0, sl), window(vp_ref, vc_ref, vn_ref, q0, sl))
            s = jnp.where(valid, _qkt(q, k), MASK_VALUE)
            m = jnp.max(s, axis=-1, keepdims=True)
            p = jnp.exp2(s - m)
            l = jnp.sum(p, axis=-1, keepdims=True)
            o = jnp.dot(p.astype(BF16), v, preferred_element_type=F32) / l
            o_ref[0, 0, q0:q0 + Q_BLOCK, sl] = o.astype(o_ref.dtype)
            lse_tile = jnp.where(lane == hh, m + jnp.log2(l), lse_tile)
        lse_ref[0, 0, q0:q0 + Q_BLOCK, :] = lse_tile


def dilated_group(qkv, cq, ck, cv):
    b, dil, class_len, _ = qkv.shape
    half = Q_BLOCK // 2
    step = min(class_len, MAX_DIL_STEP)
    per_step = step // half
    n_half_blocks = class_len // half
    w = DIL_GROUP_WIDTH

    def cur(c):
        return pl.BlockSpec((1, 1, step, w), lambda bb, r, i: (bb, r, i, c))

    def prev(c):
        return pl.BlockSpec((1, 1, half, w),
                            lambda bb, r, i: (bb, r, jnp.maximum(per_step * i - 1, 0), c))

    def nxt(c):
        return pl.BlockSpec((1, 1, half, w),
                            lambda bb, r, i: (bb, r, jnp.minimum(per_step * (i + 1), n_half_blocks - 1), c))

    return pl.pallas_call(
        functools.partial(_dilated_kernel, class_len=class_len),
        grid=(b, dil, class_len // step),
        in_specs=[cur(cq), prev(ck), cur(ck), nxt(ck), prev(cv), cur(cv), nxt(cv)],
        out_specs=[pl.BlockSpec((1, 1, step, w), lambda bb, r, i: (bb, r, i, 0)),
                   pl.BlockSpec((1, 1, step, HEAD_DIM), lambda bb, r, i: (bb, r, i, 0))],
        out_shape=[jax.ShapeDtypeStruct((b, dil, class_len, w), BF16),
                   jax.ShapeDtypeStruct((b, dil, class_len, HEAD_DIM), F32)],
        compiler_params=_params("parallel", "parallel", "arbitrary"),
        name=f"dilated_attention_d{dil}",
    )(qkv, qkv, qkv, qkv, qkv, qkv, qkv)


def _dil_merge_kernel(o0_ref, l0_ref, o1_ref, l1_ref, o2_ref, l2_ref, out_ref,
                      so1, sl1, so2, sl2):
    for src_o, src_l, dst_o, dst_l in ((o1_ref, l1_ref, so1, sl1), (o2_ref, l2_ref, so2, sl2)):
        dil, n = src_o.shape[1], src_o.shape[2]
        for r in range(dil):
            dst_l[pl.ds(r, n, stride=dil), :] = src_l[0, r]
            for hh, sl in enumerate(_heads(DIL_GROUP_WIDTH)):
                dst_o[hh, pl.ds(r, n, stride=dil), :] = src_o[0, r, :, sl].astype(F32)
    a0, a1, a2 = l0_ref[0, 0], sl1[...], sl2[...]
    m = jnp.maximum(jnp.maximum(a0, a1), a2)
    e0, e1, e2 = jnp.exp2(a0 - m), jnp.exp2(a1 - m), jnp.exp2(a2 - m)
    inv = 1.0 / (e0 + e1 + e2)
    w0, w1, w2 = e0 * inv, e1 * inv, e2 * inv
    for hh, sl in enumerate(_heads(DIL_GROUP_WIDTH)):
        h1 = slice(hh, hh + 1)
        out = (w0[:, h1] * o0_ref[0, 0, :, sl].astype(F32) + w1[:, h1] * so1[hh]
               + w2[:, h1] * so2[hh])
        out_ref[0, :, sl] = out.astype(out_ref.dtype)


def dilated_merge(outs, tr=512):
    (o0, l0), (o1, l1), (o2, l2) = outs
    b, _, s_len, w = o0.shape

    def spec(a):
        dil = a.shape[1]
        return pl.BlockSpec((1, dil, tr // dil, a.shape[3]), lambda bb, i: (bb, 0, i, 0))

    slab_o = pltpu.VMEM((w // HEAD_DIM, tr, HEAD_DIM), F32)
    slab_l = pltpu.VMEM((tr, HEAD_DIM), F32)
    return pl.pallas_call(
        _dil_merge_kernel,
        grid=(b, s_len // tr),
        in_specs=[spec(o0), spec(l0), spec(o1), spec(l1), spec(o2), spec(l2)],
        out_specs=pl.BlockSpec((1, tr, w), lambda bb, i: (bb, i, 0)),
        out_shape=jax.ShapeDtypeStruct((b, s_len, w), BF16),
        scratch_shapes=[slab_o, slab_l, slab_o, slab_l],
        compiler_params=_params("parallel", "arbitrary"),
        name="dilated_merge",
    )(o0, l0, o1, l1, o2, l2)


def _diff_kernel(q_ref, k_ref, v_ref, lq1, lk1, lq2, lk2, g_ref, o_ref,
                 s_scr, p_scr, l_scr, *, lam_init):
    lam = (jnp.exp(jnp.sum(lq1[...] * lk1[...], axis=-1, keepdims=True))
           - jnp.exp(jnp.sum(lq2[...] * lk2[...], axis=-1, keepdims=True)) + lam_init)
    th = q_ref.shape[1] // 2
    n_keys = k_ref.shape[1]

    def scores(half, c, buf):
        sl = slice(c * HEAD_DIM, (c + 1) * HEAD_DIM)
        s_scr[buf] = _qkt(q_ref[0, half * th:(half + 1) * th, sl], k_ref[0, :, sl])

    def softmax(buf):
        for r in range(th // BF16_ROWS):
            rows = slice(r * BF16_ROWS, (r + 1) * BF16_ROWS)
            m = jnp.max(s_scr[buf, rows, :], axis=-1, keepdims=True)
            l = jnp.zeros((BF16_ROWS, 1), F32)
            for c0 in range(0, n_keys, SOFTMAX_CHUNK):
                cols = slice(c0, c0 + SOFTMAX_CHUNK)
                p = jnp.exp2(s_scr[buf, rows, cols] - m)
                l = l + jnp.sum(p, axis=-1, keepdims=True)
                p_scr[buf, rows, cols] = p.astype(BF16)
            l_scr[buf, rows, :] = jnp.broadcast_to(l, (BF16_ROWS, HEAD_DIM))

    def values(buf):
        return jnp.dot(p_scr[buf], v_ref[0], preferred_element_type=F32) / l_scr[buf, :, :1]

    def finish(half, o1, o2):
        o = o1 - lam * o2
        o_ref[0, half * th:(half + 1) * th, :] = (
            _rms(o, g_ref[...]) * (1.0 - lam_init)).astype(o_ref.dtype)

    scores(0, 0, 0)
    scores(0, 1, 1)
    softmax(0)
    scores(1, 0, 0)
    softmax(1)
    o_a1 = values(0)
    scores(1, 1, 1)
    softmax(0)
    o_a2 = values(1)
    finish(0, o_a1, o_a2)
    softmax(1)
    o_b1 = values(0)
    o_b2 = values(1)
    finish(1, o_b1, o_b2)


def differential_attention(proj, lq1, lk1, lq2, lk2, subln, lam_init, tq=1024):
    b, s, _ = proj.shape
    vec = pl.BlockSpec((1, HEAD_DIM), lambda bb, h, i: (0, 0))
    cq, ck, cv = P_QB // DIFF_V_DIM, P_KB // DIFF_V_DIM, P_VB // DIFF_V_DIM
    return pl.pallas_call(
        functools.partial(_diff_kernel, lam_init=lam_init),
        grid=(b, DIFF_HEADS, s // tq),
        in_specs=[pl.BlockSpec((1, tq, DIFF_V_DIM), lambda bb, h, i: (bb, i, cq + h)),
                  pl.BlockSpec((1, s, DIFF_V_DIM), lambda bb, h, i: (bb, 0, ck + h)),
                  pl.BlockSpec((1, s, DIFF_V_DIM), lambda bb, h, i: (bb, 0, cv + h)),
                  vec, vec, vec, vec,
                  pl.BlockSpec((1, DIFF_V_DIM), lambda bb, h, i: (0, 0))],
        out_specs=pl.BlockSpec((1, tq, DIFF_V_DIM), lambda bb, h, i: (bb, i, h)),
        out_shape=jax.ShapeDtypeStruct((b, s, DIFF_V_WIDTH), BF16),
        scratch_shapes=[pltpu.VMEM((2, tq // 2, s), F32), pltpu.VMEM((2, tq // 2, s), BF16),
                        pltpu.VMEM((2, tq // 2, HEAD_DIM), F32)],
        compiler_params=_params("parallel", "parallel", "arbitrary"),
        name="differential_attention",
    )(proj, proj, proj, lq1.reshape(1, -1), lk1.reshape(1, -1), lq2.reshape(1, -1),
      lk2.reshape(1, -1), subln.reshape(1, -1))


def _gate_kernel(oa_ref, wa_ref, ob_ref, wb_ref, ga_ref, gb_ref, o_ref):
    ya = jnp.dot(oa_ref[...], wa_ref[...].astype(BF16), preferred_element_type=F32)
    yb = jnp.dot(ob_ref[...], wb_ref[...].astype(BF16), preferred_element_type=F32)
    o = (jax.nn.sigmoid(ga_ref[...].astype(F32)) * ya
         + jax.nn.sigmoid(gb_ref[...].astype(F32)) * yb)
    o_ref[...] = o.astype(o_ref.dtype)


def gated_merge(out_a, w_a, out_b, w_b, proj2d, tm=1024, tn=512):
    t = out_a.shape[0]
    n = w_a.shape[1]
    ca = P_GA // tn
    cb = (P_GA + n) // tn
    n_tiles = n // tn

    def col(i, j):
        return _serpentine(i, j, n_tiles)

    return pl.pallas_call(
        _gate_kernel,
        grid=(t // tm, n_tiles),
        in_specs=[pl.BlockSpec((tm, out_a.shape[1]), lambda i, j: (i, 0)),
                  pl.BlockSpec((w_a.shape[0], tn), lambda i, j: (0, col(i, j))),
                  pl.BlockSpec((tm, out_b.shape[1]), lambda i, j: (i, 0)),
                  pl.BlockSpec((w_b.shape[0], tn), lambda i, j: (0, col(i, j))),
                  pl.BlockSpec((tm, tn), lambda i, j: (i, ca + col(i, j))),
                  pl.BlockSpec((tm, tn), lambda i, j: (i, cb + col(i, j)))],
        out_specs=pl.BlockSpec((tm, tn), lambda i, j: (i, col(i, j))),
        out_shape=jax.ShapeDtypeStruct((t, n), BF16),
        compiler_params=_params("parallel", "arbitrary"),
        name="gated_merge",
    )(out_a, w_a, out_b, w_b, proj2d, proj2d)


def _resnorm_kernel(y_ref, x_ref, gp_ref, gn_ref, xo_ref, ho_ref):
    xn = x_ref[...] + _rms(y_ref[...].astype(F32), gp_ref[...])
    xo_ref[...] = xn
    ho_ref[...] = _rms(xn, gn_ref[...]).astype(ho_ref.dtype)


def _resnorm_last_kernel(y_ref, x_ref, gp_ref, xo_ref):
    xo_ref[...] = x_ref[...] + _rms(y_ref[...].astype(F32), gp_ref[...])


def residual_norm(y, x, g_post, g_next=None, tr=256):
    t, d = x.shape
    row = pl.BlockSpec((tr, d), lambda i: (i, 0))
    vec = pl.BlockSpec((1, d), lambda i: (0, 0))
    if g_next is None:
        return pl.pallas_call(
            _resnorm_last_kernel,
            grid=(t // tr,),
            in_specs=[row, row, vec],
            out_specs=row,
            out_shape=jax.ShapeDtypeStruct((t, d), F32),
            compiler_params=_params("parallel"),
            name="residual_norm_last",
        )(y, x, g_post.reshape(1, d))
    return pl.pallas_call(
        _resnorm_kernel,
        grid=(t // tr,),
        in_specs=[row, row, vec, vec],
        out_specs=[row, row],
        out_shape=[jax.ShapeDtypeStruct((t, d), F32), jax.ShapeDtypeStruct((t, d), BF16)],
        compiler_params=_params("parallel"),
        name="residual_norm",
    )(y, x, g_post.reshape(1, d), g_next.reshape(1, d))


def _mem_block_kernel(y_ref, x_ref, gp1_ref, gpre_ref, wq_ref, kv_ref, wo_ref, gp2_ref, gn_ref,
                      x2_ref, hn_ref, o_scr):
    x1 = x_ref[...] + _rms(y_ref[...].astype(F32), gp1_ref[...])
    h = _rms(x1, gpre_ref[...]).astype(BF16)
    q = (jnp.dot(h, wq_ref[...], preferred_element_type=F32) * Q_PRESCALE).astype(BF16)
    for hh, sl in enumerate(_heads(MEM_WIDTH)):
        slv = slice(MEM_WIDTH + hh * HEAD_DIM, MEM_WIDTH + (hh + 1) * HEAD_DIM)
        s = _qkt(q[:, sl], kv_ref[0, :, sl])
        p = jnp.exp2(s - jnp.max(s, axis=-1, keepdims=True))
        l = jnp.sum(p, axis=-1, keepdims=True)
        o = jnp.dot(p.astype(BF16), kv_ref[0, :, slv], preferred_element_type=F32) / l
        o_scr[:, sl] = o.astype(o_scr.dtype)
    y2 = jnp.dot(o_scr[...], wo_ref[...], preferred_element_type=F32)
    x2 = x1 + _rms(y2, gp2_ref[...])
    x2_ref[...] = x2
    hn_ref[...] = _rms(x2, gn_ref[...]).astype(hn_ref.dtype)


def memory_block(y, x, g_post1, g_pre, w_q, kv, w_o, g_post2, g_next, seq_len, tr=256):
    t, d = x.shape
    tiles_per_seq = seq_len // tr
    row = pl.BlockSpec((tr, d), lambda i: (i, 0))
    vec = pl.BlockSpec((1, d), lambda i: (0, 0))
    once = pl.Buffered(1)
    return pl.pallas_call(
        _mem_block_kernel,
        grid=(t // tr,),
        in_specs=[row, row, vec, vec,
                  pl.BlockSpec(w_q.shape, lambda i: (0, 0), pipeline_mode=once),
                  pl.BlockSpec((1,) + kv.shape[1:], lambda i: (i // tiles_per_seq, 0, 0)),
                  pl.BlockSpec(w_o.shape, lambda i: (0, 0), pipeline_mode=once),
                  vec, vec],
        out_specs=[row, row],
        out_shape=[jax.ShapeDtypeStruct((t, d), F32), jax.ShapeDtypeStruct((t, d), BF16)],
        scratch_shapes=[pltpu.VMEM((tr, MEM_WIDTH), BF16)],
        compiler_params=_params("arbitrary"),
        name="memory_block",
    )(y, x, g_post1.reshape(1, d), g_pre.reshape(1, d), w_q, kv, w_o,
      g_post2.reshape(1, d), g_next.reshape(1, d))


def kernel(x, mem, positions, norm_mix_pre, w_in, w_a, w_b, w_mix_out, norm_mix_post,
           lambda_q1, lambda_k1, lambda_q2, lambda_k2, diff_subln,
           norm_mem_pre, norm_mem_kv, w_mem_q, w_mem_kv, w_mem_o, norm_mem_post,
           norm_mlp_pre, w_mlp_up, w_mlp_down, norm_mlp_post):
    b, s, d = x.shape
    t = b * s
    depth = w_in.shape[0]
    m_len = mem.shape[1]
    xt = x.reshape(t, d)
    memt = mem.reshape(b * m_len, d)
    h, cos, sin = rmsnorm_and_rope_tables(xt, norm_mix_pre[0], positions)
    for layer in range(depth):
        lam_init = 0.8 - 0.6 * float(np.exp(-0.3 * layer))

        w_in_b = w_in[layer]
        proj = in_projection(h, w_in_b, cos, sin)
        proj3 = proj.reshape(b, s, -1)
        tile = DIL_GROUP_WIDTH
        outs = [dilated_group(proj3.reshape(b, 1, s, -1), P_QA // tile, P_KA // tile, P_VA // tile)]
        for g, (_, dil) in enumerate(DIL_CONFIGS):
            if dil > 1:
                qkv = in_projection_dilated(h, w_in_b, cos, sin, g, dil, b)
                outs.append(dilated_group(qkv, 0, 1, 2))
        out_a = dilated_merge(outs).reshape(t, -1)
        out_b = differential_attention(proj3, lambda_q1[layer], lambda_k1[layer],
                                       lambda_q2[layer], lambda_k2[layer],
                                       diff_subln[layer], lam_init).reshape(t, -1)
        merged = gated_merge(out_a, w_a[layer], out_b, w_b[layer], proj)
        y = matmul(merged, w_mix_out[layer], BF16, 1024, 512, name="mix_out")

        mn = rmsnorm(memt, norm_mem_kv[layer])
        kv = matmul(mn, w_mem_kv[layer], BF16, b * m_len, 512, name="mem_kv")
        xt, h = memory_block(y, xt, norm_mix_post[layer], norm_mem_pre[layer],
                             w_mem_q[layer].astype(BF16), kv.reshape(b, m_len, -1),
                             w_mem_o[layer].astype(BF16), norm_mem_post[layer],
                             norm_mlp_pre[layer], s)

        u = matmul(h, w_mlp_up[layer], BF16, 2048, 512, epilogue="relu2", name="mlp_up")
        y = matmul_kgrid(u, w_mlp_down[layer], BF16, 1024, 1024, 2048, name="mlp_down")
        if layer + 1 < depth:
            xt, h = residual_norm(y, xt, norm_mlp_post[layer], norm_mix_pre[layer + 1])
        else:
            xt = residual_norm(y, xt, norm_mlp_post[layer])
    return xt.reshape(b, s, d)
```

```python
import functools

import numpy as np
import jax
import jax.numpy as jnp
from jax import lax
from jax.experimental import pallas as pl
from jax.experimental.pallas import tpu as pltpu

F32 = jnp.float32
BF16 = jnp.bfloat16

HEAD_DIM = 128
BF16_ROWS = 16
SOFTMAX_CHUNK = 256
DIL_CONFIGS = ((128, 1), (512, 4), (2048, 16))
N_DIL_GROUPS = 3
DIL_HEADS = 8
DIL_GROUP_WIDTH = DIL_HEADS * HEAD_DIM
DIL_WIDTH = N_DIL_GROUPS * DIL_GROUP_WIDTH
DIFF_HEADS = 8
DIFF_QK_WIDTH = DIFF_HEADS * 2 * HEAD_DIM
DIFF_V_DIM = 2 * HEAD_DIM
DIFF_V_WIDTH = DIFF_HEADS * DIFF_V_DIM
MEM_HEADS = 4
MEM_WIDTH = MEM_HEADS * HEAD_DIM
ROPE_THETA = 10000.0
Q_BLOCK = 128
MAX_DIL_STEP = 4 * Q_BLOCK
NORM_EPS = 1e-6
MASK_VALUE = -1e30
LOG2E = 1.4426950408889634
Q_PRESCALE = HEAD_DIM ** -0.5 * LOG2E

W_QA = 0
W_KA = W_QA + DIL_WIDTH
W_VA = W_KA + DIL_WIDTH
W_QB = W_VA + DIL_WIDTH

PROJ_TN = 512
P_QA = 0
P_KA = P_QA + DIL_GROUP_WIDTH
P_VA = P_KA + DIL_GROUP_WIDTH
P_QB = P_VA + DIL_GROUP_WIDTH
P_KB = P_QB + DIFF_QK_WIDTH
P_VB = P_KB + DIFF_QK_WIDTH
P_GA = P_VB + DIFF_V_WIDTH

V7X_VMEM_BYTES = 64 * 1024 * 1024
VMEM_LIMIT = V7X_VMEM_BYTES - 6 * 1024 * 1024
MAX_DOUBLE_BUFFERED_ROW_TILE = V7X_VMEM_BYTES // 8


def _params(*sem):
    return pltpu.CompilerParams(dimension_semantics=sem, vmem_limit_bytes=VMEM_LIMIT)


def _rms(x, g):
    return x * lax.rsqrt(jnp.mean(x * x, axis=-1, keepdims=True) + NORM_EPS) * g


def _heads(width):
    return [slice(hh * HEAD_DIM, (hh + 1) * HEAD_DIM) for hh in range(width // HEAD_DIM)]


def _qkt(q, k):
    return lax.dot_general(q, k, (((1,), (1,)), ((), ())), preferred_element_type=F32)


def _row_operand_spec(tm, k, dtype):
    nbytes = tm * k * jnp.dtype(dtype).itemsize
    mode = pl.Buffered(1) if nbytes > MAX_DOUBLE_BUFFERED_ROW_TILE else None
    return pl.BlockSpec((tm, k), lambda i, j: (i, 0), pipeline_mode=mode)


def _serpentine(i, j, n):
    return jnp.where(i % 2 == 0, j, n - 1 - j)


def _rmsnorm_kernel(x_ref, g_ref, o_ref):
    o_ref[...] = _rms(x_ref[...], g_ref[...]).astype(o_ref.dtype)


def rmsnorm(x, g, tr=256):
    t, d = x.shape
    return pl.pallas_call(
        _rmsnorm_kernel,
        grid=(t // tr,),
        in_specs=[pl.BlockSpec((tr, d), lambda i: (i, 0)),
                  pl.BlockSpec((1, d), lambda i: (0, 0))],
        out_specs=pl.BlockSpec((tr, d), lambda i: (i, 0)),
        out_shape=jax.ShapeDtypeStruct((t, d), BF16),
        compiler_params=_params("parallel"),
        name="rmsnorm",
    )(x, g.reshape(1, d))


def _rmsnorm_rope_kernel(x_ref, g_ref, pos_ref, inv_ref, sign_ref, o_ref, cos_ref, sin_ref):
    o_ref[...] = _rms(x_ref[...], g_ref[...]).astype(o_ref.dtype)
    ang = pos_ref[...] * inv_ref[...]
    cos_ref[...] = jnp.cos(ang)
    sin_ref[...] = jnp.sin(ang) * sign_ref[...]


def rmsnorm_and_rope_tables(x, g, positions, tr=512):
    t, d = x.shape
    half = HEAD_DIM // 2
    inv = ROPE_THETA ** (-jnp.arange(0, HEAD_DIM, 2, dtype=F32) / HEAD_DIM)
    inv2 = jnp.concatenate([inv, inv]).reshape(1, HEAD_DIM)
    sign = jnp.concatenate([-jnp.ones((half,), F32), jnp.ones((half,), F32)]).reshape(1, HEAD_DIM)
    pos = positions.astype(F32).reshape(t, 1)
    vec = pl.BlockSpec((1, HEAD_DIM), lambda i: (0, 0))
    table = pl.BlockSpec((tr, HEAD_DIM), lambda i: (i, 0))
    return pl.pallas_call(
        _rmsnorm_rope_kernel,
        grid=(t // tr,),
        in_specs=[pl.BlockSpec((tr, d), lambda i: (i, 0)),
                  pl.BlockSpec((1, d), lambda i: (0, 0)),
                  pl.BlockSpec((tr, 1), lambda i: (i, 0)), vec, vec],
        out_specs=[pl.BlockSpec((tr, d), lambda i: (i, 0)), table, table],
        out_shape=[jax.ShapeDtypeStruct((t, d), BF16),
                   jax.ShapeDtypeStruct((t, HEAD_DIM), F32),
                   jax.ShapeDtypeStruct((t, HEAD_DIM), F32)],
        compiler_params=_params("parallel"),
        name="rmsnorm_rope_tables",
    )(x, g.reshape(1, d), pos, inv2, sign)


def _rope(xh, c, s):
    return xh * c + pltpu.roll(xh, HEAD_DIM // 2, 1) * s


def _inproj_kernel(h_ref, w_ref, cos_ref, sin_ref, o_ref):
    j = _serpentine(pl.program_id(0), pl.program_id(1), pl.num_programs(1))
    acc = jnp.dot(h_ref[...], w_ref[...].astype(BF16), preferred_element_type=F32)
    is_q = (j < P_KA // PROJ_TN) | ((j >= P_QB // PROJ_TN) & (j < P_KB // PROJ_TN))
    is_k = (((j >= P_KA // PROJ_TN) & (j < P_VA // PROJ_TN))
            | ((j >= P_KB // PROJ_TN) & (j < P_VB // PROJ_TN)))

    @pl.when(is_q | is_k)
    def _():
        f = jnp.where(is_q, Q_PRESCALE, 1.0).astype(F32)
        c = cos_ref[...] * f
        s = sin_ref[...] * f
        for sl in _heads(PROJ_TN):
            o_ref[:, sl] = _rope(acc[:, sl], c, s).astype(o_ref.dtype)

    @pl.when(jnp.logical_not(is_q | is_k))
    def _():
        o_ref[...] = acc.astype(o_ref.dtype)


def in_projection(h, w, cos, sin, tm=2048):
    t, k = h.shape
    tn = PROJ_TN
    n_out = w.shape[1] - 3 * (DIL_WIDTH - DIL_GROUP_WIDTH)
    per = DIL_GROUP_WIDTH // tn
    n_group0 = 3 * per
    stride_a = DIL_WIDTH // tn
    skip = W_QB // tn - n_group0

    n_tiles = n_out // tn

    def w_map(i, j):
        jc = _serpentine(i, j, n_tiles)
        return (0, jnp.where(jc < n_group0, (jc // per) * stride_a + jc % per, jc + skip))

    return pl.pallas_call(
        _inproj_kernel,
        grid=(t // tm, n_tiles),
        in_specs=[_row_operand_spec(tm, k, h.dtype),
                  pl.BlockSpec((k, tn), w_map),
                  pl.BlockSpec((tm, HEAD_DIM), lambda i, j: (i, 0)),
                  pl.BlockSpec((tm, HEAD_DIM), lambda i, j: (i, 0))],
        out_specs=pl.BlockSpec((tm, tn), lambda i, j: (i, _serpentine(i, j, n_tiles))),
        out_shape=jax.ShapeDtypeStruct((t, n_out), BF16),
        compiler_params=_params("parallel", "arbitrary"),
        name="in_projection",
    )(h, w, cos, sin)


def _inproj_dilated_kernel(h_ref, w_ref, cos_ref, sin_ref, o_ref, slab_ref, *, dil):
    j = _serpentine(pl.program_id(0), pl.program_id(1), pl.num_programs(1))
    seg = j // (DIL_GROUP_WIDTH // PROJ_TN)
    tm = h_ref.shape[0]
    acc = jnp.dot(h_ref[...], w_ref[...].astype(BF16), preferred_element_type=F32)
    f = jnp.where(seg == 0, Q_PRESCALE, 1.0).astype(F32)
    c = jnp.where(seg < 2, cos_ref[...] * f, 1.0)
    s = jnp.where(seg < 2, sin_ref[...] * f, 0.0)
    for hh, sl in enumerate(_heads(PROJ_TN)):
        slab_ref[hh] = _rope(acc[:, sl], c, s)
        for r in range(dil):
            o_ref[0, r, :, sl] = slab_ref[hh, pl.ds(r, tm // dil, stride=dil), :].astype(o_ref.dtype)


def in_projection_dilated(h, w, cos, sin, g, dil, batch, tm=1024):
    t, k = h.shape
    tn = PROJ_TN
    s_len = t // batch
    tiles_per_seq = s_len // tm
    per = DIL_GROUP_WIDTH // tn
    stride_a = DIL_WIDTH // tn
    n_tiles = 3 * per

    def w_map(i, j):
        jc = _serpentine(i, j, n_tiles)
        return (0, (jc // per) * stride_a + g * per + jc % per)

    return pl.pallas_call(
        functools.partial(_inproj_dilated_kernel, dil=dil),
        grid=(t // tm, n_tiles),
        in_specs=[pl.BlockSpec((tm, k), lambda i, j: (i, 0)),
                  pl.BlockSpec((k, tn), w_map),
                  pl.BlockSpec((tm, HEAD_DIM), lambda i, j: (i, 0)),
                  pl.BlockSpec((tm, HEAD_DIM), lambda i, j: (i, 0))],
        out_specs=pl.BlockSpec((1, dil, tm // dil, tn),
                               lambda i, j: (i // tiles_per_seq, 0, i % tiles_per_seq,
                                             _serpentine(i, j, n_tiles))),
        out_shape=jax.ShapeDtypeStruct((batch, dil, s_len // dil, 3 * DIL_GROUP_WIDTH), BF16),
        scratch_shapes=[pltpu.VMEM((tn // HEAD_DIM, tm, HEAD_DIM), F32)],
        compiler_params=_params("parallel", "arbitrary"),
        name=f"in_projection_d{dil}",
    )(h, w, cos, sin)


def _mm_kernel(a_ref, w_ref, o_ref, *, epilogue):
    acc = jnp.dot(a_ref[...], w_ref[...].astype(BF16), preferred_element_type=F32)
    if epilogue == "relu2":
        acc = jnp.square(jnp.maximum(acc, 0.0))
    o_ref[...] = acc.astype(o_ref.dtype)


def matmul(a, w, out_dtype, tm, tn, epilogue=None, name="matmul"):
    m, k = a.shape
    n = w.shape[1]
    n_tiles = n // tn
    return pl.pallas_call(
        functools.partial(_mm_kernel, epilogue=epilogue),
        grid=(m // tm, n_tiles),
        in_specs=[_row_operand_spec(tm, k, a.dtype),
                  pl.BlockSpec((k, tn), lambda i, j: (0, _serpentine(i, j, n_tiles)))],
        out_specs=pl.BlockSpec((tm, tn), lambda i, j: (i, _serpentine(i, j, n_tiles))),
        out_shape=jax.ShapeDtypeStruct((m, n), out_dtype),
        compiler_params=_params("parallel", "arbitrary"),
        name=name,
    )(a, w)


def _mm_kgrid_kernel(a_ref, w_ref, o_ref, acc_ref):
    kk = pl.program_id(2)

    @pl.when(kk == 0)
    def _():
        acc_ref[...] = jnp.zeros_like(acc_ref)

    acc_ref[...] += jnp.dot(a_ref[...], w_ref[...].astype(BF16), preferred_element_type=F32)

    @pl.when(kk == pl.num_programs(2) - 1)
    def _():
        o_ref[...] = acc_ref[...].astype(o_ref.dtype)


def matmul_kgrid(a, w, out_dtype, tm, tn, tk, name="matmul_kgrid"):
    m, k = a.shape
    n = w.shape[1]
    return pl.pallas_call(
        _mm_kgrid_kernel,
        grid=(m // tm, n // tn, k // tk),
        in_specs=[pl.BlockSpec((tm, tk), lambda i, j, kk: (i, kk)),
                  pl.BlockSpec((tk, tn), lambda i, j, kk: (kk, j))],
        out_specs=pl.BlockSpec((tm, tn), lambda i, j, kk: (i, j)),
        out_shape=jax.ShapeDtypeStruct((m, n), out_dtype),
        scratch_shapes=[pltpu.VMEM((tm, tn), F32)],
        compiler_params=_params("parallel", "parallel", "arbitrary"),
        name=name,
    )(a, w)


def _dilated_kernel(q_ref, kp_ref, kc_ref, kn_ref, vp_ref, vc_ref, vn_ref,
                    o_ref, lse_ref, *, class_len):
    i = pl.program_id(2)
    step = q_ref.shape[2]
    half = Q_BLOCK // 2
    nk = 2 * Q_BLOCK
    row = lax.broadcasted_iota(jnp.int32, (Q_BLOCK, nk), 0)
    col = lax.broadcasted_iota(jnp.int32, (Q_BLOCK, nk), 1)
    lane = lax.broadcasted_iota(jnp.int32, (Q_BLOCK, HEAD_DIM), 1)
    rel = col - half - row
    in_band = (rel >= -half) & (rel <= half)

    def window(prev_ref, cur_ref, next_ref, q0, sl):
        parts = []
        if q0 == 0:
            parts.append(prev_ref[0, 0, :, sl])
        lo, hi = max(q0 - half, 0), min(q0 + Q_BLOCK + half, step)
        parts.append(cur_ref[0, 0, lo:hi, sl])
        if q0 + Q_BLOCK == step:
            parts.append(next_ref[0, 0, :, sl])
        return jnp.concatenate(parts, axis=0)

    for sb in range(step // Q_BLOCK):
        q0 = sb * Q_BLOCK
        kpos = i * step + q0 - half + col
        valid = in_band & (kpos >= 0) & (kpos < class_len)
        lse_tile = jnp.zeros((Q_BLOCK, HEAD_DIM), F32)
        for hh, sl in enumerate(_heads(DIL_GROUP_WIDTH)):
            q = q_ref[0, 0, q0:q0 + Q_BLOCK, sl]
            k, v = (window(kp_ref, kc_ref, kn_ref, q0, sl), window(vp_ref, vc_ref, vn_ref, q0, sl))
            s = jnp.where(valid, _qkt(q, k), MASK_VALUE)
            m = jnp.max(s, axis=-1, keepdims=True)
            p = jnp.exp2(s - m)
            l = jnp.sum(p, axis=-1, keepdims=True)
            o = jnp.dot(p.astype(BF16), v, preferred_element_type=F32) / l
            o_ref[0, 0, q0:q0 + Q_BLOCK, sl] = o.astype(o_ref.dtype)
            lse_tile = jnp.where(lane == hh, m + jnp.log2(l), lse_tile)
        lse_ref[0, 0, q0:q0 + Q_BLOCK, :] = lse_tile


def dilated_group(qkv, cq, ck, cv):
    b, dil, class_len, _ = qkv.shape
    half = Q_BLOCK // 2
    step = min(class_len, MAX_DIL_STEP)
    per_step = step // half
    n_half_blocks = class_len // half
    w = DIL_GROUP_WIDTH

    def cur(c):
        return pl.BlockSpec((1, 1, step, w), lambda bb, r, i: (bb, r, i, c))

    def prev(c):
        return pl.BlockSpec((1, 1, half, w),
                            lambda bb, r, i: (bb, r, jnp.maximum(per_step * i - 1, 0), c))

    def nxt(c):
        return pl.BlockSpec((1, 1, half, w),
                            lambda bb, r, i: (bb, r, jnp.minimum(per_step * (i + 1), n_half_blocks - 1), c))

    return pl.pallas_call(
        functools.partial(_dilated_kernel, class_len=class_len),
        grid=(b, dil, class_len // step),
        in_specs=[cur(cq), prev(ck), cur(ck), nxt(ck), prev(cv), cur(cv), nxt(cv)],
        out_specs=[pl.BlockSpec((1, 1, step, w), lambda bb, r, i: (bb, r, i, 0)),
                   pl.BlockSpec((1, 1, step, HEAD_DIM), lambda bb, r, i: (bb, r, i, 0))],
        out_shape=[jax.ShapeDtypeStruct((b, dil, class_len, w), BF16),
                   jax.ShapeDtypeStruct((b, dil, class_len, HEAD_DIM), F32)],
        compiler_params=_params("parallel", "parallel", "arbitrary"),
        name=f"dilated_attention_d{dil}",
    )(qkv, qkv, qkv, qkv, qkv, qkv, qkv)


def _dil_merge_kernel(o0_ref, l0_ref, o1_ref, l1_ref, o2_ref, l2_ref, out_ref,
                      so1, sl1, so2, sl2):
    for src_o, src_l, dst_o, dst_l in ((o1_ref, l1_ref, so1, sl1), (o2_ref, l2_ref, so2, sl2)):
        dil, n = src_o.shape[1], src_o.shape[2]
        for r in range(dil):
            dst_l[pl.ds(r, n, stride=dil), :] = src_l[0, r]
            for hh, sl in enumerate(_heads(DIL_GROUP_WIDTH)):
                dst_o[hh, pl.ds(r, n, stride=dil), :] = src_o[0, r, :, sl].astype(F32)
    a0, a1, a2 = l0_ref[0, 0], sl1[...], sl2[...]
    m = jnp.maximum(jnp.maximum(a0, a1), a2)
    e0, e1, e2 = jnp.exp2(a0 - m), jnp.exp2(a1 - m), jnp.exp2(a2 - m)
    inv = 1.0 / (e0 + e1 + e2)
    w0, w1, w2 = e0 * inv, e1 * inv, e2 * inv
    for hh, sl in enumerate(_heads(DIL_GROUP_WIDTH)):
        h1 = slice(hh, hh + 1)
        out = (w0[:, h1] * o0_ref[0, 0, :, sl].astype(F32) + w1[:, h1] * so1[hh]
               + w2[:, h1] * so2[hh])
        out_ref[0, :, sl] = out.astype(out_ref.dtype)


def dilated_merge(outs, tr=1024):
    (o0, l0), (o1, l1), (o2, l2) = outs
    b, _, s_len, w = o0.shape

    def spec(a):
        dil = a.shape[1]
        return pl.BlockSpec((1, dil, tr // dil, a.shape[3]), lambda bb, i: (bb, 0, i, 0))

    slab_o = pltpu.VMEM((w // HEAD_DIM, tr, HEAD_DIM), F32)
    slab_l = pltpu.VMEM((tr, HEAD_DIM), F32)
    return pl.pallas_call(
        _dil_merge_kernel,
        grid=(b, s_len // tr),
        in_specs=[spec(o0), spec(l0), spec(o1), spec(l1), spec(o2), spec(l2)],
        out_specs=pl.BlockSpec((1, tr, w), lambda bb, i: (bb, i, 0)),
        out_shape=jax.ShapeDtypeStruct((b, s_len, w), BF16),
        scratch_shapes=[slab_o, slab_l, slab_o, slab_l],
        compiler_params=_params("parallel", "arbitrary"),
        name="dilated_merge",
    )(o0, l0, o1, l1, o2, l2)


def _diff_kernel(q_ref, k_ref, v_ref, lq1, lk1, lq2, lk2, g_ref, o_ref,
                 s_scr, p_scr, l_scr, *, lam_init):
    lam = (jnp.exp(jnp.sum(lq1[...] * lk1[...], axis=-1, keepdims=True))
           - jnp.exp(jnp.sum(lq2[...] * lk2[...], axis=-1, keepdims=True)) + lam_init)
    th = q_ref.shape[1] // 2
    n_keys = k_ref.shape[1]

    def scores(half, c, buf):
        sl = slice(c * HEAD_DIM, (c + 1) * HEAD_DIM)
        s_scr[buf] = _qkt(q_ref[0, half * th:(half + 1) * th, sl], k_ref[0, :, sl])

    def softmax(buf):
        for r in range(th // BF16_ROWS):
            rows = slice(r * BF16_ROWS, (r + 1) * BF16_ROWS)
            m = jnp.max(s_scr[buf, rows, :], axis=-1, keepdims=True)
            l = jnp.zeros((BF16_ROWS, 1), F32)
            for c0 in range(0, n_keys, SOFTMAX_CHUNK):
                cols = slice(c0, c0 + SOFTMAX_CHUNK)
                p = jnp.exp2(s_scr[buf, rows, cols] - m)
                l = l + jnp.sum(p, axis=-1, keepdims=True)
                p_scr[buf, rows, cols] = p.astype(BF16)
            l_scr[buf, rows, :] = jnp.broadcast_to(l, (BF16_ROWS, HEAD_DIM))

    def values(buf):
        return jnp.dot(p_scr[buf], v_ref[0], preferred_element_type=F32) / l_scr[buf, :, :1]

    def finish(half, o1, o2):
        o = o1 - lam * o2
        o_ref[0, half * th:(half + 1) * th, :] = (
            _rms(o, g_ref[...]) * (1.0 - lam_init)).astype(o_ref.dtype)

    scores(0, 0, 0)
    scores(0, 1, 1)
    softmax(0)
    scores(1, 0, 0)
    softmax(1)
    o_a1 = values(0)
    scores(1, 1, 1)
    softmax(0)
    o_a2 = values(1)
    finish(0, o_a1, o_a2)
    softmax(1)
    o_b1 = values(0)
    o_b2 = values(1)
    finish(1, o_b1, o_b2)


def differential_attention(proj, lq1, lk1, lq2, lk2, subln, lam_init, tq=1024):
    b, s, _ = proj.shape
    vec = pl.BlockSpec((1, HEAD_DIM), lambda bb, h, i: (0, 0))
    cq, ck, cv = P_QB // DIFF_V_DIM, P_KB // DIFF_V_DIM, P_VB // DIFF_V_DIM
    return pl.pallas_call(
        functools.partial(_diff_kernel, lam_init=lam_init),
        grid=(b, DIFF_HEADS, s // tq),
        in_specs=[pl.BlockSpec((1, tq, DIFF_V_DIM), lambda bb, h, i: (bb, i, cq + h)),
                  pl.BlockSpec((1, s, DIFF_V_DIM), lambda bb, h, i: (bb, 0, ck + h)),
                  pl.BlockSpec((1, s, DIFF_V_DIM), lambda bb, h, i: (bb, 0, cv + h)),
                  vec, vec, vec, vec,
                  pl.BlockSpec((1, DIFF_V_DIM), lambda bb, h, i: (0, 0))],
        out_specs=pl.BlockSpec((1, tq, DIFF_V_DIM), lambda bb, h, i: (bb, i, h)),
        out_shape=jax.ShapeDtypeStruct((b, s, DIFF_V_WIDTH), BF16),
        scratch_shapes=[pltpu.VMEM((2, tq // 2, s), F32), pltpu.VMEM((2, tq // 2, s), BF16),
                        pltpu.VMEM((2, tq // 2, HEAD_DIM), F32)],
        compiler_params=_params("parallel", "parallel", "arbitrary"),
        name="differential_attention",
    )(proj, proj, proj, lq1.reshape(1, -1), lk1.reshape(1, -1), lq2.reshape(1, -1),
      lk2.reshape(1, -1), subln.reshape(1, -1))


def _gate_kernel(oa_ref, wa_ref, ob_ref, wb_ref, ga_ref, gb_ref, o_ref):
    ya = jnp.dot(oa_ref[...], wa_ref[...].astype(BF16), preferred_element_type=F32)
    yb = jnp.dot(ob_ref[...], wb_ref[...].astype(BF16), preferred_element_type=F32)
    o = (jax.nn.sigmoid(ga_ref[...].astype(F32)) * ya
         + jax.nn.sigmoid(gb_ref[...].astype(F32)) * yb)
    o_ref[...] = o.astype(o_ref.dtype)


def gated_merge(out_a, w_a, out_b, w_b, proj2d, tm=1024, tn=512):
    t = out_a.shape[0]
    n = w_a.shape[1]
    ca = P_GA // tn
    cb = (P_GA + n) // tn
    n_tiles = n // tn

    def col(i, j):
        return _serpentine(i, j, n_tiles)

    return pl.pallas_call(
        _gate_kernel,
        grid=(t // tm, n_tiles),
        in_specs=[pl.BlockSpec((tm, out_a.shape[1]), lambda i, j: (i, 0)),
                  pl.BlockSpec((w_a.shape[0], tn), lambda i, j: (0, col(i, j))),
                  pl.BlockSpec((tm, out_b.shape[1]), lambda i, j: (i, 0)),
                  pl.BlockSpec((w_b.shape[0], tn), lambda i, j: (0, col(i, j))),
                  pl.BlockSpec((tm, tn), lambda i, j: (i, ca + col(i, j))),
                  pl.BlockSpec((tm, tn), lambda i, j: (i, cb + col(i, j)))],
        out_specs=pl.BlockSpec((tm, tn), lambda i, j: (i, col(i, j))),
        out_shape=jax.ShapeDtypeStruct((t, n), BF16),
        compiler_params=_params("parallel", "arbitrary"),
        name="gated_merge",
    )(out_a, w_a, out_b, w_b, proj2d, proj2d)


def _resnorm_kernel(y_ref, x_ref, gp_ref, gn_ref, xo_ref, ho_ref):
    xn = x_ref[...] + _rms(y_ref[...].astype(F32), gp_ref[...])
    xo_ref[...] = xn
    ho_ref[...] = _rms(xn, gn_ref[...]).astype(ho_ref.dtype)


def _resnorm_last_kernel(y_ref, x_ref, gp_ref, xo_ref):
    xo_ref[...] = x_ref[...] + _rms(y_ref[...].astype(F32), gp_ref[...])


def residual_norm(y, x, g_post, g_next=None, tr=512):
    t, d = x.shape
    row = pl.BlockSpec((tr, d), lambda i: (i, 0))
    vec = pl.BlockSpec((1, d), lambda i: (0, 0))
    if g_next is None:
        return pl.pallas_call(
            _resnorm_last_kernel,
            grid=(t // tr,),
            in_specs=[row, row, vec],
            out_specs=row,
            out_shape=jax.ShapeDtypeStruct((t, d), F32),
            compiler_params=_params("parallel"),
            name="residual_norm_last",
        )(y, x, g_post.reshape(1, d))
    return pl.pallas_call(
        _resnorm_kernel,
        grid=(t // tr,),
        in_specs=[row, row, vec, vec],
        out_specs=[row, row],
        out_shape=[jax.ShapeDtypeStruct((t, d), F32), jax.ShapeDtypeStruct((t, d), BF16)],
        compiler_params=_params("parallel"),
        name="residual_norm",
    )(y, x, g_post.reshape(1, d), g_next.reshape(1, d))


def _mem_block_kernel(y_ref, x_ref, gp1_ref, gpre_ref, wq_ref, kv_ref, wo_ref, gp2_ref, gn_ref,
                      x2_ref, hn_ref, o_scr):
    x1 = x_ref[...] + _rms(y_ref[...].astype(F32), gp1_ref[...])
    h = _rms(x1, gpre_ref[...]).astype(BF16)
    q = (jnp.dot(h, wq_ref[...], preferred_element_type=F32) * Q_PRESCALE).astype(BF16)
    for hh, sl in enumerate(_heads(MEM_WIDTH)):
        slv = slice(MEM_WIDTH + hh * HEAD_DIM, MEM_WIDTH + (hh + 1) * HEAD_DIM)
        s = _qkt(q[:, sl], kv_ref[0, :, sl])
        p = jnp.exp2(s - jnp.max(s, axis=-1, keepdims=True))
        l = jnp.sum(p, axis=-1, keepdims=True)
        o = jnp.dot(p.astype(BF16), kv_ref[0, :, slv], preferred_element_type=F32) / l
        o_scr[:, sl] = o.astype(o_scr.dtype)
    y2 = jnp.dot(o_scr[...], wo_ref[...], preferred_element_type=F32)
    x2 = x1 + _rms(y2, gp2_ref[...])
    x2_ref[...] = x2
    hn_ref[...] = _rms(x2, gn_ref[...]).astype(hn_ref.dtype)


def memory_block(y, x, g_post1, g_pre, w_q, kv, w_o, g_post2, g_next, seq_len, tr=256):
    t, d = x.shape
    tiles_per_seq = seq_len // tr
    row = pl.BlockSpec((tr, d), lambda i: (i, 0))
    vec = pl.BlockSpec((1, d), lambda i: (0, 0))
    once = pl.Buffered(1)
    return pl.pallas_call(
        _mem_block_kernel,
        grid=(t // tr,),
        in_specs=[row, row, vec, vec,
                  pl.BlockSpec(w_q.shape, lambda i: (0, 0), pipeline_mode=once),
                  pl.BlockSpec((1,) + kv.shape[1:], lambda i: (i // tiles_per_seq, 0, 0)),
                  pl.BlockSpec(w_o.shape, lambda i: (0, 0), pipeline_mode=once),
                  vec, vec],
        out_specs=[row, row],
        out_shape=[jax.ShapeDtypeStruct((t, d), F32), jax.ShapeDtypeStruct((t, d), BF16)],
        scratch_shapes=[pltpu.VMEM((tr, MEM_WIDTH), BF16)],
        compiler_params=_params("arbitrary"),
        name="memory_block",
    )(y, x, g_post1.reshape(1, d), g_pre.reshape(1, d), w_q, kv, w_o,
      g_post2.reshape(1, d), g_next.reshape(1, d))


def kernel(x, mem, positions, norm_mix_pre, w_in, w_a, w_b, w_mix_out, norm_mix_post,
           lambda_q1, lambda_k1, lambda_q2, lambda_k2, diff_subln,
           norm_mem_pre, norm_mem_kv, w_mem_q, w_mem_kv, w_mem_o, norm_mem_post,
           norm_mlp_pre, w_mlp_up, w_mlp_down, norm_mlp_post):
    b, s, d = x.shape
    t = b * s
    depth = w_in.shape[0]
    m_len = mem.shape[1]
    xt = x.reshape(t, d)
    memt = mem.reshape(b * m_len, d)
    h, cos, sin = rmsnorm_and_rope_tables(xt, norm_mix_pre[0], positions)
    for layer in range(depth):
        lam_init = 0.8 - 0.6 * float(np.exp(-0.3 * layer))

        w_in_b = w_in[layer]
        proj = in_projection(h, w_in_b, cos, sin)
        proj3 = proj.reshape(b, s, -1)
        tile = DIL_GROUP_WIDTH
        outs = [dilated_group(proj3.reshape(b, 1, s, -1), P_QA // tile, P_KA // tile, P_VA // tile)]
        for g, (_, dil) in enumerate(DIL_CONFIGS):
            if dil > 1:
                qkv = in_projection_dilated(h, w_in_b, cos, sin, g, dil, b)
                outs.append(dilated_group(qkv, 0, 1, 2))
        out_a = dilated_merge(outs).reshape(t, -1)
        out_b = differential_attention(proj3, lambda_q1[layer], lambda_k1[layer],
                                       lambda_q2[layer], lambda_k2[layer],
                                       diff_subln[layer], lam_init).reshape(t, -1)
        merged = gated_merge(out_a, w_a[layer], out_b, w_b[layer], proj)
        y = matmul(merged, w_mix_out[layer], BF16, 1024, 512, name="mix_out")

        mn = rmsnorm(memt, norm_mem_kv[layer])
        kv = matmul(mn, w_mem_kv[layer], BF16, b * m_len, 512, name="mem_kv")
        xt, h = memory_block(y, xt, norm_mix_post[layer], norm_mem_pre[layer],
                             w_mem_q[layer].astype(BF16), kv.reshape(b, m_len, -1),
                             w_mem_o[layer].astype(BF16), norm_mem_post[layer],
                             norm_mlp_pre[layer], s)

        u = matmul(h, w_mlp_up[layer], BF16, 2048, 512, epilogue="relu2", name="mlp_up")
        y = matmul_kgrid(u, w_mlp_down[layer], BF16, 1024, 1024, 2048, name="mlp_down")
        if layer + 1 < depth:
            xt, h = residual_norm(y, xt, norm_mlp_post[layer], norm_mix_pre[layer + 1])
        else:
            xt = residual_norm(y, xt, norm_mlp_post[layer])
    return xt.reshape(b, s, d)
```

```python
import functools

import numpy as np
import jax
import jax.numpy as jnp
from jax import lax
from jax.experimental import pallas as pl
from jax.experimental.pallas import tpu as pltpu

F32 = jnp.float32
BF16 = jnp.bfloat16

HEAD_DIM = 128
BF16_ROWS = 16
SOFTMAX_CHUNK = 256
DIL_CONFIGS = ((128, 1), (512, 4), (2048, 16))
N_DIL_GROUPS = 3
DIL_HEADS = 8
DIL_GROUP_WIDTH = DIL_HEADS * HEAD_DIM
DIL_WIDTH = N_DIL_GROUPS * DIL_GROUP_WIDTH
DIFF_HEADS = 8
DIFF_QK_WIDTH = DIFF_HEADS * 2 * HEAD_DIM
DIFF_V_DIM = 2 * HEAD_DIM
DIFF_V_WIDTH = DIFF_HEADS * DIFF_V_DIM
MEM_HEADS = 4
MEM_WIDTH = MEM_HEADS * HEAD_DIM
ROPE_THETA = 10000.0
Q_BLOCK = 128
MAX_DIL_STEP = 4 * Q_BLOCK
NORM_EPS = 1e-6
MASK_VALUE = -1e30
LOG2E = 1.4426950408889634
Q_PRESCALE = HEAD_DIM ** -0.5 * LOG2E

W_QA = 0
W_KA = W_QA + DIL_WIDTH
W_VA = W_KA + DIL_WIDTH
W_QB = W_VA + DIL_WIDTH

PROJ_TN = 512
P_QA = 0
P_KA = P_QA + DIL_GROUP_WIDTH
P_VA = P_KA + DIL_GROUP_WIDTH
P_QB = P_VA + DIL_GROUP_WIDTH
P_KB = P_QB + DIFF_QK_WIDTH
P_VB = P_KB + DIFF_QK_WIDTH
P_GA = P_VB + DIFF_V_WIDTH

V7X_VMEM_BYTES = 64 * 1024 * 1024
VMEM_LIMIT = V7X_VMEM_BYTES - 6 * 1024 * 1024
MAX_DOUBLE_BUFFERED_ROW_TILE = V7X_VMEM_BYTES // 8


def _params(*sem):
    return pltpu.CompilerParams(dimension_semantics=sem, vmem_limit_bytes=VMEM_LIMIT)


def _rms(x, g):
    return x * lax.rsqrt(jnp.mean(x * x, axis=-1, keepdims=True) + NORM_EPS) * g


def _heads(width):
    return [slice(hh * HEAD_DIM, (hh + 1) * HEAD_DIM) for hh in range(width // HEAD_DIM)]


def _qkt(q, k):
    return lax.dot_general(q, k, (((1,), (1,)), ((), ())), preferred_element_type=F32)


def _row_operand_spec(tm, k, dtype):
    nbytes = tm * k * jnp.dtype(dtype).itemsize
    mode = pl.Buffered(1) if nbytes > MAX_DOUBLE_BUFFERED_ROW_TILE else None
    return pl.BlockSpec((tm, k), lambda i, j: (i, 0), pipeline_mode=mode)


def _serpentine(i, j, n):
    return jnp.where(i % 2 == 0, j, n - 1 - j)


def _rmsnorm_kernel(x_ref, g_ref, o_ref):
    o_ref[...] = _rms(x_ref[...], g_ref[...]).astype(o_ref.dtype)


def rmsnorm(x, g, tr=256):
    t, d = x.shape
    return pl.pallas_call(
        _rmsnorm_kernel,
        grid=(t // tr,),
        in_specs=[pl.BlockSpec((tr, d), lambda i: (i, 0)),
                  pl.BlockSpec((1, d), lambda i: (0, 0))],
        out_specs=pl.BlockSpec((tr, d), lambda i: (i, 0)),
        out_shape=jax.ShapeDtypeStruct((t, d), BF16),
        compiler_params=_params("parallel"),
        name="rmsnorm",
    )(x, g.reshape(1, d))


def _rmsnorm_rope_kernel(x_ref, g_ref, pos_ref, inv_ref, sign_ref, o_ref, cos_ref, sin_ref):
    o_ref[...] = _rms(x_ref[...], g_ref[...]).astype(o_ref.dtype)
    ang = pos_ref[...] * inv_ref[...]
    cos_ref[...] = jnp.cos(ang)
    sin_ref[...] = jnp.sin(ang) * sign_ref[...]


def rmsnorm_and_rope_tables(x, g, positions, tr=512):
    t, d = x.shape
    half = HEAD_DIM // 2
    inv = ROPE_THETA ** (-jnp.arange(0, HEAD_DIM, 2, dtype=F32) / HEAD_DIM)
    inv2 = jnp.concatenate([inv, inv]).reshape(1, HEAD_DIM)
    sign = jnp.concatenate([-jnp.ones((half,), F32), jnp.ones((half,), F32)]).reshape(1, HEAD_DIM)
    pos = positions.astype(F32).reshape(t, 1)
    vec = pl.BlockSpec((1, HEAD_DIM), lambda i: (0, 0))
    table = pl.BlockSpec((tr, HEAD_DIM), lambda i: (i, 0))
    return pl.pallas_call(
        _rmsnorm_rope_kernel,
        grid=(t // tr,),
        in_specs=[pl.BlockSpec((tr, d), lambda i: (i, 0)),
                  pl.BlockSpec((1, d), lambda i: (0, 0)),
                  pl.BlockSpec((tr, 1), lambda i: (i, 0)), vec, vec],
        out_specs=[pl.BlockSpec((tr, d), lambda i: (i, 0)), table, table],
        out_shape=[jax.ShapeDtypeStruct((t, d), BF16),
                   jax.ShapeDtypeStruct((t, HEAD_DIM), F32),
                   jax.ShapeDtypeStruct((t, HEAD_DIM), F32)],
        compiler_params=_params("parallel"),
        name="rmsnorm_rope_tables",
    )(x, g.reshape(1, d), pos, inv2, sign)


def _rope(xh, c, s):
    return xh * c + pltpu.roll(xh, HEAD_DIM // 2, 1) * s


def _inproj_kernel(h_ref, w_ref, cos_ref, sin_ref, o_ref):
    j = _serpentine(pl.program_id(0), pl.program_id(1), pl.num_programs(1))
    acc = jnp.dot(h_ref[...], w_ref[...].astype(BF16), preferred_element_type=F32)
    is_q = (j < P_KA // PROJ_TN) | ((j >= P_QB // PROJ_TN) & (j < P_KB // PROJ_TN))
    is_k = (((j >= P_KA // PROJ_TN) & (j < P_VA // PROJ_TN))
            | ((j >= P_KB // PROJ_TN) & (j < P_VB // PROJ_TN)))

    @pl.when(is_q | is_k)
    def _():
        f = jnp.where(is_q, Q_PRESCALE, 1.0).astype(F32)
        c = cos_ref[...] * f
        s = sin_ref[...] * f
        for sl in _heads(PROJ_TN):
            o_ref[:, sl] = _rope(acc[:, sl], c, s).astype(o_ref.dtype)

    @pl.when(jnp.logical_not(is_q | is_k))
    def _():
        o_ref[...] = acc.astype(o_ref.dtype)


def in_projection(h, w, cos, sin, tm=2048):
    t, k = h.shape
    tn = PROJ_TN
    n_out = w.shape[1] - 3 * (DIL_WIDTH - DIL_GROUP_WIDTH)
    per = DIL_GROUP_WIDTH // tn
    n_group0 = 3 * per
    stride_a = DIL_WIDTH // tn
    skip = W_QB // tn - n_group0

    n_tiles = n_out // tn

    def w_map(i, j):
        jc = _serpentine(i, j, n_tiles)
        return (0, jnp.where(jc < n_group0, (jc // per) * stride_a + jc % per, jc + skip))

    return pl.pallas_call(
        _inproj_kernel,
        grid=(t // tm, n_tiles),
        in_specs=[_row_operand_spec(tm, k, h.dtype),
                  pl.BlockSpec((k, tn), w_map),
                  pl.BlockSpec((tm, HEAD_DIM), lambda i, j: (i, 0)),
                  pl.BlockSpec((tm, HEAD_DIM), lambda i, j: (i, 0))],
        out_specs=pl.BlockSpec((tm, tn), lambda i, j: (i, _serpentine(i, j, n_tiles))),
        out_shape=jax.ShapeDtypeStruct((t, n_out), BF16),
        compiler_params=_params("parallel", "arbitrary"),
        name="in_projection",
    )(h, w, cos, sin)


def _inproj_dilated_kernel(h_ref, w_ref, cos_ref, sin_ref, o_ref, slab_ref, *, dil):
    j = _serpentine(pl.program_id(0), pl.program_id(1), pl.num_programs(1))
    seg = j // (DIL_GROUP_WIDTH // PROJ_TN)
    tm = h_ref.shape[0]
    acc = jnp.dot(h_ref[...], w_ref[...].astype(BF16), preferred_element_type=F32)
    f = jnp.where(seg == 0, Q_PRESCALE, 1.0).astype(F32)
    c = jnp.where(seg < 2, cos_ref[...] * f, 1.0)
    s = jnp.where(seg < 2, sin_ref[...] * f, 0.0)
    for hh, sl in enumerate(_heads(PROJ_TN)):
        slab_ref[hh] = _rope(acc[:, sl], c, s)
        for r in range(dil):
            o_ref[0, r, :, sl] = slab_ref[hh, pl.ds(r, tm // dil, stride=dil), :].astype(o_ref.dtype)


def in_projection_dilated(h, w, cos, sin, g, dil, batch, tm=1024):
    t, k = h.shape
    tn = PROJ_TN
    s_len = t // batch
    tiles_per_seq = s_len // tm
    per = DIL_GROUP_WIDTH // tn
    stride_a = DIL_WIDTH // tn
    n_tiles = 3 * per

    def w_map(i, j):
        jc = _serpentine(i, j, n_tiles)
        return (0, (jc // per) * stride_a + g * per + jc % per)

    return pl.pallas_call(
        functools.partial(_inproj_dilated_kernel, dil=dil),
        grid=(t // tm, n_tiles),
        in_specs=[pl.BlockSpec((tm, k), lambda i, j: (i, 0)),
                  pl.BlockSpec((k, tn), w_map),
                  pl.BlockSpec((tm, HEAD_DIM), lambda i, j: (i, 0)),
                  pl.BlockSpec((tm, HEAD_DIM), lambda i, j: (i, 0))],
        out_specs=pl.BlockSpec((1, dil, tm // dil, tn),
                               lambda i, j: (i // tiles_per_seq, 0, i % tiles_per_seq,
                                             _serpentine(i, j, n_tiles))),
        out_shape=jax.ShapeDtypeStruct((batch, dil, s_len // dil, 3 * DIL_GROUP_WIDTH), BF16),
        scratch_shapes=[pltpu.VMEM((tn // HEAD_DIM, tm, HEAD_DIM), F32)],
        compiler_params=_params("parallel", "arbitrary"),
        name=f"in_projection_d{dil}",
    )(h, w, cos, sin)


def _mm_kernel(a_ref, w_ref, o_ref, *, epilogue):
    acc = jnp.dot(a_ref[...], w_ref[...].astype(BF16), preferred_element_type=F32)
    if epilogue == "relu2":
        acc = jnp.square(jnp.maximum(acc, 0.0))
    o_ref[...] = acc.astype(o_ref.dtype)


def matmul(a, w, out_dtype, tm, tn, epilogue=None, name="matmul"):
    m, k = a.shape
    n = w.shape[1]
    n_tiles = n // tn
    return pl.pallas_call(
        functools.partial(_mm_kernel, epilogue=epilogue),
        grid=(m // tm, n_tiles),
        in_specs=[_row_operand_spec(tm, k, a.dtype),
                  pl.BlockSpec((k, tn), lambda i, j: (0, _serpentine(i, j, n_tiles)))],
        out_specs=pl.BlockSpec((tm, tn), lambda i, j: (i, _serpentine(i, j, n_tiles))),
        out_shape=jax.ShapeDtypeStruct((m, n), out_dtype),
        compiler_params=_params("parallel", "arbitrary"),
        name=name,
    )(a, w)


def _mm_kgrid_kernel(a_ref, w_ref, o_ref, acc_ref):
    kk = pl.program_id(2)

    @pl.when(kk == 0)
    def _():
        acc_ref[...] = jnp.zeros_like(acc_ref)

    acc_ref[...] += jnp.dot(a_ref[...], w_ref[...].astype(BF16), preferred_element_type=F32)

    @pl.when(kk == pl.num_programs(2) - 1)
    def _():
        o_ref[...] = acc_ref[...].astype(o_ref.dtype)


def matmul_kgrid(a, w, out_dtype, tm, tn, tk, name="matmul_kgrid"):
    m, k = a.shape
    n = w.shape[1]
    return pl.pallas_call(
        _mm_kgrid_kernel,
        grid=(m // tm, n // tn, k // tk),
        in_specs=[pl.BlockSpec((tm, tk), lambda i, j, kk: (i, kk)),
                  pl.BlockSpec((tk, tn), lambda i, j, kk: (kk, j))],
        out_specs=pl.BlockSpec((tm, tn), lambda i, j, kk: (i, j)),
        out_shape=jax.ShapeDtypeStruct((m, n), out_dtype),
        scratch_shapes=[pltpu.VMEM((tm, tn), F32)],
        compiler_params=_params("parallel", "parallel", "arbitrary"),
        name=name,
    )(a, w)


def _dilated_kernel(q_ref, kp_ref, kc_ref, kn_ref, vp_ref, vc_ref, vn_ref,
                    o_ref, lse_ref, *, class_len):
    i = pl.program_id(2)
    step = q_ref.shape[2]
    half = Q_BLOCK // 2
    nk = 2 * Q_BLOCK
    row = lax.broadcasted_iota(jnp.int32, (Q_BLOCK, nk), 0)
    col = lax.broadcasted_iota(jnp.int32, (Q_BLOCK, nk), 1)
    lane = lax.broadcasted_iota(jnp.int32, (Q_BLOCK, HEAD_DIM), 1)
    rel = col - half - row
    in_band = (rel >= -half) & (rel <= half)

    def window(prev_ref, cur_ref, next_ref, q0, sl):
        parts = []
        if q0 == 0:
            parts.append(prev_ref[0, 0, :, sl])
        lo, hi = max(q0 - half, 0), min(q0 + Q_BLOCK + half, step)
        parts.append(cur_ref[0, 0, lo:hi, sl])
        if q0 + Q_BLOCK == step:
            parts.append(next_ref[0, 0, :, sl])
        return jnp.concatenate(parts, axis=0)

    for sb in range(step // Q_BLOCK):
        q0 = sb * Q_BLOCK
        kpos = i * step + q0 - half + col
        valid = in_band & (kpos >= 0) & (kpos < class_len)
        lse_tile = jnp.zeros((Q_BLOCK, HEAD_DIM), F32)
        for hh, sl in enumerate(_heads(DIL_GROUP_WIDTH)):
            q = q_ref[0, 0, q0:q0 + Q_BLOCK, sl]
            k, v = (window(kp_ref, kc_ref, kn_ref, q0, sl), window(vp_ref, vc_ref, vn_ref, q0, sl))
            s = jnp.where(valid, _qkt(q, k), MASK_VALUE)
            m = jnp.max(s, axis=-1, keepdims=True)
            p = jnp.exp2(s - m)
            l = jnp.sum(p, axis=-1, keepdims=True)
            o = jnp.dot(p.astype(BF16), v, preferred_element_type=F32) / l
            o_ref[0, 0, q0:q0 + Q_BLOCK, sl] = o.astype(o_ref.dtype)
            lse_tile = jnp.where(lane == hh, m + jnp.log2(l), lse_tile)
        lse_ref[0, 0, q0:q0 + Q_BLOCK, :] = lse_tile


def dilated_group(qkv, cq, ck, cv):
    b, dil, class_len, _ = qkv.shape
    half = Q_BLOCK // 2
    step = min(class_len, MAX_DIL_STEP)
    per_step = step // half
    n_half_blocks = class_len // half
    w = DIL_GROUP_WIDTH

    def cur(c):
        return pl.BlockSpec((1, 1, step, w), lambda bb, r, i: (bb, r, i, c))

    def prev(c):
        return pl.BlockSpec((1, 1, half, w),
                            lambda bb, r, i: (bb, r, jnp.maximum(per_step * i - 1, 0), c))

    def nxt(c):
        return pl.BlockSpec((1, 1, half, w),
                            lambda bb, r, i: (bb, r, jnp.minimum(per_step * (i + 1), n_half_blocks - 1), c))

    return pl.pallas_call(
        functools.partial(_dilated_kernel, class_len=class_len),
        grid=(b, dil, class_len // step),
        in_specs=[cur(cq), prev(ck), cur(ck), nxt(ck), prev(cv), cur(cv), nxt(cv)],
        out_specs=[pl.BlockSpec((1, 1, step, w), lambda bb, r, i: (bb, r, i, 0)),
                   pl.BlockSpec((1, 1, step, HEAD_DIM), lambda bb, r, i: (bb, r, i, 0))],
        out_shape=[jax.ShapeDtypeStruct((b, dil, class_len, w), BF16),
                   jax.ShapeDtypeStruct((b, dil, class_len, HEAD_DIM), F32)],
        compiler_params=_params("parallel", "parallel", "arbitrary"),
        name=f"dilated_attention_d{dil}",
    )(qkv, qkv, qkv, qkv, qkv, qkv, qkv)


def _dil_merge_kernel(o0_ref, l0_ref, o1_ref, l1_ref, o2_ref, l2_ref, out_ref,
                      so1, sl1, so2, sl2):
    for src_o, src_l, dst_o, dst_l in ((o1_ref, l1_ref, so1, sl1), (o2_ref, l2_ref, so2, sl2)):
        dil, n = src_o.shape[1], src_o.shape[2]
        for r in range(dil):
            dst_l[pl.ds(r, n, stride=dil), :] = src_l[0, r]
            for hh, sl in enumerate(_heads(DIL_GROUP_WIDTH)):
                dst_o[hh, pl.ds(r, n, stride=dil), :] = src_o[0, r, :, sl].astype(F32)
    a0, a1, a2 = l0_ref[0, 0], sl1[...], sl2[...]
    m = jnp.maximum(jnp.maximum(a0, a1), a2)
    e0, e1, e2 = jnp.exp2(a0 - m), jnp.exp2(a1 - m), jnp.exp2(a2 - m)
    inv = 1.0 / (e0 + e1 + e2)
    w0, w1, w2 = e0 * inv, e1 * inv, e2 * inv
    for hh, sl in enumerate(_heads(DIL_GROUP_WIDTH)):
        h1 = slice(hh, hh + 1)
        out = (w0[:, h1] * o0_ref[0, 0, :, sl].astype(F32) + w1[:, h1] * so1[hh]
               + w2[:, h1] * so2[hh])
        out_ref[0, :, sl] = out.astype(out_ref.dtype)


def dilated_merge(outs, tr=1024):
    (o0, l0), (o1, l1), (o2, l2) = outs
    b, _, s_len, w = o0.shape

    def spec(a):
        dil = a.shape[1]
        return pl.BlockSpec((1, dil, tr // dil, a.shape[3]), lambda bb, i: (bb, 0, i, 0))

    slab_o = pltpu.VMEM((w // HEAD_DIM, tr, HEAD_DIM), F32)
    slab_l = pltpu.VMEM((tr, HEAD_DIM), F32)
    return pl.pallas_call(
        _dil_merge_kernel,
        grid=(b, s_len // tr),
        in_specs=[spec(o0), spec(l0), spec(o1), spec(l1), spec(o2), spec(l2)],
        out_specs=pl.BlockSpec((1, tr, w), lambda bb, i: (bb, i, 0)),
        out_shape=jax.ShapeDtypeStruct((b, s_len, w), BF16),
        scratch_shapes=[slab_o, slab_l, slab_o, slab_l],
        compiler_params=_params("parallel", "arbitrary"),
        name="dilated_merge",
    )(o0, l0, o1, l1, o2, l2)


def _diff_kernel(q_ref, k_ref, v_ref, lq1, lk1, lq2, lk2, g_ref, o_ref,
                 s_scr, p_scr, l_scr, *, lam_init):
    lam = (jnp.exp(jnp.sum(lq1[...] * lk1[...], axis=-1, keepdims=True))
           - jnp.exp(jnp.sum(lq2[...] * lk2[...], axis=-1, keepdims=True)) + lam_init)
    th = q_ref.shape[1] // 2
    n_keys = k_ref.shape[1]

    def scores(half, c, buf):
        sl = slice(c * HEAD_DIM, (c + 1) * HEAD_DIM)
        s_scr[buf] = _qkt(q_ref[0, half * th:(half + 1) * th, sl], k_ref[0, :, sl])

    def softmax(buf):
        for r in range(th // BF16_ROWS):
            rows = slice(r * BF16_ROWS, (r + 1) * BF16_ROWS)
            m = jnp.max(s_scr[buf, rows, :], axis=-1, keepdims=True)
            l = jnp.zeros((BF16_ROWS, 1), F32)
            for c0 in range(0, n_keys, SOFTMAX_CHUNK):
                cols = slice(c0, c0 + SOFTMAX_CHUNK)
                p = jnp.exp2(s_scr[buf, rows, cols] - m)
                l = l + jnp.sum(p, axis=-1, keepdims=True)
                p_scr[buf, rows, cols] = p.astype(BF16)
            l_scr[buf, rows, :] = jnp.broadcast_to(l, (BF16_ROWS, HEAD_DIM))

    def values(buf):
        return jnp.dot(p_scr[buf], v_ref[0], preferred_element_type=F32) / l_scr[buf, :, :1]

    def finish(half, o1, o2):
        o = o1 - lam * o2
        o_ref[0, half * th:(half + 1) * th, :] = (
            _rms(o, g_ref[...]) * (1.0 - lam_init)).astype(o_ref.dtype)

    scores(0, 0, 0)
    scores(0, 1, 1)
    softmax(0)
    scores(1, 0, 0)
    softmax(1)
    o_a1 = values(0)
    scores(1, 1, 1)
    softmax(0)
    o_a2 = values(1)
    finish(0, o_a1, o_a2)
    softmax(1)
    o_b1 = values(0)
    o_b2 = values(1)
    finish(1, o_b1, o_b2)


def differential_attention(proj, lq1, lk1, lq2, lk2, subln, lam_init, tq=1024):
    b, s, _ = proj.shape
    vec = pl.BlockSpec((1, HEAD_DIM), lambda bb, h, i: (0, 0))
    cq, ck, cv = P_QB // DIFF_V_DIM, P_KB // DIFF_V_DIM, P_VB // DIFF_V_DIM
    return pl.pallas_call(
        functools.partial(_diff_kernel, lam_init=lam_init),
        grid=(b, DIFF_HEADS, s // tq),
        in_specs=[pl.BlockSpec((1, tq, DIFF_V_DIM), lambda bb, h, i: (bb, i, cq + h)),
                  pl.BlockSpec((1, s, DIFF_V_DIM), lambda bb, h, i: (bb, 0, ck + h)),
                  pl.BlockSpec((1, s, DIFF_V_DIM), lambda bb, h, i: (bb, 0, cv + h)),
                  vec, vec, vec, vec,
                  pl.BlockSpec((1, DIFF_V_DIM), lambda bb, h, i: (0, 0))],
        out_specs=pl.BlockSpec((1, tq, DIFF_V_DIM), lambda bb, h, i: (bb, i, h)),
        out_shape=jax.ShapeDtypeStruct((b, s, DIFF_V_WIDTH), BF16),
        scratch_shapes=[pltpu.VMEM((2, tq // 2, s), F32), pltpu.VMEM((2, tq // 2, s), BF16),
                        pltpu.VMEM((2, tq // 2, HEAD_DIM), F32)],
        compiler_params=_params("parallel", "parallel", "arbitrary"),
        name="differential_attention",
    )(proj, proj, proj, lq1.reshape(1, -1), lk1.reshape(1, -1), lq2.reshape(1, -1),
      lk2.reshape(1, -1), subln.reshape(1, -1))


def _gate_kernel(oa_ref, wa_ref, ob_ref, wb_ref, ga_ref, gb_ref, o_ref):
    ya = jnp.dot(oa_ref[...], wa_ref[...].astype(BF16), preferred_element_type=F32)
    yb = jnp.dot(ob_ref[...], wb_ref[...].astype(BF16), preferred_element_type=F32)
    o = (jax.nn.sigmoid(ga_ref[...].astype(F32)) * ya
         + jax.nn.sigmoid(gb_ref[...].astype(F32)) * yb)
    o_ref[...] = o.astype(o_ref.dtype)


def gated_merge(out_a, w_a, out_b, w_b, proj2d, tm=1024, tn=512):
    t = out_a.shape[0]
    n = w_a.shape[1]
    ca = P_GA // tn
    cb = (P_GA + n) // tn
    n_tiles = n // tn

    def col(i, j):
        return _serpentine(i, j, n_tiles)

    return pl.pallas_call(
        _gate_kernel,
        grid=(t // tm, n_tiles),
        in_specs=[pl.BlockSpec((tm, out_a.shape[1]), lambda i, j: (i, 0)),
                  pl.BlockSpec((w_a.shape[0], tn), lambda i, j: (0, col(i, j))),
                  pl.BlockSpec((tm, out_b.shape[1]), lambda i, j: (i, 0)),
                  pl.BlockSpec((w_b.shape[0], tn), lambda i, j: (0, col(i, j))),
                  pl.BlockSpec((tm, tn), lambda i, j: (i, ca + col(i, j))),
                  pl.BlockSpec((tm, tn), lambda i, j: (i, cb + col(i, j)))],
        out_specs=pl.BlockSpec((tm, tn), lambda i, j: (i, col(i, j))),
        out_shape=jax.ShapeDtypeStruct((t, n), BF16),
        compiler_params=_params("parallel", "arbitrary"),
        name="gated_merge",
    )(out_a, w_a, out_b, w_b, proj2d, proj2d)


def _resnorm_kernel(y_ref, x_ref, gp_ref, gn_ref, xo_ref, ho_ref):
    xn = x_ref[...] + _rms(y_ref[...].astype(F32), gp_ref[...])
    xo_ref[...] = xn
    ho_ref[...] = _rms(xn, gn_ref[...]).astype(ho_ref.dtype)


def _resnorm_last_kernel(y_ref, x_ref, gp_ref, xo_ref):
    xo_ref[...] = x_ref[...] + _rms(y_ref[...].astype(F32), gp_ref[...])


def residual_norm(y, x, g_post, g_next=None, tr=512):
    t, d = x.shape
    row = pl.BlockSpec((tr, d), lambda i: (i, 0))
    vec = pl.BlockSpec((1, d), lambda i: (0, 0))
    if g_next is None:
        return pl.pallas_call(
            _resnorm_last_kernel,
            grid=(t // tr,),
            in_specs=[row, row, vec],
            out_specs=row,
            out_shape=jax.ShapeDtypeStruct((t, d), F32),
            compiler_params=_params("parallel"),
            name="residual_norm_last",
        )(y, x, g_post.reshape(1, d))
    return pl.pallas_call(
        _resnorm_kernel,
        grid=(t // tr,),
        in_specs=[row, row, vec, vec],
        out_specs=[row, row],
        out_shape=[jax.ShapeDtypeStruct((t, d), F32), jax.ShapeDtypeStruct((t, d), BF16)],
        compiler_params=_params("parallel"),
        name="residual_norm",
    )(y, x, g_post.reshape(1, d), g_next.reshape(1, d))


def _mem_block_kernel(y_ref, x_ref, gp1_ref, gpre_ref, wq_ref, kv_ref, wo_ref, gp2_ref, gn_ref,
                      x2_ref, hn_ref, o_scr):
    x1 = x_ref[...] + _rms(y_ref[...].astype(F32), gp1_ref[...])
    h = _rms(x1, gpre_ref[...]).astype(BF16)
    q = (jnp.dot(h, wq_ref[...], preferred_element_type=F32) * Q_PRESCALE).astype(BF16)
    for hh, sl in enumerate(_heads(MEM_WIDTH)):
        slv = slice(MEM_WIDTH + hh * HEAD_DIM, MEM_WIDTH + (hh + 1) * HEAD_DIM)
        s = _qkt(q[:, sl], kv_ref[0, :, sl])
        p = jnp.exp2(s - jnp.max(s, axis=-1, keepdims=True))
        l = jnp.sum(p, axis=-1, keepdims=True)
        o = jnp.dot(p.astype(BF16), kv_ref[0, :, slv], preferred_element_type=F32) / l
        o_scr[:, sl] = o.astype(o_scr.dtype)
    y2 = jnp.dot(o_scr[...], wo_ref[...], preferred_element_type=F32)
    x2 = x1 + _rms(y2, gp2_ref[...])
    x2_ref[...] = x2
    hn_ref[...] = _rms(x2, gn_ref[...]).astype(hn_ref.dtype)


def memory_block(y, x, g_post1, g_pre, w_q, kv, w_o, g_post2, g_next, seq_len, tr=256):
    t, d = x.shape
    tiles_per_seq = seq_len // tr
    row = pl.BlockSpec((tr, d), lambda i: (i, 0))
    vec = pl.BlockSpec((1, d), lambda i: (0, 0))
    once = pl.Buffered(1)
    return pl.pallas_call(
        _mem_block_kernel,
        grid=(t // tr,),
        in_specs=[row, row, vec, vec,
                  pl.BlockSpec(w_q.shape, lambda i: (0, 0), pipeline_mode=once),
                  pl.BlockSpec((1,) + kv.shape[1:], lambda i: (i // tiles_per_seq, 0, 0)),
                  pl.BlockSpec(w_o.shape, lambda i: (0, 0), pipeline_mode=once),
                  vec, vec],
        out_specs=[row, row],
        out_shape=[jax.ShapeDtypeStruct((t, d), F32), jax.ShapeDtypeStruct((t, d), BF16)],
        scratch_shapes=[pltpu.VMEM((tr, MEM_WIDTH), BF16)],
        compiler_params=_params("arbitrary"),
        name="memory_block",
    )(y, x, g_post1.reshape(1, d), g_pre.reshape(1, d), w_q, kv, w_o,
      g_post2.reshape(1, d), g_next.reshape(1, d))


def kernel(x, mem, positions, norm_mix_pre, w_in, w_a, w_b, w_mix_out, norm_mix_post,
           lambda_q1, lambda_k1, lambda_q2, lambda_k2, diff_subln,
           norm_mem_pre, norm_mem_kv, w_mem_q, w_mem_kv, w_mem_o, norm_mem_post,
           norm_mlp_pre, w_mlp_up, w_mlp_down, norm_mlp_post):
    b, s, d = x.shape
    t = b * s
    depth = w_in.shape[0]
    m_len = mem.shape[1]
    assert all(window // (2 * dil) == Q_BLOCK // 2 for window, dil in DIL_CONFIGS)
    assert w_in.shape[2] == W_QB + 2 * DIFF_QK_WIDTH + DIFF_V_WIDTH + 2 * d
    assert positions.shape == (b, s) and mem.shape == (b, m_len, d)
    xt = x.reshape(t, d)
    memt = mem.reshape(b * m_len, d)
    h, cos, sin = rmsnorm_and_rope_tables(xt, norm_mix_pre[0], positions)
    for layer in range(depth):
        lam_init = 0.8 - 0.6 * float(np.exp(-0.3 * layer))

        w_in_b = w_in[layer]
        proj = in_projection(h, w_in_b, cos, sin)
        proj3 = proj.reshape(b, s, -1)
        tile = DIL_GROUP_WIDTH
        outs = [dilated_group(proj3.reshape(b, 1, s, -1), P_QA // tile, P_KA // tile, P_VA // tile)]
        for g, (_, dil) in enumerate(DIL_CONFIGS):
            if dil > 1:
                qkv = in_projection_dilated(h, w_in_b, cos, sin, g, dil, b)
                outs.append(dilated_group(qkv, 0, 1, 2))
        out_a = dilated_merge(outs).reshape(t, -1)
        out_b = differential_attention(proj3, lambda_q1[layer], lambda_k1[layer],
                                       lambda_q2[layer], lambda_k2[layer],
                                       diff_subln[layer], lam_init).reshape(t, -1)
        merged = gated_merge(out_a, w_a[layer], out_b, w_b[layer], proj)
        y = matmul(merged, w_mix_out[layer], BF16, 1024, 512, name="mix_out")

        mn = rmsnorm(memt, norm_mem_kv[layer])
        kv = matmul(mn, w_mem_kv[layer], BF16, b * m_len, 512, name="mem_kv")
        xt, h = memory_block(y, xt, norm_mix_post[layer], norm_mem_pre[layer],
                             w_mem_q[layer].astype(BF16), kv.reshape(b, m_len, -1),
                             w_mem_o[layer].astype(BF16), norm_mem_post[layer],
                             norm_mlp_pre[layer], s)

        u = matmul(h, w_mlp_up[layer], BF16, 2048, 512, epilogue="relu2", name="mlp_up")
        y = matmul_kgrid(u, w_mlp_down[layer], BF16, 1024, 1024, 2048, name="mlp_down")
        if layer + 1 < depth:
            xt, h = residual_norm(y, xt, norm_mlp_post[layer], norm_mix_pre[layer + 1])
        else:
            xt = residual_norm(y, xt, norm_mlp_post[layer])
    return xt.reshape(b, s, d)
```

```python
import functools

import numpy as np
import jax
import jax.numpy as jnp
from jax import lax
from jax.experimental import pallas as pl
from jax.experimental.pallas import tpu as pltpu

F32 = jnp.float32
BF16 = jnp.bfloat16

HEAD_DIM = 128
BF16_ROWS = 16
SOFTMAX_CHUNK = 256
DIL_CONFIGS = ((128, 1), (512, 4), (2048, 16))
N_DIL_GROUPS = 3
DIL_HEADS = 8
DIL_GROUP_WIDTH = DIL_HEADS * HEAD_DIM
DIL_WIDTH = N_DIL_GROUPS * DIL_GROUP_WIDTH
DIFF_HEADS = 8
DIFF_QK_WIDTH = DIFF_HEADS * 2 * HEAD_DIM
DIFF_V_DIM = 2 * HEAD_DIM
DIFF_V_WIDTH = DIFF_HEADS * DIFF_V_DIM
MEM_HEADS = 4
MEM_WIDTH = MEM_HEADS * HEAD_DIM
ROPE_THETA = 10000.0
Q_BLOCK = 128
MAX_DIL_STEP = 4 * Q_BLOCK
NORM_EPS = 1e-6
MASK_VALUE = -1e30
LOG2E = 1.4426950408889634
Q_PRESCALE = HEAD_DIM ** -0.5 * LOG2E

W_QA = 0
W_KA = W_QA + DIL_WIDTH
W_VA = W_KA + DIL_WIDTH
W_QB = W_VA + DIL_WIDTH

PROJ_TN = 512
P_QA = 0
P_KA = P_QA + DIL_GROUP_WIDTH
P_VA = P_KA + DIL_GROUP_WIDTH
P_QB = P_VA + DIL_GROUP_WIDTH
P_KB = P_QB + DIFF_QK_WIDTH
P_VB = P_KB + DIFF_QK_WIDTH
P_GA = P_VB + DIFF_V_WIDTH

V7X_VMEM_BYTES = 64 * 1024 * 1024
VMEM_LIMIT = V7X_VMEM_BYTES - 6 * 1024 * 1024
MAX_DOUBLE_BUFFERED_ROW_TILE = V7X_VMEM_BYTES // 8
INPUT_RING = 3


def _params(*sem):
    return pltpu.CompilerParams(dimension_semantics=sem, vmem_limit_bytes=VMEM_LIMIT)


def _rms(x, g):
    return x * lax.rsqrt(jnp.mean(x * x, axis=-1, keepdims=True) + NORM_EPS) * g


def _heads(width):
    return [slice(hh * HEAD_DIM, (hh + 1) * HEAD_DIM) for hh in range(width // HEAD_DIM)]


def _qkt(q, k):
    return lax.dot_general(q, k, (((1,), (1,)), ((), ())), preferred_element_type=F32)


def _ring_copy(hbm_ref, buf, sem, step):
    tr = buf.shape[1]
    slot = step % INPUT_RING
    rows = pl.ds(pl.multiple_of(step * tr, tr), tr)
    return pltpu.make_async_copy(hbm_ref.at[rows, :], buf.at[slot], sem.at[slot])


def _ring_advance(streams):
    s = pl.program_id(0)
    n = pl.num_programs(0)

    @pl.when(s == 0)
    def _():
        for k in range(INPUT_RING - 1):
            for stream in streams:
                _ring_copy(*stream, k).start()

    @pl.when(s + INPUT_RING - 1 < n)
    def _():
        for stream in streams:
            _ring_copy(*stream, s + INPUT_RING - 1).start()

    for stream in streams:
        _ring_copy(*stream, s).wait()
    return s % INPUT_RING


def _ring_scratch(tr, d, dtype):
    return [pltpu.VMEM((INPUT_RING, tr, d), dtype), pltpu.SemaphoreType.DMA((INPUT_RING,))]


def _row_operand_spec(tm, k, dtype):
    nbytes = tm * k * jnp.dtype(dtype).itemsize
    mode = pl.Buffered(1) if nbytes > MAX_DOUBLE_BUFFERED_ROW_TILE else None
    return pl.BlockSpec((tm, k), lambda i, j: (i, 0), pipeline_mode=mode)


def _serpentine(i, j, n):
    return jnp.where(i % 2 == 0, j, n - 1 - j)


def _rmsnorm_kernel(x_ref, g_ref, o_ref):
    o_ref[...] = _rms(x_ref[...], g_ref[...]).astype(o_ref.dtype)


def rmsnorm(x, g, tr=256):
    t, d = x.shape
    return pl.pallas_call(
        _rmsnorm_kernel,
        grid=(t // tr,),
        in_specs=[pl.BlockSpec((tr, d), lambda i: (i, 0)),
                  pl.BlockSpec((1, d), lambda i: (0, 0))],
        out_specs=pl.BlockSpec((tr, d), lambda i: (i, 0)),
        out_shape=jax.ShapeDtypeStruct((t, d), BF16),
        compiler_params=_params("parallel"),
        name="rmsnorm",
    )(x, g.reshape(1, d))


def _rmsnorm_rope_kernel(x_hbm, g_ref, pos_ref, inv_ref, sign_ref, o_ref, cos_ref, sin_ref,
                         x_buf, x_sem):
    slot = _ring_advance([(x_hbm, x_buf, x_sem)])
    o_ref[...] = _rms(x_buf[slot], g_ref[...]).astype(o_ref.dtype)
    ang = pos_ref[...] * inv_ref[...]
    cos_ref[...] = jnp.cos(ang)
    sin_ref[...] = jnp.sin(ang) * sign_ref[...]


def rmsnorm_and_rope_tables(x, g, positions, tr=512):
    t, d = x.shape
    half = HEAD_DIM // 2
    inv = ROPE_THETA ** (-jnp.arange(0, HEAD_DIM, 2, dtype=F32) / HEAD_DIM)
    inv2 = jnp.concatenate([inv, inv]).reshape(1, HEAD_DIM)
    sign = jnp.concatenate([-jnp.ones((half,), F32), jnp.ones((half,), F32)]).reshape(1, HEAD_DIM)
    pos = positions.astype(F32).reshape(t, 1)
    vec = pl.BlockSpec((1, HEAD_DIM), lambda i: (0, 0))
    table = pl.BlockSpec((tr, HEAD_DIM), lambda i: (i, 0))
    assert t // tr >= INPUT_RING
    return pl.pallas_call(
        _rmsnorm_rope_kernel,
        grid=(t // tr,),
        in_specs=[pl.BlockSpec(memory_space=pl.ANY),
                  pl.BlockSpec((1, d), lambda i: (0, 0)),
                  pl.BlockSpec((tr, 1), lambda i: (i, 0)), vec, vec],
        out_specs=[pl.BlockSpec((tr, d), lambda i: (i, 0)), table, table],
        out_shape=[jax.ShapeDtypeStruct((t, d), BF16),
                   jax.ShapeDtypeStruct((t, HEAD_DIM), F32),
                   jax.ShapeDtypeStruct((t, HEAD_DIM), F32)],
        scratch_shapes=_ring_scratch(tr, d, x.dtype),
        compiler_params=_params("arbitrary"),
        name="rmsnorm_rope_tables",
    )(x, g.reshape(1, d), pos, inv2, sign)


def _rope(xh, c, s):
    return xh * c + pltpu.roll(xh, HEAD_DIM // 2, 1) * s


def _inproj_kernel(h_ref, w_ref, cos_ref, sin_ref, o_ref):
    j = _serpentine(pl.program_id(0), pl.program_id(1), pl.num_programs(1))
    acc = jnp.dot(h_ref[...], w_ref[...].astype(BF16), preferred_element_type=F32)
    is_q = (j < P_KA // PROJ_TN) | ((j >= P_QB // PROJ_TN) & (j < P_KB // PROJ_TN))
    is_k = (((j >= P_KA // PROJ_TN) & (j < P_VA // PROJ_TN))
            | ((j >= P_KB // PROJ_TN) & (j < P_VB // PROJ_TN)))

    @pl.when(is_q | is_k)
    def _():
        f = jnp.where(is_q, Q_PRESCALE, 1.0).astype(F32)
        c = cos_ref[...] * f
        s = sin_ref[...] * f
        for sl in _heads(PROJ_TN):
            o_ref[:, sl] = _rope(acc[:, sl], c, s).astype(o_ref.dtype)

    @pl.when(jnp.logical_not(is_q | is_k))
    def _():
        o_ref[...] = acc.astype(o_ref.dtype)


def in_projection(h, w, cos, sin, tm=2048):
    t, k = h.shape
    tn = PROJ_TN
    n_out = w.shape[1] - 3 * (DIL_WIDTH - DIL_GROUP_WIDTH)
    per = DIL_GROUP_WIDTH // tn
    n_group0 = 3 * per
    stride_a = DIL_WIDTH // tn
    skip = W_QB // tn - n_group0

    n_tiles = n_out // tn

    def w_map(i, j):
        jc = _serpentine(i, j, n_tiles)
        return (0, jnp.where(jc < n_group0, (jc // per) * stride_a + jc % per, jc + skip))

    return pl.pallas_call(
        _inproj_kernel,
        grid=(t // tm, n_tiles),
        in_specs=[_row_operand_spec(tm, k, h.dtype),
                  pl.BlockSpec((k, tn), w_map),
                  pl.BlockSpec((tm, HEAD_DIM), lambda i, j: (i, 0)),
                  pl.BlockSpec((tm, HEAD_DIM), lambda i, j: (i, 0))],
        out_specs=pl.BlockSpec((tm, tn), lambda i, j: (i, _serpentine(i, j, n_tiles))),
        out_shape=jax.ShapeDtypeStruct((t, n_out), BF16),
        compiler_params=_params("parallel", "arbitrary"),
        name="in_projection",
    )(h, w, cos, sin)


def _inproj_dilated_kernel(h_ref, w_ref, cos_ref, sin_ref, o_ref, slab_ref, *, dil):
    j = _serpentine(pl.program_id(0), pl.program_id(1), pl.num_programs(1))
    seg = j // (DIL_GROUP_WIDTH // PROJ_TN)
    tm = h_ref.shape[0]
    acc = jnp.dot(h_ref[...], w_ref[...].astype(BF16), preferred_element_type=F32)
    f = jnp.where(seg == 0, Q_PRESCALE, 1.0).astype(F32)
    c = jnp.where(seg < 2, cos_ref[...] * f, 1.0)
    s = jnp.where(seg < 2, sin_ref[...] * f, 0.0)
    for hh, sl in enumerate(_heads(PROJ_TN)):
        slab_ref[hh] = _rope(acc[:, sl], c, s)
        for r in range(dil):
            o_ref[0, r, :, sl] = slab_ref[hh, pl.ds(r, tm // dil, stride=dil), :].astype(o_ref.dtype)


def in_projection_dilated(h, w, cos, sin, g, dil, batch, tm=1024):
    t, k = h.shape
    tn = PROJ_TN
    s_len = t // batch
    tiles_per_seq = s_len // tm
    per = DIL_GROUP_WIDTH // tn
    stride_a = DIL_WIDTH // tn
    n_tiles = 3 * per

    def w_map(i, j):
        jc = _serpentine(i, j, n_tiles)
        return (0, (jc // per) * stride_a + g * per + jc % per)

    return pl.pallas_call(
        functools.partial(_inproj_dilated_kernel, dil=dil),
        grid=(t // tm, n_tiles),
        in_specs=[pl.BlockSpec((tm, k), lambda i, j: (i, 0)),
                  pl.BlockSpec((k, tn), w_map),
                  pl.BlockSpec((tm, HEAD_DIM), lambda i, j: (i, 0)),
                  pl.BlockSpec((tm, HEAD_DIM), lambda i, j: (i, 0))],
        out_specs=pl.BlockSpec((1, dil, tm // dil, tn),
                               lambda i, j: (i // tiles_per_seq, 0, i % tiles_per_seq,
                                             _serpentine(i, j, n_tiles))),
        out_shape=jax.ShapeDtypeStruct((batch, dil, s_len // dil, 3 * DIL_GROUP_WIDTH), BF16),
        scratch_shapes=[pltpu.VMEM((tn // HEAD_DIM, tm, HEAD_DIM), F32)],
        compiler_params=_params("parallel", "arbitrary"),
        name=f"in_projection_d{dil}",
    )(h, w, cos, sin)


def _mm_kernel(a_ref, w_ref, o_ref, *, epilogue):
    acc = jnp.dot(a_ref[...], w_ref[...].astype(BF16), preferred_element_type=F32)
    if epilogue == "relu2":
        acc = jnp.square(jnp.maximum(acc, 0.0))
    o_ref[...] = acc.astype(o_ref.dtype)


def matmul(a, w, out_dtype, tm, tn, epilogue=None, name="matmul"):
    m, k = a.shape
    n = w.shape[1]
    n_tiles = n // tn
    return pl.pallas_call(
        functools.partial(_mm_kernel, epilogue=epilogue),
        grid=(m // tm, n_tiles),
        in_specs=[_row_operand_spec(tm, k, a.dtype),
                  pl.BlockSpec((k, tn), lambda i, j: (0, _serpentine(i, j, n_tiles)))],
        out_specs=pl.BlockSpec((tm, tn), lambda i, j: (i, _serpentine(i, j, n_tiles))),
        out_shape=jax.ShapeDtypeStruct((m, n), out_dtype),
        compiler_params=_params("parallel", "arbitrary"),
        name=name,
    )(a, w)


def _mm_kgrid_kernel(a_ref, w_ref, o_ref, acc_ref):
    kk = pl.program_id(2)

    @pl.when(kk == 0)
    def _():
        acc_ref[...] = jnp.zeros_like(acc_ref)

    acc_ref[...] += jnp.dot(a_ref[...], w_ref[...].astype(BF16), preferred_element_type=F32)

    @pl.when(kk == pl.num_programs(2) - 1)
    def _():
        o_ref[...] = acc_ref[...].astype(o_ref.dtype)


def matmul_kgrid(a, w, out_dtype, tm, tn, tk, name="matmul_kgrid"):
    m, k = a.shape
    n = w.shape[1]
    return pl.pallas_call(
        _mm_kgrid_kernel,
        grid=(m // tm, n // tn, k // tk),
        in_specs=[pl.BlockSpec((tm, tk), lambda i, j, kk: (i, kk)),
                  pl.BlockSpec((tk, tn), lambda i, j, kk: (kk, j))],
        out_specs=pl.BlockSpec((tm, tn), lambda i, j, kk: (i, j)),
        out_shape=jax.ShapeDtypeStruct((m, n), out_dtype),
        scratch_shapes=[pltpu.VMEM((tm, tn), F32)],
        compiler_params=_params("parallel", "parallel", "arbitrary"),
        name=name,
    )(a, w)


def _dilated_kernel(q_ref, kp_ref, kc_ref, kn_ref, vp_ref, vc_ref, vn_ref,
                    o_ref, lse_ref, *, class_len):
    i = pl.program_id(2)
    step = q_ref.shape[2]
    half = Q_BLOCK // 2
    nk = 2 * Q_BLOCK
    row = lax.broadcasted_iota(jnp.int32, (Q_BLOCK, nk), 0)
    col = lax.broadcasted_iota(jnp.int32, (Q_BLOCK, nk), 1)
    lane = lax.broadcasted_iota(jnp.int32, (Q_BLOCK, HEAD_DIM), 1)
    rel = col - half - row
    in_band = (rel >= -half) & (rel <= half)

    def window(prev_ref, cur_ref, next_ref, q0, sl):
        parts = []
        if q0 == 0:
            parts.append(prev_ref[0, 0, :, sl])
        lo, hi = max(q0 - half, 0), min(q0 + Q_BLOCK + half, step)
        parts.append(cur_ref[0, 0, lo:hi, sl])
        if q0 + Q_BLOCK == step:
            parts.append(next_ref[0, 0, :, sl])
        return jnp.concatenate(parts, axis=0)

    for sb in range(step // Q_BLOCK):
        q0 = sb * Q_BLOCK
        kpos = i * step + q0 - half + col
        valid = in_band & (kpos >= 0) & (kpos < class_len)
        lse_tile = jnp.zeros((Q_BLOCK, HEAD_DIM), F32)
        for hh, sl in enumerate(_heads(DIL_GROUP_WIDTH)):
            q = q_ref[0, 0, q0:q0 + Q_BLOCK, sl]
            k, v = (window(kp_ref, kc_ref, kn_ref, q0, sl), window(vp_ref, vc_ref, vn_ref, q0, sl))
            s = jnp.where(valid, _qkt(q, k), MASK_VALUE)
            m = jnp.max(s, axis=-1, keepdims=True)
            p = jnp.exp2(s - m)
            l = jnp.sum(p, axis=-1, keepdims=True)
            o = jnp.dot(p.astype(BF16), v, preferred_element_type=F32) / l
            o_ref[0, 0, q0:q0 + Q_BLOCK, sl] = o.astype(o_ref.dtype)
            lse_tile = jnp.where(lane == hh, m + jnp.log2(l), lse_tile)
        lse_ref[0, 0, q0:q0 + Q_BLOCK, :] = lse_tile


def dilated_group(qkv, cq, ck, cv):
    b, dil, class_len, _ = qkv.shape
    half = Q_BLOCK // 2
    step = min(class_len, MAX_DIL_STEP)
    per_step = step // half
    n_half_blocks = class_len // half
    w = DIL_GROUP_WIDTH

    def cur(c):
        return pl.BlockSpec((1, 1, step, w), lambda bb, r, i: (bb, r, i, c))

    def prev(c):
        return pl.BlockSpec((1, 1, half, w),
                            lambda bb, r, i: (bb, r, jnp.maximum(per_step * i - 1, 0), c))

    def nxt(c):
        return pl.BlockSpec((1, 1, half, w),
                            lambda bb, r, i: (bb, r, jnp.minimum(per_step * (i + 1), n_half_blocks - 1), c))

    return pl.pallas_call(
        functools.partial(_dilated_kernel, class_len=class_len),
        grid=(b, dil, class_len // step),
        in_specs=[cur(cq), prev(ck), cur(ck), nxt(ck), prev(cv), cur(cv), nxt(cv)],
        out_specs=[pl.BlockSpec((1, 1, step, w), lambda bb, r, i: (bb, r, i, 0)),
                   pl.BlockSpec((1, 1, step, HEAD_DIM), lambda bb, r, i: (bb, r, i, 0))],
        out_shape=[jax.ShapeDtypeStruct((b, dil, class_len, w), BF16),
                   jax.ShapeDtypeStruct((b, dil, class_len, HEAD_DIM), F32)],
        compiler_params=_params("parallel", "parallel", "arbitrary"),
        name=f"dilated_attention_d{dil}",
    )(qkv, qkv, qkv, qkv, qkv, qkv, qkv)


def _dil_merge_kernel(o0_ref, l0_ref, o1_ref, l1_ref, o2_ref, l2_ref, out_ref,
                      so1, sl1, so2, sl2):
    for src_o, src_l, dst_o, dst_l in ((o1_ref, l1_ref, so1, sl1), (o2_ref, l2_ref, so2, sl2)):
        dil, n = src_o.shape[1], src_o.shape[2]
        for r in range(dil):
            dst_l[pl.ds(r, n, stride=dil), :] = src_l[0, r]
            for hh, sl in enumerate(_heads(DIL_GROUP_WIDTH)):
                dst_o[hh, pl.ds(r, n, stride=dil), :] = src_o[0, r, :, sl].astype(F32)
    a0, a1, a2 = l0_ref[0, 0], sl1[...], sl2[...]
    m = jnp.maximum(jnp.maximum(a0, a1), a2)
    e0, e1, e2 = jnp.exp2(a0 - m), jnp.exp2(a1 - m), jnp.exp2(a2 - m)
    inv = 1.0 / (e0 + e1 + e2)
    w0, w1, w2 = e0 * inv, e1 * inv, e2 * inv
    for hh, sl in enumerate(_heads(DIL_GROUP_WIDTH)):
        h1 = slice(hh, hh + 1)
        out = (w0[:, h1] * o0_ref[0, 0, :, sl].astype(F32) + w1[:, h1] * so1[hh]
               + w2[:, h1] * so2[hh])
        out_ref[0, :, sl] = out.astype(out_ref.dtype)


def dilated_merge(outs, tr=1024):
    (o0, l0), (o1, l1), (o2, l2) = outs
    b, _, s_len, w = o0.shape

    def spec(a):
        dil = a.shape[1]
        return pl.BlockSpec((1, dil, tr // dil, a.shape[3]), lambda bb, i: (bb, 0, i, 0))

    slab_o = pltpu.VMEM((w // HEAD_DIM, tr, HEAD_DIM), F32)
    slab_l = pltpu.VMEM((tr, HEAD_DIM), F32)
    return pl.pallas_call(
        _dil_merge_kernel,
        grid=(b, s_len // tr),
        in_specs=[spec(o0), spec(l0), spec(o1), spec(l1), spec(o2), spec(l2)],
        out_specs=pl.BlockSpec((1, tr, w), lambda bb, i: (bb, i, 0)),
        out_shape=jax.ShapeDtypeStruct((b, s_len, w), BF16),
        scratch_shapes=[slab_o, slab_l, slab_o, slab_l],
        compiler_params=_params("parallel", "arbitrary"),
        name="dilated_merge",
    )(o0, l0, o1, l1, o2, l2)


def _diff_kernel(q_ref, k_ref, v_ref, lq1, lk1, lq2, lk2, g_ref, o_ref,
                 s_scr, p_scr, l_scr, *, lam_init):
    lam = (jnp.exp(jnp.sum(lq1[...] * lk1[...], axis=-1, keepdims=True))
           - jnp.exp(jnp.sum(lq2[...] * lk2[...], axis=-1, keepdims=True)) + lam_init)
    th = q_ref.shape[1] // 2
    n_keys = k_ref.shape[1]

    def scores(half, c, buf):
        sl = slice(c * HEAD_DIM, (c + 1) * HEAD_DIM)
        s_scr[buf] = _qkt(q_ref[0, half * th:(half + 1) * th, sl], k_ref[0, :, sl])

    def softmax(buf):
        for r in range(th // BF16_ROWS):
            rows = slice(r * BF16_ROWS, (r + 1) * BF16_ROWS)
            m = jnp.max(s_scr[buf, rows, :], axis=-1, keepdims=True)
            l = jnp.zeros((BF16_ROWS, 1), F32)
            for c0 in range(0, n_keys, SOFTMAX_CHUNK):
                cols = slice(c0, c0 + SOFTMAX_CHUNK)
                p = jnp.exp2(s_scr[buf, rows, cols] - m)
                l = l + jnp.sum(p, axis=-1, keepdims=True)
                p_scr[buf, rows, cols] = p.astype(BF16)
            l_scr[buf, rows, :] = jnp.broadcast_to(l, (BF16_ROWS, HEAD_DIM))

    def values(buf):
        return jnp.dot(p_scr[buf], v_ref[0], preferred_element_type=F32) / l_scr[buf, :, :1]

    def finish(half, o1, o2):
        o = o1 - lam * o2
        o_ref[0, half * th:(half + 1) * th, :] = (
            _rms(o, g_ref[...]) * (1.0 - lam_init)).astype(o_ref.dtype)

    scores(0, 0, 0)
    scores(0, 1, 1)
    softmax(0)
    scores(1, 0, 0)
    softmax(1)
    o_a1 = values(0)
    scores(1, 1, 1)
    softmax(0)
    o_a2 = values(1)
    finish(0, o_a1, o_a2)
    softmax(1)
    o_b1 = values(0)
    o_b2 = values(1)
    finish(1, o_b1, o_b2)


def differential_attention(proj, lq1, lk1, lq2, lk2, subln, lam_init, tq=1024):
    b, s, _ = proj.shape
    vec = pl.BlockSpec((1, HEAD_DIM), lambda bb, h, i: (0, 0))
    cq, ck, cv = P_QB // DIFF_V_DIM, P_KB // DIFF_V_DIM, P_VB // DIFF_V_DIM
    return pl.pallas_call(
        functools.partial(_diff_kernel, lam_init=lam_init),
        grid=(b, DIFF_HEADS, s // tq),
        in_specs=[pl.BlockSpec((1, tq, DIFF_V_DIM), lambda bb, h, i: (bb, i, cq + h)),
                  pl.BlockSpec((1, s, DIFF_V_DIM), lambda bb, h, i: (bb, 0, ck + h)),
                  pl.BlockSpec((1, s, DIFF_V_DIM), lambda bb, h, i: (bb, 0, cv + h)),
                  vec, vec, vec, vec,
                  pl.BlockSpec((1, DIFF_V_DIM), lambda bb, h, i: (0, 0))],
        out_specs=pl.BlockSpec((1, tq, DIFF_V_DIM), lambda bb, h, i: (bb, i, h)),
        out_shape=jax.ShapeDtypeStruct((b, s, DIFF_V_WIDTH), BF16),
        scratch_shapes=[pltpu.VMEM((2, tq // 2, s), F32), pltpu.VMEM((2, tq // 2, s), BF16),
                        pltpu.VMEM((2, tq // 2, HEAD_DIM), F32)],
        compiler_params=_params("parallel", "parallel", "arbitrary"),
        name="differential_attention",
    )(proj, proj, proj, lq1.reshape(1, -1), lk1.reshape(1, -1), lq2.reshape(1, -1),
      lk2.reshape(1, -1), subln.reshape(1, -1))


def _gate_kernel(oa_ref, wa_ref, ob_ref, wb_ref, ga_ref, gb_ref, o_ref):
    ya = jnp.dot(oa_ref[...], wa_ref[...].astype(BF16), preferred_element_type=F32)
    yb = jnp.dot(ob_ref[...], wb_ref[...].astype(BF16), preferred_element_type=F32)
    o = (jax.nn.sigmoid(ga_ref[...].astype(F32)) * ya
         + jax.nn.sigmoid(gb_ref[...].astype(F32)) * yb)
    o_ref[...] = o.astype(o_ref.dtype)


def gated_merge(out_a, w_a, out_b, w_b, proj2d, tm=1024, tn=512):
    t = out_a.shape[0]
    n = w_a.shape[1]
    ca = P_GA // tn
    cb = (P_GA + n) // tn
    n_tiles = n // tn

    def col(i, j):
        return _serpentine(i, j, n_tiles)

    return pl.pallas_call(
        _gate_kernel,
        grid=(t // tm, n_tiles),
        in_specs=[pl.BlockSpec((tm, out_a.shape[1]), lambda i, j: (i, 0)),
                  pl.BlockSpec((w_a.shape[0], tn), lambda i, j: (0, col(i, j))),
                  pl.BlockSpec((tm, out_b.shape[1]), lambda i, j: (i, 0)),
                  pl.BlockSpec((w_b.shape[0], tn), lambda i, j: (0, col(i, j))),
                  pl.BlockSpec((tm, tn), lambda i, j: (i, ca + col(i, j))),
                  pl.BlockSpec((tm, tn), lambda i, j: (i, cb + col(i, j)))],
        out_specs=pl.BlockSpec((tm, tn), lambda i, j: (i, col(i, j))),
        out_shape=jax.ShapeDtypeStruct((t, n), BF16),
        compiler_params=_params("parallel", "arbitrary"),
        name="gated_merge",
    )(out_a, w_a, out_b, w_b, proj2d, proj2d)


def _resnorm_kernel(y_ref, x_ref, gp_ref, gn_ref, xo_ref, ho_ref):
    xn = x_ref[...] + _rms(y_ref[...].astype(F32), gp_ref[...])
    xo_ref[...] = xn
    ho_ref[...] = _rms(xn, gn_ref[...]).astype(ho_ref.dtype)


def _resnorm_last_kernel(y_hbm, x_hbm, gp_ref, xo_ref, y_buf, y_sem, x_buf, x_sem):
    slot = _ring_advance([(y_hbm, y_buf, y_sem), (x_hbm, x_buf, x_sem)])
    xo_ref[...] = x_buf[slot] + _rms(y_buf[slot].astype(F32), gp_ref[...])


def residual_norm(y, x, g_post, g_next=None, tr=256):
    t, d = x.shape
    row = pl.BlockSpec((tr, d), lambda i: (i, 0))
    vec = pl.BlockSpec((1, d), lambda i: (0, 0))
    if g_next is None:
        assert t // tr >= INPUT_RING
        streamed = pl.BlockSpec(memory_space=pl.ANY)
        return pl.pallas_call(
            _resnorm_last_kernel,
            grid=(t // tr,),
            in_specs=[streamed, streamed, vec],
            out_specs=row,
            out_shape=jax.ShapeDtypeStruct((t, d), F32),
            scratch_shapes=_ring_scratch(tr, d, y.dtype) + _ring_scratch(tr, d, x.dtype),
            compiler_params=_params("arbitrary"),
            name="residual_norm_last",
        )(y, x, g_post.reshape(1, d))
    return pl.pallas_call(
        _resnorm_kernel,
        grid=(t // tr,),
        in_specs=[row, row, vec, vec],
        out_specs=[row, row],
        out_shape=[jax.ShapeDtypeStruct((t, d), F32), jax.ShapeDtypeStruct((t, d), BF16)],
        compiler_params=_params("parallel"),
        name="residual_norm",
    )(y, x, g_post.reshape(1, d), g_next.reshape(1, d))


def _mem_block_kernel(y_ref, x_ref, gp1_ref, gpre_ref, wq_ref, kv_ref, wo_ref, gp2_ref, gn_ref,
                      x2_ref, hn_ref, o_scr):
    x1 = x_ref[...] + _rms(y_ref[...].astype(F32), gp1_ref[...])
    h = _rms(x1, gpre_ref[...]).astype(BF16)
    q = (jnp.dot(h, wq_ref[...], preferred_element_type=F32) * Q_PRESCALE).astype(BF16)
    for hh, sl in enumerate(_heads(MEM_WIDTH)):
        slv = slice(MEM_WIDTH + hh * HEAD_DIM, MEM_WIDTH + (hh + 1) * HEAD_DIM)
        s = _qkt(q[:, sl], kv_ref[0, :, sl])
        p = jnp.exp2(s - jnp.max(s, axis=-1, keepdims=True))
        l = jnp.sum(p, axis=-1, keepdims=True)
        o = jnp.dot(p.astype(BF16), kv_ref[0, :, slv], preferred_element_type=F32) / l
        o_scr[:, sl] = o.astype(o_scr.dtype)
    y2 = jnp.dot(o_scr[...], wo_ref[...], preferred_element_type=F32)
    x2 = x1 + _rms(y2, gp2_ref[...])
    x2_ref[...] = x2
    hn_ref[...] = _rms(x2, gn_ref[...]).astype(hn_ref.dtype)


def memory_block(y, x, g_post1, g_pre, w_q, kv, w_o, g_post2, g_next, seq_len, tr=256):
    t, d = x.shape
    tiles_per_seq = seq_len // tr
    row = pl.BlockSpec((tr, d), lambda i: (i, 0))
    vec = pl.BlockSpec((1, d), lambda i: (0, 0))
    once = pl.Buffered(1)
    return pl.pallas_call(
        _mem_block_kernel,
        grid=(t // tr,),
        in_specs=[row, row, vec, vec,
                  pl.BlockSpec(w_q.shape, lambda i: (0, 0), pipeline_mode=once),
                  pl.BlockSpec((1,) + kv.shape[1:], lambda i: (i // tiles_per_seq, 0, 0)),
                  pl.BlockSpec(w_o.shape, lambda i: (0, 0), pipeline_mode=once),
                  vec, vec],
        out_specs=[row, row],
        out_shape=[jax.ShapeDtypeStruct((t, d), F32), jax.ShapeDtypeStruct((t, d), BF16)],
        scratch_shapes=[pltpu.VMEM((tr, MEM_WIDTH), BF16)],
        compiler_params=_params("arbitrary"),
        name="memory_block",
    )(y, x, g_post1.reshape(1, d), g_pre.reshape(1, d), w_q, kv, w_o,
      g_post2.reshape(1, d), g_next.reshape(1, d))


def kernel(x, mem, positions, norm_mix_pre, w_in, w_a, w_b, w_mix_out, norm_mix_post,
           lambda_q1, lambda_k1, lambda_q2, lambda_k2, diff_subln,
           norm_mem_pre, norm_mem_kv, w_mem_q, w_mem_kv, w_mem_o, norm_mem_post,
           norm_mlp_pre, w_mlp_up, w_mlp_down, norm_mlp_post):
    b, s, d = x.shape
    t = b * s
    depth = w_in.shape[0]
    m_len = mem.shape[1]
    assert all(window // (2 * dil) == Q_BLOCK // 2 for window, dil in DIL_CONFIGS)
    assert w_in.shape[2] == W_QB + 2 * DIFF_QK_WIDTH + DIFF_V_WIDTH + 2 * d
    assert positions.shape == (b, s) and mem.shape == (b, m_len, d)
    xt = x.reshape(t, d)
    memt = mem.reshape(b * m_len, d)
    h, cos, sin = rmsnorm_and_rope_tables(xt, norm_mix_pre[0], positions)
    for layer in range(depth):
        lam_init = 0.8 - 0.6 * float(np.exp(-0.3 * layer))

        w_in_b = w_in[layer]
        proj = in_projection(h, w_in_b, cos, sin)
        proj3 = proj.reshape(b, s, -1)
        tile = DIL_GROUP_WIDTH
        outs = [dilated_group(proj3.reshape(b, 1, s, -1), P_QA // tile, P_KA // tile, P_VA // tile)]
        for g, (_, dil) in enumerate(DIL_CONFIGS):
            if dil > 1:
                qkv = in_projection_dilated(h, w_in_b, cos, sin, g, dil, b)
                outs.append(dilated_group(qkv, 0, 1, 2))
        out_a = dilated_merge(outs).reshape(t, -1)
        out_b = differential_attention(proj3, lambda_q1[layer], lambda_k1[layer],
                                       lambda_q2[layer], lambda_k2[layer],
                                       diff_subln[layer], lam_init).reshape(t, -1)
        merged = gated_merge(out_a, w_a[layer], out_b, w_b[layer], proj)
        y = matmul(merged, w_mix_out[layer], BF16, 1024, 512, name="mix_out")

        mn = rmsnorm(memt, norm_mem_kv[layer])
        kv = matmul(mn, w_mem_kv[layer], BF16, b * m_len, 512, name="mem_kv")
        xt, h = memory_block(y, xt, norm_mix_post[layer], norm_mem_pre[layer],
                             w_mem_q[layer].astype(BF16), kv.reshape(b, m_len, -1),
                             w_mem_o[layer].astype(BF16), norm_mem_post[layer],
                             norm_mlp_pre[layer], s)

        u = matmul(h, w_mlp_up[layer], BF16, 2048, 512, epilogue="relu2", name="mlp_up")
        y = matmul_kgrid(u, w_mlp_down[layer], BF16, 1024, 1024, 2048, name="mlp_down")
        if layer + 1 < depth:
            xt, h = residual_norm(y, xt, norm_mlp_post[layer], norm_mix_pre[layer + 1])
        else:
            xt = residual_norm(y, xt, norm_mlp_post[layer])
    return xt.reshape(b, s, d)
```

```python
import functools

import numpy as np
import jax
import jax.numpy as jnp
from jax import lax
from jax.experimental import pallas as pl
from jax.experimental.pallas import tpu as pltpu

F32 = jnp.float32
BF16 = jnp.bfloat16

HEAD_DIM = 128
BF16_ROWS = 16
SOFTMAX_CHUNK = 256
DIL_CONFIGS = ((128, 1), (512, 4), (2048, 16))
N_DIL_GROUPS = 3
DIL_HEADS = 8
DIL_GROUP_WIDTH = DIL_HEADS * HEAD_DIM
DIL_WIDTH = N_DIL_GROUPS * DIL_GROUP_WIDTH
DIFF_HEADS = 8
DIFF_QK_WIDTH = DIFF_HEADS * 2 * HEAD_DIM
DIFF_V_DIM = 2 * HEAD_DIM
DIFF_V_WIDTH = DIFF_HEADS * DIFF_V_DIM
MEM_HEADS = 4
MEM_WIDTH = MEM_HEADS * HEAD_DIM
ROPE_THETA = 10000.0
Q_BLOCK = 128
MAX_DIL_STEP = 4 * Q_BLOCK
NORM_EPS = 1e-6
MASK_VALUE = -1e30
LOG2E = 1.4426950408889634
Q_PRESCALE = HEAD_DIM ** -0.5 * LOG2E

W_QA = 0
W_KA = W_QA + DIL_WIDTH
W_VA = W_KA + DIL_WIDTH
W_QB = W_VA + DIL_WIDTH

PROJ_TN = 512
P_QA = 0
P_KA = P_QA + DIL_GROUP_WIDTH
P_VA = P_KA + DIL_GROUP_WIDTH
P_QB = P_VA + DIL_GROUP_WIDTH
P_KB = P_QB + DIFF_QK_WIDTH
P_VB = P_KB + DIFF_QK_WIDTH
P_GA = P_VB + DIFF_V_WIDTH

V7X_VMEM_BYTES = 64 * 1024 * 1024
VMEM_LIMIT = V7X_VMEM_BYTES - 6 * 1024 * 1024
MAX_DOUBLE_BUFFERED_ROW_TILE = V7X_VMEM_BYTES // 8
INPUT_RING = 3


def _params(*sem):
    return pltpu.CompilerParams(dimension_semantics=sem, vmem_limit_bytes=VMEM_LIMIT)


def _rms(x, g):
    return x * lax.rsqrt(jnp.mean(x * x, axis=-1, keepdims=True) + NORM_EPS) * g


def _heads(width):
    return [slice(hh * HEAD_DIM, (hh + 1) * HEAD_DIM) for hh in range(width // HEAD_DIM)]


def _qkt(q, k):
    return lax.dot_general(q, k, (((1,), (1,)), ((), ())), preferred_element_type=F32)


def _ring_copy(hbm_ref, buf, sem, step):
    tr = buf.shape[1]
    slot = step % INPUT_RING
    rows = pl.ds(pl.multiple_of(step * tr, tr), tr)
    return pltpu.make_async_copy(hbm_ref.at[rows, :], buf.at[slot], sem.at[slot])


def _ring_advance(streams):
    s = pl.program_id(0)
    n = pl.num_programs(0)

    @pl.when(s == 0)
    def _():
        for k in range(INPUT_RING - 1):
            for stream in streams:
                _ring_copy(*stream, k).start()

    @pl.when(s + INPUT_RING - 1 < n)
    def _():
        for stream in streams:
            _ring_copy(*stream, s + INPUT_RING - 1).start()

    for stream in streams:
        _ring_copy(*stream, s).wait()
    return s % INPUT_RING


def _ring_scratch(tr, d, dtype):
    return [pltpu.VMEM((INPUT_RING, tr, d), dtype), pltpu.SemaphoreType.DMA((INPUT_RING,))]


def _row_operand_spec(tm, k, dtype):
    nbytes = tm * k * jnp.dtype(dtype).itemsize
    mode = pl.Buffered(1) if nbytes > MAX_DOUBLE_BUFFERED_ROW_TILE else None
    return pl.BlockSpec((tm, k), lambda i, j: (i, 0), pipeline_mode=mode)


def _serpentine(i, j, n):
    return jnp.where(i % 2 == 0, j, n - 1 - j)


def _rmsnorm_kernel(x_ref, g_ref, o_ref):
    o_ref[...] = _rms(x_ref[...], g_ref[...]).astype(o_ref.dtype)


def rmsnorm(x, g, tr=256):
    t, d = x.shape
    return pl.pallas_call(
        _rmsnorm_kernel,
        grid=(t // tr,),
        in_specs=[pl.BlockSpec((tr, d), lambda i: (i, 0)),
                  pl.BlockSpec((1, d), lambda i: (0, 0))],
        out_specs=pl.BlockSpec((tr, d), lambda i: (i, 0)),
        out_shape=jax.ShapeDtypeStruct((t, d), BF16),
        compiler_params=_params("parallel"),
        name="rmsnorm",
    )(x, g.reshape(1, d))


def _rmsnorm_rope_kernel(x_hbm, g_ref, pos_ref, inv_ref, sign_ref, o_ref, cos_ref, sin_ref,
                         x_buf, x_sem):
    slot = _ring_advance([(x_hbm, x_buf, x_sem)])
    o_ref[...] = _rms(x_buf[slot], g_ref[...]).astype(o_ref.dtype)
    ang = pos_ref[...] * inv_ref[...]
    cos_ref[...] = jnp.cos(ang)
    sin_ref[...] = jnp.sin(ang) * sign_ref[...]


def rmsnorm_and_rope_tables(x, g, positions, tr=512):
    t, d = x.shape
    half = HEAD_DIM // 2
    inv = ROPE_THETA ** (-jnp.arange(0, HEAD_DIM, 2, dtype=F32) / HEAD_DIM)
    inv2 = jnp.concatenate([inv, inv]).reshape(1, HEAD_DIM)
    sign = jnp.concatenate([-jnp.ones((half,), F32), jnp.ones((half,), F32)]).reshape(1, HEAD_DIM)
    pos = positions.astype(F32).reshape(t, 1)
    vec = pl.BlockSpec((1, HEAD_DIM), lambda i: (0, 0))
    table = pl.BlockSpec((tr, HEAD_DIM), lambda i: (i, 0))
    assert t // tr >= INPUT_RING
    return pl.pallas_call(
        _rmsnorm_rope_kernel,
        grid=(t // tr,),
        in_specs=[pl.BlockSpec(memory_space=pl.ANY),
                  pl.BlockSpec((1, d), lambda i: (0, 0)),
                  pl.BlockSpec((tr, 1), lambda i: (i, 0)), vec, vec],
        out_specs=[pl.BlockSpec((tr, d), lambda i: (i, 0)), table, table],
        out_shape=[jax.ShapeDtypeStruct((t, d), BF16),
                   jax.ShapeDtypeStruct((t, HEAD_DIM), F32),
                   jax.ShapeDtypeStruct((t, HEAD_DIM), F32)],
        scratch_shapes=_ring_scratch(tr, d, x.dtype),
        compiler_params=_params("arbitrary"),
        name="rmsnorm_rope_tables",
    )(x, g.reshape(1, d), pos, inv2, sign)


def _rope(xh, c, s):
    return xh * c + pltpu.roll(xh, HEAD_DIM // 2, 1) * s


def _inproj_kernel(h_ref, w_ref, cos_ref, sin_ref, o_ref):
    j = _serpentine(pl.program_id(0), pl.program_id(1), pl.num_programs(1))
    acc = jnp.dot(h_ref[...], w_ref[...].astype(BF16), preferred_element_type=F32)
    is_q = (j < P_KA // PROJ_TN) | ((j >= P_QB // PROJ_TN) & (j < P_KB // PROJ_TN))
    is_k = (((j >= P_KA // PROJ_TN) & (j < P_VA // PROJ_TN))
            | ((j >= P_KB // PROJ_TN) & (j < P_VB // PROJ_TN)))

    @pl.when(is_q | is_k)
    def _():
        f = jnp.where(is_q, Q_PRESCALE, 1.0).astype(F32)
        c = cos_ref[...] * f
        s = sin_ref[...] * f
        for sl in _heads(PROJ_TN):
            o_ref[:, sl] = _rope(acc[:, sl], c, s).astype(o_ref.dtype)

    @pl.when(jnp.logical_not(is_q | is_k))
    def _():
        o_ref[...] = acc.astype(o_ref.dtype)


def in_projection(h, w, cos, sin, tm=2048):
    t, k = h.shape
    tn = PROJ_TN
    n_out = w.shape[1] - 3 * (DIL_WIDTH - DIL_GROUP_WIDTH)
    per = DIL_GROUP_WIDTH // tn
    n_group0 = 3 * per
    stride_a = DIL_WIDTH // tn
    skip = W_QB // tn - n_group0

    n_tiles = n_out // tn

    def w_map(i, j):
        jc = _serpentine(i, j, n_tiles)
        return (0, jnp.where(jc < n_group0, (jc // per) * stride_a + jc % per, jc + skip))

    return pl.pallas_call(
        _inproj_kernel,
        grid=(t // tm, n_tiles),
        in_specs=[_row_operand_spec(tm, k, h.dtype),
                  pl.BlockSpec((k, tn), w_map),
                  pl.BlockSpec((tm, HEAD_DIM), lambda i, j: (i, 0)),
                  pl.BlockSpec((tm, HEAD_DIM), lambda i, j: (i, 0))],
        out_specs=pl.BlockSpec((tm, tn), lambda i, j: (i, _serpentine(i, j, n_tiles))),
        out_shape=jax.ShapeDtypeStruct((t, n_out), BF16),
        compiler_params=_params("parallel", "arbitrary"),
        name="in_projection",
    )(h, w, cos, sin)


def _inproj_dilated_kernel(h_ref, w_ref, cos_ref, sin_ref, o_ref, slab_ref, *, dil):
    j = _serpentine(pl.program_id(0), pl.program_id(1), pl.num_programs(1))
    seg = j // (DIL_GROUP_WIDTH // PROJ_TN)
    tm = h_ref.shape[0]
    acc = jnp.dot(h_ref[...], w_ref[...].astype(BF16), preferred_element_type=F32)
    f = jnp.where(seg == 0, Q_PRESCALE, 1.0).astype(F32)
    c = jnp.where(seg < 2, cos_ref[...] * f, 1.0)
    s = jnp.where(seg < 2, sin_ref[...] * f, 0.0)
    for hh, sl in enumerate(_heads(PROJ_TN)):
        slab_ref[hh] = _rope(acc[:, sl], c, s)
        for r in range(dil):
            o_ref[0, r, :, sl] = slab_ref[hh, pl.ds(r, tm // dil, stride=dil), :].astype(o_ref.dtype)


def in_projection_dilated(h, w, cos, sin, g, dil, batch, tm=1024):
    t, k = h.shape
    tn = PROJ_TN
    s_len = t // batch
    tiles_per_seq = s_len // tm
    per = DIL_GROUP_WIDTH // tn
    stride_a = DIL_WIDTH // tn
    n_tiles = 3 * per

    def w_map(i, j):
        jc = _serpentine(i, j, n_tiles)
        return (0, (jc // per) * stride_a + g * per + jc % per)

    return pl.pallas_call(
        functools.partial(_inproj_dilated_kernel, dil=dil),
        grid=(t // tm, n_tiles),
        in_specs=[pl.BlockSpec((tm, k), lambda i, j: (i, 0)),
                  pl.BlockSpec((k, tn), w_map),
                  pl.BlockSpec((tm, HEAD_DIM), lambda i, j: (i, 0)),
                  pl.BlockSpec((tm, HEAD_DIM), lambda i, j: (i, 0))],
        out_specs=pl.BlockSpec((1, dil, tm // dil, tn),
                               lambda i, j: (i // tiles_per_seq, 0, i % tiles_per_seq,
                                             _serpentine(i, j, n_tiles))),
        out_shape=jax.ShapeDtypeStruct((batch, dil, s_len // dil, 3 * DIL_GROUP_WIDTH), BF16),
        scratch_shapes=[pltpu.VMEM((tn // HEAD_DIM, tm, HEAD_DIM), F32)],
        compiler_params=_params("parallel", "arbitrary"),
        name=f"in_projection_d{dil}",
    )(h, w, cos, sin)


def _mm_kernel(a_ref, w_ref, o_ref, *, epilogue):
    acc = jnp.dot(a_ref[...], w_ref[...].astype(BF16), preferred_element_type=F32)
    if epilogue == "relu2":
        acc = jnp.square(jnp.maximum(acc, 0.0))
    o_ref[...] = acc.astype(o_ref.dtype)


def matmul(a, w, out_dtype, tm, tn, epilogue=None, name="matmul"):
    m, k = a.shape
    n = w.shape[1]
    n_tiles = n // tn
    return pl.pallas_call(
        functools.partial(_mm_kernel, epilogue=epilogue),
        grid=(m // tm, n_tiles),
        in_specs=[_row_operand_spec(tm, k, a.dtype),
                  pl.BlockSpec((k, tn), lambda i, j: (0, _serpentine(i, j, n_tiles)))],
        out_specs=pl.BlockSpec((tm, tn), lambda i, j: (i, _serpentine(i, j, n_tiles))),
        out_shape=jax.ShapeDtypeStruct((m, n), out_dtype),
        compiler_params=_params("parallel", "arbitrary"),
        name=name,
    )(a, w)


def _mm_kgrid_kernel(a_ref, w_ref, o_ref, acc_ref):
    kk = pl.program_id(2)

    @pl.when(kk == 0)
    def _():
        acc_ref[...] = jnp.zeros_like(acc_ref)

    acc_ref[...] += jnp.dot(a_ref[...], w_ref[...].astype(BF16), preferred_element_type=F32)

    @pl.when(kk == pl.num_programs(2) - 1)
    def _():
        o_ref[...] = acc_ref[...].astype(o_ref.dtype)


def matmul_kgrid(a, w, out_dtype, tm, tn, tk, name="matmul_kgrid"):
    m, k = a.shape
    n = w.shape[1]
    return pl.pallas_call(
        _mm_kgrid_kernel,
        grid=(m // tm, n // tn, k // tk),
        in_specs=[pl.BlockSpec((tm, tk), lambda i, j, kk: (i, kk)),
                  pl.BlockSpec((tk, tn), lambda i, j, kk: (kk, j))],
        out_specs=pl.BlockSpec((tm, tn), lambda i, j, kk: (i, j)),
        out_shape=jax.ShapeDtypeStruct((m, n), out_dtype),
        scratch_shapes=[pltpu.VMEM((tm, tn), F32)],
        compiler_params=_params("parallel", "parallel", "arbitrary"),
        name=name,
    )(a, w)


def _dilated_kernel(q_ref, kp_ref, kc_ref, kn_ref, vp_ref, vc_ref, vn_ref,
                    o_ref, lse_ref, *, class_len):
    i = pl.program_id(2)
    n_cls, step = q_ref.shape[1], q_ref.shape[2]
    half = Q_BLOCK // 2
    nk = 2 * Q_BLOCK
    row = lax.broadcasted_iota(jnp.int32, (Q_BLOCK, nk), 0)
    col = lax.broadcasted_iota(jnp.int32, (Q_BLOCK, nk), 1)
    lane = lax.broadcasted_iota(jnp.int32, (Q_BLOCK, HEAD_DIM), 1)
    rel = col - half - row
    in_band = (rel >= -half) & (rel <= half)

    def window(prev_ref, cur_ref, next_ref, cls, q0, sl):
        parts = []
        if q0 == 0:
            parts.append(prev_ref[0, cls, :, sl])
        lo, hi = max(q0 - half, 0), min(q0 + Q_BLOCK + half, step)
        parts.append(cur_ref[0, cls, lo:hi, sl])
        if q0 + Q_BLOCK == step:
            parts.append(next_ref[0, cls, :, sl])
        return jnp.concatenate(parts, axis=0)

    for cls in range(n_cls):
        for sb in range(step // Q_BLOCK):
            q0 = sb * Q_BLOCK
            kpos = i * step + q0 - half + col
            valid = in_band & (kpos >= 0) & (kpos < class_len)
            lse_tile = jnp.zeros((Q_BLOCK, HEAD_DIM), F32)
            for hh, sl in enumerate(_heads(DIL_GROUP_WIDTH)):
                q = q_ref[0, cls, q0:q0 + Q_BLOCK, sl]
                k = window(kp_ref, kc_ref, kn_ref, cls, q0, sl)
                v = window(vp_ref, vc_ref, vn_ref, cls, q0, sl)
                s = jnp.where(valid, _qkt(q, k), MASK_VALUE)
                m = jnp.max(s, axis=-1, keepdims=True)
                p = jnp.exp2(s - m)
                l = jnp.sum(p, axis=-1, keepdims=True)
                o = jnp.dot(p.astype(BF16), v, preferred_element_type=F32) / l
                o_ref[0, cls, q0:q0 + Q_BLOCK, sl] = o.astype(o_ref.dtype)
                lse_tile = jnp.where(lane == hh, m + jnp.log2(l), lse_tile)
            lse_ref[0, cls, q0:q0 + Q_BLOCK, :] = lse_tile


def dilated_group(qkv, cq, ck, cv):
    b, dil, class_len, _ = qkv.shape
    half = Q_BLOCK // 2
    step = min(class_len, MAX_DIL_STEP)
    n_cls = min(dil, MAX_DIL_STEP // step)
    per_step = step // half
    n_half_blocks = class_len // half
    w = DIL_GROUP_WIDTH

    def cur(c):
        return pl.BlockSpec((1, n_cls, step, w), lambda bb, r, i: (bb, r, i, c))

    def prev(c):
        return pl.BlockSpec((1, n_cls, half, w),
                            lambda bb, r, i: (bb, r, jnp.maximum(per_step * i - 1, 0), c))

    def nxt(c):
        return pl.BlockSpec((1, n_cls, half, w),
                            lambda bb, r, i: (bb, r, jnp.minimum(per_step * (i + 1), n_half_blocks - 1), c))

    return pl.pallas_call(
        functools.partial(_dilated_kernel, class_len=class_len),
        grid=(b, dil // n_cls, class_len // step),
        in_specs=[cur(cq), prev(ck), cur(ck), nxt(ck), prev(cv), cur(cv), nxt(cv)],
        out_specs=[pl.BlockSpec((1, n_cls, step, w), lambda bb, r, i: (bb, r, i, 0)),
                   pl.BlockSpec((1, n_cls, step, HEAD_DIM), lambda bb, r, i: (bb, r, i, 0))],
        out_shape=[jax.ShapeDtypeStruct((b, dil, class_len, w), BF16),
                   jax.ShapeDtypeStruct((b, dil, class_len, HEAD_DIM), F32)],
        compiler_params=_params("parallel", "parallel", "arbitrary"),
        name=f"dilated_attention_d{dil}",
    )(qkv, qkv, qkv, qkv, qkv, qkv, qkv)


def _dil_merge_kernel(o0_ref, l0_ref, o1_ref, l1_ref, o2_ref, l2_ref, out_ref,
                      so1, sl1, so2, sl2):
    for src_o, src_l, dst_o, dst_l in ((o1_ref, l1_ref, so1, sl1), (o2_ref, l2_ref, so2, sl2)):
        dil, n = src_o.shape[1], src_o.shape[2]
        for r in range(dil):
            dst_l[pl.ds(r, n, stride=dil), :] = src_l[0, r]
            for hh, sl in enumerate(_heads(DIL_GROUP_WIDTH)):
                dst_o[hh, pl.ds(r, n, stride=dil), :] = src_o[0, r, :, sl].astype(F32)
    a0, a1, a2 = l0_ref[0, 0], sl1[...], sl2[...]
    m = jnp.maximum(jnp.maximum(a0, a1), a2)
    e0, e1, e2 = jnp.exp2(a0 - m), jnp.exp2(a1 - m), jnp.exp2(a2 - m)
    inv = 1.0 / (e0 + e1 + e2)
    w0, w1, w2 = e0 * inv, e1 * inv, e2 * inv
    for hh, sl in enumerate(_heads(DIL_GROUP_WIDTH)):
        h1 = slice(hh, hh + 1)
        out = (w0[:, h1] * o0_ref[0, 0, :, sl].astype(F32) + w1[:, h1] * so1[hh]
               + w2[:, h1] * so2[hh])
        out_ref[0, :, sl] = out.astype(out_ref.dtype)


def dilated_merge(outs, tr=1024):
    (o0, l0), (o1, l1), (o2, l2) = outs
    b, _, s_len, w = o0.shape

    def spec(a):
        dil = a.shape[1]
        return pl.BlockSpec((1, dil, tr // dil, a.shape[3]), lambda bb, i: (bb, 0, i, 0))

    slab_o = pltpu.VMEM((w // HEAD_DIM, tr, HEAD_DIM), F32)
    slab_l = pltpu.VMEM((tr, HEAD_DIM), F32)
    return pl.pallas_call(
        _dil_merge_kernel,
        grid=(b, s_len // tr),
        in_specs=[spec(o0), spec(l0), spec(o1), spec(l1), spec(o2), spec(l2)],
        out_specs=pl.BlockSpec((1, tr, w), lambda bb, i: (bb, i, 0)),
        out_shape=jax.ShapeDtypeStruct((b, s_len, w), BF16),
        scratch_shapes=[slab_o, slab_l, slab_o, slab_l],
        compiler_params=_params("parallel", "arbitrary"),
        name="dilated_merge",
    )(o0, l0, o1, l1, o2, l2)


def _diff_kernel(q_ref, k_ref, v_ref, lq1, lk1, lq2, lk2, g_ref, o_ref,
                 s_scr, p_scr, l_scr, *, lam_init):
    lam = (jnp.exp(jnp.sum(lq1[...] * lk1[...], axis=-1, keepdims=True))
           - jnp.exp(jnp.sum(lq2[...] * lk2[...], axis=-1, keepdims=True)) + lam_init)
    th = q_ref.shape[1] // 2
    n_keys = k_ref.shape[1]

    def scores(half, c, buf):
        sl = slice(c * HEAD_DIM, (c + 1) * HEAD_DIM)
        s_scr[buf] = _qkt(q_ref[0, half * th:(half + 1) * th, sl], k_ref[0, :, sl])

    def softmax(buf):
        for r in range(th // BF16_ROWS):
            rows = slice(r * BF16_ROWS, (r + 1) * BF16_ROWS)
            m = jnp.max(s_scr[buf, rows, :], axis=-1, keepdims=True)
            l = jnp.zeros((BF16_ROWS, 1), F32)
            for c0 in range(0, n_keys, SOFTMAX_CHUNK):
                cols = slice(c0, c0 + SOFTMAX_CHUNK)
                p = jnp.exp2(s_scr[buf, rows, cols] - m)
                l = l + jnp.sum(p, axis=-1, keepdims=True)
                p_scr[buf, rows, cols] = p.astype(BF16)
            l_scr[buf, rows, :] = jnp.broadcast_to(l, (BF16_ROWS, HEAD_DIM))

    def values(buf):
        return jnp.dot(p_scr[buf], v_ref[0], preferred_element_type=F32) / l_scr[buf, :, :1]

    def finish(half, o1, o2):
        o = o1 - lam * o2
        o_ref[0, half * th:(half + 1) * th, :] = (
            _rms(o, g_ref[...]) * (1.0 - lam_init)).astype(o_ref.dtype)

    scores(0, 0, 0)
    scores(0, 1, 1)
    softmax(0)
    scores(1, 0, 0)
    softmax(1)
    o_a1 = values(0)
    scores(1, 1, 1)
    softmax(0)
    o_a2 = values(1)
    finish(0, o_a1, o_a2)
    softmax(1)
    o_b1 = values(0)
    o_b2 = values(1)
    finish(1, o_b1, o_b2)


def differential_attention(proj, lq1, lk1, lq2, lk2, subln, lam_init, tq=1024):
    b, s, _ = proj.shape
    vec = pl.BlockSpec((1, HEAD_DIM), lambda bb, h, i: (0, 0))
    cq, ck, cv = P_QB // DIFF_V_DIM, P_KB // DIFF_V_DIM, P_VB // DIFF_V_DIM
    return pl.pallas_call(
        functools.partial(_diff_kernel, lam_init=lam_init),
        grid=(b, DIFF_HEADS, s // tq),
        in_specs=[pl.BlockSpec((1, tq, DIFF_V_DIM), lambda bb, h, i: (bb, i, cq + h)),
                  pl.BlockSpec((1, s, DIFF_V_DIM), lambda bb, h, i: (bb, 0, ck + h)),
                  pl.BlockSpec((1, s, DIFF_V_DIM), lambda bb, h, i: (bb, 0, cv + h)),
                  vec, vec, vec, vec,
                  pl.BlockSpec((1, DIFF_V_DIM), lambda bb, h, i: (0, 0))],
        out_specs=pl.BlockSpec((1, tq, DIFF_V_DIM), lambda bb, h, i: (bb, i, h)),
        out_shape=jax.ShapeDtypeStruct((b, s, DIFF_V_WIDTH), BF16),
        scratch_shapes=[pltpu.VMEM((2, tq // 2, s), F32), pltpu.VMEM((2, tq // 2, s), BF16),
                        pltpu.VMEM((2, tq // 2, HEAD_DIM), F32)],
        compiler_params=_params("parallel", "parallel", "arbitrary"),
        name="differential_attention",
    )(proj, proj, proj, lq1.reshape(1, -1), lk1.reshape(1, -1), lq2.reshape(1, -1),
      lk2.reshape(1, -1), subln.reshape(1, -1))


def _gate_kernel(oa_ref, wa_ref, ob_ref, wb_ref, ga_ref, gb_ref, o_ref):
    ya = jnp.dot(oa_ref[...], wa_ref[...].astype(BF16), preferred_element_type=F32)
    yb = jnp.dot(ob_ref[...], wb_ref[...].astype(BF16), preferred_element_type=F32)
    o = (jax.nn.sigmoid(ga_ref[...].astype(F32)) * ya
         + jax.nn.sigmoid(gb_ref[...].astype(F32)) * yb)
    o_ref[...] = o.astype(o_ref.dtype)


def gated_merge(out_a, w_a, out_b, w_b, proj2d, tm=1024, tn=512):
    t = out_a.shape[0]
    n = w_a.shape[1]
    ca = P_GA // tn
    cb = (P_GA + n) // tn
    n_tiles = n // tn

    def col(i, j):
        return _serpentine(i, j, n_tiles)

    return pl.pallas_call(
        _gate_kernel,
        grid=(t // tm, n_tiles),
        in_specs=[pl.BlockSpec((tm, out_a.shape[1]), lambda i, j: (i, 0)),
                  pl.BlockSpec((w_a.shape[0], tn), lambda i, j: (0, col(i, j))),
                  pl.BlockSpec((tm, out_b.shape[1]), lambda i, j: (i, 0)),
                  pl.BlockSpec((w_b.shape[0], tn), lambda i, j: (0, col(i, j))),
                  pl.BlockSpec((tm, tn), lambda i, j: (i, ca + col(i, j))),
                  pl.BlockSpec((tm, tn), lambda i, j: (i, cb + col(i, j)))],
        out_specs=pl.BlockSpec((tm, tn), lambda i, j: (i, col(i, j))),
        out_shape=jax.ShapeDtypeStruct((t, n), BF16),
        compiler_params=_params("parallel", "arbitrary"),
        name="gated_merge",
    )(out_a, w_a, out_b, w_b, proj2d, proj2d)


def _resnorm_kernel(y_ref, x_ref, gp_ref, gn_ref, xo_ref, ho_ref):
    xn = x_ref[...] + _rms(y_ref[...].astype(F32), gp_ref[...])
    xo_ref[...] = xn
    ho_ref[...] = _rms(xn, gn_ref[...]).astype(ho_ref.dtype)


def _resnorm_last_kernel(y_ref, x_ref, gp_ref, xo_ref):
    xo_ref[...] = x_ref[...] + _rms(y_ref[...].astype(F32), gp_ref[...])


def residual_norm(y, x, g_post, g_next=None, tr=512):
    t, d = x.shape
    row = pl.BlockSpec((tr, d), lambda i: (i, 0))
    vec = pl.BlockSpec((1, d), lambda i: (0, 0))
    if g_next is None:
        return pl.pallas_call(
            _resnorm_last_kernel,
            grid=(t // tr,),
            in_specs=[row, row, vec],
            out_specs=row,
            out_shape=jax.ShapeDtypeStruct((t, d), F32),
            compiler_params=_params("parallel"),
            name="residual_norm_last",
        )(y, x, g_post.reshape(1, d))
    return pl.pallas_call(
        _resnorm_kernel,
        grid=(t // tr,),
        in_specs=[row, row, vec, vec],
        out_specs=[row, row],
        out_shape=[jax.ShapeDtypeStruct((t, d), F32), jax.ShapeDtypeStruct((t, d), BF16)],
        compiler_params=_params("parallel"),
        name="residual_norm",
    )(y, x, g_post.reshape(1, d), g_next.reshape(1, d))


def _mem_block_kernel(y_ref, x_ref, gp1_ref, gpre_ref, wq_ref, kv_ref, wo_ref, gp2_ref, gn_ref,
                      x2_ref, hn_ref, o_scr):
    x1 = x_ref[...] + _rms(y_ref[...].astype(F32), gp1_ref[...])
    h = _rms(x1, gpre_ref[...]).astype(BF16)
    q = (jnp.dot(h, wq_ref[...], preferred_element_type=F32) * Q_PRESCALE).astype(BF16)
    for hh, sl in enumerate(_heads(MEM_WIDTH)):
        slv = slice(MEM_WIDTH + hh * HEAD_DIM, MEM_WIDTH + (hh + 1) * HEAD_DIM)
        s = _qkt(q[:, sl], kv_ref[0, :, sl])
        p = jnp.exp2(s - jnp.max(s, axis=-1, keepdims=True))
        l = jnp.sum(p, axis=-1, keepdims=True)
        o = jnp.dot(p.astype(BF16), kv_ref[0, :, slv], preferred_element_type=F32) / l
        o_scr[:, sl] = o.astype(o_scr.dtype)
    y2 = jnp.dot(o_scr[...], wo_ref[...], preferred_element_type=F32)
    x2 = x1 + _rms(y2, gp2_ref[...])
    x2_ref[...] = x2
    hn_ref[...] = _rms(x2, gn_ref[...]).astype(hn_ref.dtype)


def memory_block(y, x, g_post1, g_pre, w_q, kv, w_o, g_post2, g_next, seq_len, tr=256):
    t, d = x.shape
    tiles_per_seq = seq_len // tr
    row = pl.BlockSpec((tr, d), lambda i: (i, 0))
    vec = pl.BlockSpec((1, d), lambda i: (0, 0))
    once = pl.Buffered(1)
    return pl.pallas_call(
        _mem_block_kernel,
        grid=(t // tr,),
        in_specs=[row, row, vec, vec,
                  pl.BlockSpec(w_q.shape, lambda i: (0, 0), pipeline_mode=once),
                  pl.BlockSpec((1,) + kv.shape[1:], lambda i: (i // tiles_per_seq, 0, 0)),
                  pl.BlockSpec(w_o.shape, lambda i: (0, 0), pipeline_mode=once),
                  vec, vec],
        out_specs=[row, row],
        out_shape=[jax.ShapeDtypeStruct((t, d), F32), jax.ShapeDtypeStruct((t, d), BF16)],
        scratch_shapes=[pltpu.VMEM((tr, MEM_WIDTH), BF16)],
        compiler_params=_params("arbitrary"),
        name="memory_block",
    )(y, x, g_post1.reshape(1, d), g_pre.reshape(1, d), w_q, kv, w_o,
      g_post2.reshape(1, d), g_next.reshape(1, d))


def kernel(x, mem, positions, norm_mix_pre, w_in, w_a, w_b, w_mix_out, norm_mix_post,
           lambda_q1, lambda_k1, lambda_q2, lambda_k2, diff_subln,
           norm_mem_pre, norm_mem_kv, w_mem_q, w_mem_kv, w_mem_o, norm_mem_post,
           norm_mlp_pre, w_mlp_up, w_mlp_down, norm_mlp_post):
    b, s, d = x.shape
    t = b * s
    depth = w_in.shape[0]
    m_len = mem.shape[1]
    assert all(window // (2 * dil) == Q_BLOCK // 2 for window, dil in DIL_CONFIGS)
    assert w_in.shape[2] == W_QB + 2 * DIFF_QK_WIDTH + DIFF_V_WIDTH + 2 * d
    assert positions.shape == (b, s) and mem.shape == (b, m_len, d)
    xt = x.reshape(t, d)
    memt = mem.reshape(b * m_len, d)
    h, cos, sin = rmsnorm_and_rope_tables(xt, norm_mix_pre[0], positions)
    for layer in range(depth):
        lam_init = 0.8 - 0.6 * float(np.exp(-0.3 * layer))

        w_in_b = w_in[layer]
        proj = in_projection(h, w_in_b, cos, sin)
        proj3 = proj.reshape(b, s, -1)
        tile = DIL_GROUP_WIDTH
        outs = [dilated_group(proj3.reshape(b, 1, s, -1), P_QA // tile, P_KA // tile, P_VA // tile)]
        for g, (_, dil) in enumerate(DIL_CONFIGS):
            if dil > 1:
                qkv = in_projection_dilated(h, w_in_b, cos, sin, g, dil, b)
                outs.append(dilated_group(qkv, 0, 1, 2))
        out_a = dilated_merge(outs).reshape(t, -1)
        out_b = differential_attention(proj3, lambda_q1[layer], lambda_k1[layer],
                                       lambda_q2[layer], lambda_k2[layer],
                                       diff_subln[layer], lam_init).reshape(t, -1)
        merged = gated_merge(out_a, w_a[layer], out_b, w_b[layer], proj)
        y = matmul(merged, w_mix_out[layer], BF16, 1024, 512, name="mix_out")

        mn = rmsnorm(memt, norm_mem_kv[layer])
        kv = matmul(mn, w_mem_kv[layer], BF16, b * m_len, 512, name="mem_kv")
        xt, h = memory_block(y, xt, norm_mix_post[layer], norm_mem_pre[layer],
                             w_mem_q[layer].astype(BF16), kv.reshape(b, m_len, -1),
                             w_mem_o[layer].astype(BF16), norm_mem_post[layer],
                             norm_mlp_pre[layer], s)

        u = matmul(h, w_mlp_up[layer], BF16, 2048, 512, epilogue="relu2", name="mlp_up")
        y = matmul_kgrid(u, w_mlp_down[layer], BF16, 1024, 1024, 2048, name="mlp_down")
        if layer + 1 < depth:
            xt, h = residual_norm(y, xt, norm_mlp_post[layer], norm_mix_pre[layer + 1])
        else:
            xt = residual_norm(y, xt, norm_mlp_post[layer])
    return xt.reshape(b, s, d)
```

```python
import functools

import numpy as np
import jax
import jax.numpy as jnp
from jax import lax
from jax.experimental import pallas as pl
from jax.experimental.pallas import tpu as pltpu

F32 = jnp.float32
BF16 = jnp.bfloat16

HEAD_DIM = 128
BF16_ROWS = 16
SOFTMAX_CHUNK = 256
DIL_CONFIGS = ((128, 1), (512, 4), (2048, 16))
N_DIL_GROUPS = 3
DIL_HEADS = 8
DIL_GROUP_WIDTH = DIL_HEADS * HEAD_DIM
DIL_WIDTH = N_DIL_GROUPS * DIL_GROUP_WIDTH
DIFF_HEADS = 8
DIFF_QK_WIDTH = DIFF_HEADS * 2 * HEAD_DIM
DIFF_V_DIM = 2 * HEAD_DIM
DIFF_V_WIDTH = DIFF_HEADS * DIFF_V_DIM
MEM_HEADS = 4
MEM_WIDTH = MEM_HEADS * HEAD_DIM
ROPE_THETA = 10000.0
Q_BLOCK = 128
MAX_DIL_STEP = 4 * Q_BLOCK
NORM_EPS = 1e-6
MASK_VALUE = -1e30
LOG2E = 1.4426950408889634
Q_PRESCALE = HEAD_DIM ** -0.5 * LOG2E

W_QA = 0
W_KA = W_QA + DIL_WIDTH
W_VA = W_KA + DIL_WIDTH
W_QB = W_VA + DIL_WIDTH

PROJ_TN = 512
P_QA = 0
P_KA = P_QA + DIL_GROUP_WIDTH
P_VA = P_KA + DIL_GROUP_WIDTH
P_QB = P_VA + DIL_GROUP_WIDTH
P_KB = P_QB + DIFF_QK_WIDTH
P_VB = P_KB + DIFF_QK_WIDTH
P_GA = P_VB + DIFF_V_WIDTH

V7X_VMEM_BYTES = 64 * 1024 * 1024
VMEM_LIMIT = V7X_VMEM_BYTES - 6 * 1024 * 1024
MAX_DOUBLE_BUFFERED_ROW_TILE = V7X_VMEM_BYTES // 8
INPUT_RING = 3


def _params(*sem):
    return pltpu.CompilerParams(dimension_semantics=sem, vmem_limit_bytes=VMEM_LIMIT)


def _rms(x, g):
    return x * lax.rsqrt(jnp.mean(x * x, axis=-1, keepdims=True) + NORM_EPS) * g


def _heads(width):
    return [slice(hh * HEAD_DIM, (hh + 1) * HEAD_DIM) for hh in range(width // HEAD_DIM)]


def _qkt(q, k):
    return lax.dot_general(q, k, (((1,), (1,)), ((), ())), preferred_element_type=F32)


def _ring_copy(hbm_ref, buf, sem, step):
    tr = buf.shape[1]
    slot = step % INPUT_RING
    rows = pl.ds(pl.multiple_of(step * tr, tr), tr)
    return pltpu.make_async_copy(hbm_ref.at[rows, :], buf.at[slot], sem.at[slot])


def _ring_advance(streams):
    s = pl.program_id(0)
    n = pl.num_programs(0)

    @pl.when(s == 0)
    def _():
        for k in range(INPUT_RING - 1):
            for stream in streams:
                _ring_copy(*stream, k).start()

    @pl.when(s + INPUT_RING - 1 < n)
    def _():
        for stream in streams:
            _ring_copy(*stream, s + INPUT_RING - 1).start()

    for stream in streams:
        _ring_copy(*stream, s).wait()
    return s % INPUT_RING


def _ring_scratch(tr, d, dtype):
    return [pltpu.VMEM((INPUT_RING, tr, d), dtype), pltpu.SemaphoreType.DMA((INPUT_RING,))]


def _row_operand_spec(tm, k, dtype):
    nbytes = tm * k * jnp.dtype(dtype).itemsize
    mode = pl.Buffered(1) if nbytes > MAX_DOUBLE_BUFFERED_ROW_TILE else None
    return pl.BlockSpec((tm, k), lambda i, j: (i, 0), pipeline_mode=mode)


def _serpentine(i, j, n):
    return jnp.where(i % 2 == 0, j, n - 1 - j)


def _rmsnorm_kernel(x_ref, g_ref, o_ref):
    o_ref[...] = _rms(x_ref[...], g_ref[...]).astype(o_ref.dtype)


def rmsnorm(x, g, tr=256):
    t, d = x.shape
    return pl.pallas_call(
        _rmsnorm_kernel,
        grid=(t // tr,),
        in_specs=[pl.BlockSpec((tr, d), lambda i: (i, 0)),
                  pl.BlockSpec((1, d), lambda i: (0, 0))],
        out_specs=pl.BlockSpec((tr, d), lambda i: (i, 0)),
        out_shape=jax.ShapeDtypeStruct((t, d), BF16),
        compiler_params=_params("parallel"),
        name="rmsnorm",
    )(x, g.reshape(1, d))


def _rmsnorm_rope_kernel(x_hbm, g_ref, pos_ref, inv_ref, sign_ref, o_ref, cos_ref, sin_ref,
                         x_buf, x_sem):
    slot = _ring_advance([(x_hbm, x_buf, x_sem)])
    o_ref[...] = _rms(x_buf[slot], g_ref[...]).astype(o_ref.dtype)
    ang = pos_ref[...] * inv_ref[...]
    cos_ref[...] = jnp.cos(ang)
    sin_ref[...] = jnp.sin(ang) * sign_ref[...]


def rmsnorm_and_rope_tables(x, g, positions, tr=512):
    t, d = x.shape
    half = HEAD_DIM // 2
    inv = ROPE_THETA ** (-jnp.arange(0, HEAD_DIM, 2, dtype=F32) / HEAD_DIM)
    inv2 = jnp.concatenate([inv, inv]).reshape(1, HEAD_DIM)
    sign = jnp.concatenate([-jnp.ones((half,), F32), jnp.ones((half,), F32)]).reshape(1, HEAD_DIM)
    pos = positions.astype(F32).reshape(t, 1)
    vec = pl.BlockSpec((1, HEAD_DIM), lambda i: (0, 0))
    table = pl.BlockSpec((tr, HEAD_DIM), lambda i: (i, 0))
    assert t // tr >= INPUT_RING
    return pl.pallas_call(
        _rmsnorm_rope_kernel,
        grid=(t // tr,),
        in_specs=[pl.BlockSpec(memory_space=pl.ANY),
                  pl.BlockSpec((1, d), lambda i: (0, 0)),
                  pl.BlockSpec((tr, 1), lambda i: (i, 0)), vec, vec],
        out_specs=[pl.BlockSpec((tr, d), lambda i: (i, 0)), table, table],
        out_shape=[jax.ShapeDtypeStruct((t, d), BF16),
                   jax.ShapeDtypeStruct((t, HEAD_DIM), F32),
                   jax.ShapeDtypeStruct((t, HEAD_DIM), F32)],
        scratch_shapes=_ring_scratch(tr, d, x.dtype),
        compiler_params=_params("arbitrary"),
        name="rmsnorm_rope_tables",
    )(x, g.reshape(1, d), pos, inv2, sign)


def _rope(xh, c, s):
    return xh * c + pltpu.roll(xh, HEAD_DIM // 2, 1) * s


def _inproj_kernel(h_ref, w_ref, cos_ref, sin_ref, o_ref):
    j = _serpentine(pl.program_id(0), pl.program_id(1), pl.num_programs(1))
    acc = jnp.dot(h_ref[...], w_ref[...].astype(BF16), preferred_element_type=F32)
    is_q = (j < P_KA // PROJ_TN) | ((j >= P_QB // PROJ_TN) & (j < P_KB // PROJ_TN))
    is_k = (((j >= P_KA // PROJ_TN) & (j < P_VA // PROJ_TN))
            | ((j >= P_KB // PROJ_TN) & (j < P_VB // PROJ_TN)))

    @pl.when(is_q | is_k)
    def _():
        f = jnp.where(is_q, Q_PRESCALE, 1.0).astype(F32)
        c = cos_ref[...] * f
        s = sin_ref[...] * f
        for sl in _heads(PROJ_TN):
            o_ref[:, sl] = _rope(acc[:, sl], c, s).astype(o_ref.dtype)

    @pl.when(jnp.logical_not(is_q | is_k))
    def _():
        o_ref[...] = acc.astype(o_ref.dtype)


def in_projection(h, w, cos, sin, tm=2048):
    t, k = h.shape
    tn = PROJ_TN
    n_out = w.shape[1] - 3 * (DIL_WIDTH - DIL_GROUP_WIDTH)
    per = DIL_GROUP_WIDTH // tn
    n_group0 = 3 * per
    stride_a = DIL_WIDTH // tn
    skip = W_QB // tn - n_group0

    n_tiles = n_out // tn

    def w_map(i, j):
        jc = _serpentine(i, j, n_tiles)
        return (0, jnp.where(jc < n_group0, (jc // per) * stride_a + jc % per, jc + skip))

    return pl.pallas_call(
        _inproj_kernel,
        grid=(t // tm, n_tiles),
        in_specs=[_row_operand_spec(tm, k, h.dtype),
                  pl.BlockSpec((k, tn), w_map),
                  pl.BlockSpec((tm, HEAD_DIM), lambda i, j: (i, 0)),
                  pl.BlockSpec((tm, HEAD_DIM), lambda i, j: (i, 0))],
        out_specs=pl.BlockSpec((tm, tn), lambda i, j: (i, _serpentine(i, j, n_tiles))),
        out_shape=jax.ShapeDtypeStruct((t, n_out), BF16),
        compiler_params=_params("parallel", "arbitrary"),
        name="in_projection",
    )(h, w, cos, sin)


def _inproj_dilated_kernel(h_ref, w_ref, cos_ref, sin_ref, o_ref, slab_ref, *, dil):
    j = _serpentine(pl.program_id(0), pl.program_id(1), pl.num_programs(1))
    seg = j // (DIL_GROUP_WIDTH // PROJ_TN)
    tm = h_ref.shape[0]
    acc = jnp.dot(h_ref[...], w_ref[...].astype(BF16), preferred_element_type=F32)
    f = jnp.where(seg == 0, Q_PRESCALE, 1.0).astype(F32)
    c = jnp.where(seg < 2, cos_ref[...] * f, 1.0)
    s = jnp.where(seg < 2, sin_ref[...] * f, 0.0)
    for hh, sl in enumerate(_heads(PROJ_TN)):
        slab_ref[hh] = _rope(acc[:, sl], c, s)
        for r in range(dil):
            o_ref[0, r, :, sl] = slab_ref[hh, pl.ds(r, tm // dil, stride=dil), :].astype(o_ref.dtype)


def in_projection_dilated(h, w, cos, sin, g, dil, batch, tm=1024):
    t, k = h.shape
    tn = PROJ_TN
    s_len = t // batch
    tiles_per_seq = s_len // tm
    per = DIL_GROUP_WIDTH // tn
    stride_a = DIL_WIDTH // tn
    n_tiles = 3 * per

    def w_map(i, j):
        jc = _serpentine(i, j, n_tiles)
        return (0, (jc // per) * stride_a + g * per + jc % per)

    return pl.pallas_call(
        functools.partial(_inproj_dilated_kernel, dil=dil),
        grid=(t // tm, n_tiles),
        in_specs=[pl.BlockSpec((tm, k), lambda i, j: (i, 0)),
                  pl.BlockSpec((k, tn), w_map),
                  pl.BlockSpec((tm, HEAD_DIM), lambda i, j: (i, 0)),
                  pl.BlockSpec((tm, HEAD_DIM), lambda i, j: (i, 0))],
        out_specs=pl.BlockSpec((1, dil, tm // dil, tn),
                               lambda i, j: (i // tiles_per_seq, 0, i % tiles_per_seq,
                                             _serpentine(i, j, n_tiles))),
        out_shape=jax.ShapeDtypeStruct((batch, dil, s_len // dil, 3 * DIL_GROUP_WIDTH), BF16),
        scratch_shapes=[pltpu.VMEM((tn // HEAD_DIM, tm, HEAD_DIM), F32)],
        compiler_params=_params("parallel", "arbitrary"),
        name=f"in_projection_d{dil}",
    )(h, w, cos, sin)


def _mm_kernel(a_ref, w_ref, o_ref, *, epilogue):
    acc = jnp.dot(a_ref[...], w_ref[...].astype(BF16), preferred_element_type=F32)
    if epilogue == "relu2":
        acc = jnp.square(jnp.maximum(acc, 0.0))
    o_ref[...] = acc.astype(o_ref.dtype)


def matmul(a, w, out_dtype, tm, tn, epilogue=None, name="matmul"):
    m, k = a.shape
    n = w.shape[1]
    n_tiles = n // tn
    return pl.pallas_call(
        functools.partial(_mm_kernel, epilogue=epilogue),
        grid=(m // tm, n_tiles),
        in_specs=[_row_operand_spec(tm, k, a.dtype),
                  pl.BlockSpec((k, tn), lambda i, j: (0, _serpentine(i, j, n_tiles)))],
        out_specs=pl.BlockSpec((tm, tn), lambda i, j: (i, _serpentine(i, j, n_tiles))),
        out_shape=jax.ShapeDtypeStruct((m, n), out_dtype),
        compiler_params=_params("parallel", "arbitrary"),
        name=name,
    )(a, w)


def _mm_kgrid_kernel(a_ref, w_ref, o_ref, acc_ref):
    kk = pl.program_id(2)

    @pl.when(kk == 0)
    def _():
        acc_ref[...] = jnp.zeros_like(acc_ref)

    acc_ref[...] += jnp.dot(a_ref[...], w_ref[...].astype(BF16), preferred_element_type=F32)

    @pl.when(kk == pl.num_programs(2) - 1)
    def _():
        o_ref[...] = acc_ref[...].astype(o_ref.dtype)


def matmul_kgrid(a, w, out_dtype, tm, tn, tk, name="matmul_kgrid"):
    m, k = a.shape
    n = w.shape[1]
    return pl.pallas_call(
        _mm_kgrid_kernel,
        grid=(m // tm, n // tn, k // tk),
        in_specs=[pl.BlockSpec((tm, tk), lambda i, j, kk: (i, kk)),
                  pl.BlockSpec((tk, tn), lambda i, j, kk: (kk, j))],
        out_specs=pl.BlockSpec((tm, tn), lambda i, j, kk: (i, j)),
        out_shape=jax.ShapeDtypeStruct((m, n), out_dtype),
        scratch_shapes=[pltpu.VMEM((tm, tn), F32)],
        compiler_params=_params("parallel", "parallel", "arbitrary"),
        name=name,
    )(a, w)


def _dilated_kernel(q_ref, kp_ref, kc_ref, kn_ref, vp_ref, vc_ref, vn_ref,
                    o_ref, lse_ref, *, class_len):
    i = pl.program_id(2)
    n_cls, step = q_ref.shape[1], q_ref.shape[2]
    half = Q_BLOCK // 2
    nk = 2 * Q_BLOCK
    row = lax.broadcasted_iota(jnp.int32, (Q_BLOCK, nk), 0)
    col = lax.broadcasted_iota(jnp.int32, (Q_BLOCK, nk), 1)
    lane = lax.broadcasted_iota(jnp.int32, (Q_BLOCK, HEAD_DIM), 1)
    rel = col - half - row
    in_band = (rel >= -half) & (rel <= half)

    def window(prev_ref, cur_ref, next_ref, cls, q0, sl):
        parts = []
        if q0 == 0:
            parts.append(prev_ref[0, cls, :, sl])
        lo, hi = max(q0 - half, 0), min(q0 + Q_BLOCK + half, step)
        parts.append(cur_ref[0, cls, lo:hi, sl])
        if q0 + Q_BLOCK == step:
            parts.append(next_ref[0, cls, :, sl])
        return jnp.concatenate(parts, axis=0)

    for cls in range(n_cls):
        for sb in range(step // Q_BLOCK):
            q0 = sb * Q_BLOCK
            kpos = i * step + q0 - half + col
            valid = in_band & (kpos >= 0) & (kpos < class_len)
            lse_tile = jnp.zeros((Q_BLOCK, HEAD_DIM), F32)
            for hh, sl in enumerate(_heads(DIL_GROUP_WIDTH)):
                q = q_ref[0, cls, q0:q0 + Q_BLOCK, sl]
                k = window(kp_ref, kc_ref, kn_ref, cls, q0, sl)
                v = window(vp_ref, vc_ref, vn_ref, cls, q0, sl)
                s = jnp.where(valid, _qkt(q, k), MASK_VALUE)
                m = jnp.max(s, axis=-1, keepdims=True)
                p = jnp.exp2(s - m)
                l = jnp.sum(p, axis=-1, keepdims=True)
                o = jnp.dot(p.astype(BF16), v, preferred_element_type=F32) / l
                o_ref[0, cls, q0:q0 + Q_BLOCK, sl] = o.astype(o_ref.dtype)
                lse_tile = jnp.where(lane == hh, m + jnp.log2(l), lse_tile)
            lse_ref[0, cls, q0:q0 + Q_BLOCK, :] = lse_tile


def dilated_group(qkv, cq, ck, cv):
    b, dil, class_len, _ = qkv.shape
    half = Q_BLOCK // 2
    step = min(class_len, MAX_DIL_STEP)
    n_cls = min(dil, MAX_DIL_STEP // step)
    per_step = step // half
    n_half_blocks = class_len // half
    w = DIL_GROUP_WIDTH

    def cur(c):
        return pl.BlockSpec((1, n_cls, step, w), lambda bb, r, i: (bb, r, i, c))

    def prev(c):
        return pl.BlockSpec((1, n_cls, half, w),
                            lambda bb, r, i: (bb, r, jnp.maximum(per_step * i - 1, 0), c))

    def nxt(c):
        return pl.BlockSpec((1, n_cls, half, w),
                            lambda bb, r, i: (bb, r, jnp.minimum(per_step * (i + 1), n_half_blocks - 1), c))

    return pl.pallas_call(
        functools.partial(_dilated_kernel, class_len=class_len),
        grid=(b, dil // n_cls, class_len // step),
        in_specs=[cur(cq), prev(ck), cur(ck), nxt(ck), prev(cv), cur(cv), nxt(cv)],
        out_specs=[pl.BlockSpec((1, n_cls, step, w), lambda bb, r, i: (bb, r, i, 0)),
                   pl.BlockSpec((1, n_cls, step, HEAD_DIM), lambda bb, r, i: (bb, r, i, 0))],
        out_shape=[jax.ShapeDtypeStruct((b, dil, class_len, w), BF16),
                   jax.ShapeDtypeStruct((b, dil, class_len, HEAD_DIM), F32)],
        compiler_params=_params("parallel", "parallel", "arbitrary"),
        name=f"dilated_attention_d{dil}",
    )(qkv, qkv, qkv, qkv, qkv, qkv, qkv)


def _dil_merge_kernel(o0_ref, l0_ref, o1_ref, l1_ref, o2_ref, l2_ref, out_ref,
                      so1, sl1, so2, sl2):
    for src_o, src_l, dst_o, dst_l in ((o1_ref, l1_ref, so1, sl1), (o2_ref, l2_ref, so2, sl2)):
        dil, n = src_o.shape[1], src_o.shape[2]
        for r in range(dil):
            dst_l[pl.ds(r, n, stride=dil), :] = src_l[0, r]
            for hh, sl in enumerate(_heads(DIL_GROUP_WIDTH)):
                dst_o[hh, pl.ds(r, n, stride=dil), :] = src_o[0, r, :, sl].astype(F32)
    a0, a1, a2 = l0_ref[0, 0], sl1[...], sl2[...]
    m = jnp.maximum(jnp.maximum(a0, a1), a2)
    e0, e1, e2 = jnp.exp2(a0 - m), jnp.exp2(a1 - m), jnp.exp2(a2 - m)
    inv = 1.0 / (e0 + e1 + e2)
    w0, w1, w2 = e0 * inv, e1 * inv, e2 * inv
    for hh, sl in enumerate(_heads(DIL_GROUP_WIDTH)):
        h1 = slice(hh, hh + 1)
        out = (w0[:, h1] * o0_ref[0, 0, :, sl].astype(F32) + w1[:, h1] * so1[hh]
               + w2[:, h1] * so2[hh])
        out_ref[0, :, sl] = out.astype(out_ref.dtype)


def dilated_merge(outs, tr=1024):
    (o0, l0), (o1, l1), (o2, l2) = outs
    b, _, s_len, w = o0.shape

    def spec(a):
        dil = a.shape[1]
        return pl.BlockSpec((1, dil, tr // dil, a.shape[3]), lambda bb, i: (bb, 0, i, 0))

    slab_o = pltpu.VMEM((w // HEAD_DIM, tr, HEAD_DIM), F32)
    slab_l = pltpu.VMEM((tr, HEAD_DIM), F32)
    return pl.pallas_call(
        _dil_merge_kernel,
        grid=(b, s_len // tr),
        in_specs=[spec(o0), spec(l0), spec(o1), spec(l1), spec(o2), spec(l2)],
        out_specs=pl.BlockSpec((1, tr, w), lambda bb, i: (bb, i, 0)),
        out_shape=jax.ShapeDtypeStruct((b, s_len, w), BF16),
        scratch_shapes=[slab_o, slab_l, slab_o, slab_l],
        compiler_params=_params("parallel", "arbitrary"),
        name="dilated_merge",
    )(o0, l0, o1, l1, o2, l2)


def _diff_kernel(q_ref, k_ref, v_ref, lq1, lk1, lq2, lk2, g_ref, o_ref,
                 s_scr, p_scr, l_scr, *, lam_init, n_parts):
    lam = (jnp.exp(jnp.sum(lq1[...] * lk1[...], axis=-1, keepdims=True))
           - jnp.exp(jnp.sum(lq2[...] * lk2[...], axis=-1, keepdims=True)) + lam_init)
    th = q_ref.shape[1] // n_parts
    n_keys = k_ref.shape[1]
    n_units = 2 * n_parts

    def scores(u):
        part, c = divmod(u, 2)
        sl = slice(c * HEAD_DIM, (c + 1) * HEAD_DIM)
        s_scr[u % 2] = _qkt(q_ref[0, part * th:(part + 1) * th, sl], k_ref[0, :, sl])

    def softmax(u):
        buf = u % 2
        for r in range(th // BF16_ROWS):
            rows = slice(r * BF16_ROWS, (r + 1) * BF16_ROWS)
            m = jnp.max(s_scr[buf, rows, :], axis=-1, keepdims=True)
            l = jnp.zeros((BF16_ROWS, 1), F32)
            for c0 in range(0, n_keys, SOFTMAX_CHUNK):
                cols = slice(c0, c0 + SOFTMAX_CHUNK)
                p = jnp.exp2(s_scr[buf, rows, cols] - m)
                l = l + jnp.sum(p, axis=-1, keepdims=True)
                p_scr[buf, rows, cols] = p.astype(BF16)
            l_scr[buf, rows, :] = jnp.broadcast_to(l, (BF16_ROWS, HEAD_DIM))

    def values(u):
        buf = u % 2
        return jnp.dot(p_scr[buf], v_ref[0], preferred_element_type=F32) / l_scr[buf, :, :1]

    def finish(part, o1, o2):
        o = o1 - lam * o2
        o_ref[0, part * th:(part + 1) * th, :] = (
            _rms(o, g_ref[...]) * (1.0 - lam_init)).astype(o_ref.dtype)

    scores(0)
    scores(1)
    outs = {}
    for u in range(n_units):
        softmax(u)
        if u + 2 < n_units:
            scores(u + 2)
        if u >= 1:
            outs[u - 1] = values(u - 1)
            if (u - 1) % 2 == 1:
                finish((u - 1) // 2, outs.pop(u - 2), outs.pop(u - 1))
    outs[n_units - 1] = values(n_units - 1)
    finish(n_parts - 1, outs.pop(n_units - 2), outs.pop(n_units - 1))


def differential_attention(proj, lq1, lk1, lq2, lk2, subln, lam_init, tq=2048, n_parts=4):
    b, s, _ = proj.shape
    vec = pl.BlockSpec((1, HEAD_DIM), lambda bb, h, i: (0, 0))
    cq, ck, cv = P_QB // DIFF_V_DIM, P_KB // DIFF_V_DIM, P_VB // DIFF_V_DIM
    return pl.pallas_call(
        functools.partial(_diff_kernel, lam_init=lam_init, n_parts=n_parts),
        grid=(b, DIFF_HEADS, s // tq),
        in_specs=[pl.BlockSpec((1, tq, DIFF_V_DIM), lambda bb, h, i: (bb, i, cq + h)),
                  pl.BlockSpec((1, s, DIFF_V_DIM), lambda bb, h, i: (bb, 0, ck + h)),
                  pl.BlockSpec((1, s, DIFF_V_DIM), lambda bb, h, i: (bb, 0, cv + h)),
                  vec, vec, vec, vec,
                  pl.BlockSpec((1, DIFF_V_DIM), lambda bb, h, i: (0, 0))],
        out_specs=pl.BlockSpec((1, tq, DIFF_V_DIM), lambda bb, h, i: (bb, i, h)),
        out_shape=jax.ShapeDtypeStruct((b, s, DIFF_V_WIDTH), BF16),
        scratch_shapes=[pltpu.VMEM((2, tq // n_parts, s), F32),
                        pltpu.VMEM((2, tq // n_parts, s), BF16),
                        pltpu.VMEM((2, tq // n_parts, HEAD_DIM), F32)],
        compiler_params=_params("parallel", "parallel", "arbitrary"),
        name="differential_attention",
    )(proj, proj, proj, lq1.reshape(1, -1), lk1.reshape(1, -1), lq2.reshape(1, -1),
      lk2.reshape(1, -1), subln.reshape(1, -1))


def _gate_kernel(oa_ref, wa_ref, ob_ref, wb_ref, ga_ref, gb_ref, o_ref):
    ya = jnp.dot(oa_ref[...], wa_ref[...].astype(BF16), preferred_element_type=F32)
    yb = jnp.dot(ob_ref[...], wb_ref[...].astype(BF16), preferred_element_type=F32)
    o = (jax.nn.sigmoid(ga_ref[...].astype(F32)) * ya
         + jax.nn.sigmoid(gb_ref[...].astype(F32)) * yb)
    o_ref[...] = o.astype(o_ref.dtype)


def gated_merge(out_a, w_a, out_b, w_b, proj2d, tm=1024, tn=512):
    t = out_a.shape[0]
    n = w_a.shape[1]
    ca = P_GA // tn
    cb = (P_GA + n) // tn
    n_tiles = n // tn

    def col(i, j):
        return _serpentine(i, j, n_tiles)

    return pl.pallas_call(
        _gate_kernel,
        grid=(t // tm, n_tiles),
        in_specs=[pl.BlockSpec((tm, out_a.shape[1]), lambda i, j: (i, 0)),
                  pl.BlockSpec((w_a.shape[0], tn), lambda i, j: (0, col(i, j))),
                  pl.BlockSpec((tm, out_b.shape[1]), lambda i, j: (i, 0)),
                  pl.BlockSpec((w_b.shape[0], tn), lambda i, j: (0, col(i, j))),
                  pl.BlockSpec((tm, tn), lambda i, j: (i, ca + col(i, j))),
                  pl.BlockSpec((tm, tn), lambda i, j: (i, cb + col(i, j)))],
        out_specs=pl.BlockSpec((tm, tn), lambda i, j: (i, col(i, j))),
        out_shape=jax.ShapeDtypeStruct((t, n), BF16),
        compiler_params=_params("parallel", "arbitrary"),
        name="gated_merge",
    )(out_a, w_a, out_b, w_b, proj2d, proj2d)


def _resnorm_kernel(y_ref, x_ref, gp_ref, gn_ref, xo_ref, ho_ref):
    xn = x_ref[...] + _rms(y_ref[...].astype(F32), gp_ref[...])
    xo_ref[...] = xn
    ho_ref[...] = _rms(xn, gn_ref[...]).astype(ho_ref.dtype)


def _resnorm_last_kernel(y_ref, x_ref, gp_ref, xo_ref):
    xo_ref[...] = x_ref[...] + _rms(y_ref[...].astype(F32), gp_ref[...])


def residual_norm(y, x, g_post, g_next=None, tr=512):
    t, d = x.shape
    row = pl.BlockSpec((tr, d), lambda i: (i, 0))
    vec = pl.BlockSpec((1, d), lambda i: (0, 0))
    if g_next is None:
        return pl.pallas_call(
            _resnorm_last_kernel,
            grid=(t // tr,),
            in_specs=[row, row, vec],
            out_specs=row,
            out_shape=jax.ShapeDtypeStruct((t, d), F32),
            compiler_params=_params("parallel"),
            name="residual_norm_last",
        )(y, x, g_post.reshape(1, d))
    return pl.pallas_call(
        _resnorm_kernel,
        grid=(t // tr,),
        in_specs=[row, row, vec, vec],
        out_specs=[row, row],
        out_shape=[jax.ShapeDtypeStruct((t, d), F32), jax.ShapeDtypeStruct((t, d), BF16)],
        compiler_params=_params("parallel"),
        name="residual_norm",
    )(y, x, g_post.reshape(1, d), g_next.reshape(1, d))


def _mem_block_kernel(y_ref, x_ref, gp1_ref, gpre_ref, wq_ref, kv_ref, wo_ref, gp2_ref, gn_ref,
                      x2_ref, hn_ref, o_scr):
    x1 = x_ref[...] + _rms(y_ref[...].astype(F32), gp1_ref[...])
    h = _rms(x1, gpre_ref[...]).astype(BF16)
    q = (jnp.dot(h, wq_ref[...], preferred_element_type=F32) * Q_PRESCALE).astype(BF16)
    for hh, sl in enumerate(_heads(MEM_WIDTH)):
        slv = slice(MEM_WIDTH + hh * HEAD_DIM, MEM_WIDTH + (hh + 1) * HEAD_DIM)
        s = _qkt(q[:, sl], kv_ref[0, :, sl])
        p = jnp.exp2(s - jnp.max(s, axis=-1, keepdims=True))
        l = jnp.sum(p, axis=-1, keepdims=True)
        o = jnp.dot(p.astype(BF16), kv_ref[0, :, slv], preferred_element_type=F32) / l
        o_scr[:, sl] = o.astype(o_scr.dtype)
    y2 = jnp.dot(o_scr[...], wo_ref[...], preferred_element_type=F32)
    x2 = x1 + _rms(y2, gp2_ref[...])
    x2_ref[...] = x2
    hn_ref[...] = _rms(x2, gn_ref[...]).astype(hn_ref.dtype)


def memory_block(y, x, g_post1, g_pre, w_q, kv, w_o, g_post2, g_next, seq_len, tr=256):
    t, d = x.shape
    tiles_per_seq = seq_len // tr
    row = pl.BlockSpec((tr, d), lambda i: (i, 0))
    vec = pl.BlockSpec((1, d), lambda i: (0, 0))
    once = pl.Buffered(1)
    return pl.pallas_call(
        _mem_block_kernel,
        grid=(t // tr,),
        in_specs=[row, row, vec, vec,
                  pl.BlockSpec(w_q.shape, lambda i: (0, 0), pipeline_mode=once),
                  pl.BlockSpec((1,) + kv.shape[1:], lambda i: (i // tiles_per_seq, 0, 0)),
                  pl.BlockSpec(w_o.shape, lambda i: (0, 0), pipeline_mode=once),
                  vec, vec],
        out_specs=[row, row],
        out_shape=[jax.ShapeDtypeStruct((t, d), F32), jax.ShapeDtypeStruct((t, d), BF16)],
        scratch_shapes=[pltpu.VMEM((tr, MEM_WIDTH), BF16)],
        compiler_params=_params("arbitrary"),
        name="memory_block",
    )(y, x, g_post1.reshape(1, d), g_pre.reshape(1, d), w_q, kv, w_o,
      g_post2.reshape(1, d), g_next.reshape(1, d))


def kernel(x, mem, positions, norm_mix_pre, w_in, w_a, w_b, w_mix_out, norm_mix_post,
           lambda_q1, lambda_k1, lambda_q2, lambda_k2, diff_subln,
           norm_mem_pre, norm_mem_kv, w_mem_q, w_mem_kv, w_mem_o, norm_mem_post,
           norm_mlp_pre, w_mlp_up, w_mlp_down, norm_mlp_post):
    b, s, d = x.shape
    t = b * s
    depth = w_in.shape[0]
    m_len = mem.shape[1]
    assert all(window // (2 * dil) == Q_BLOCK // 2 for window, dil in DIL_CONFIGS)
    assert w_in.shape[2] == W_QB + 2 * DIFF_QK_WIDTH + DIFF_V_WIDTH + 2 * d
    assert positions.shape == (b, s) and mem.shape == (b, m_len, d)
    xt = x.reshape(t, d)
    memt = mem.reshape(b * m_len, d)
    h, cos, sin = rmsnorm_and_rope_tables(xt, norm_mix_pre[0], positions)
    for layer in range(depth):
        lam_init = 0.8 - 0.6 * float(np.exp(-0.3 * layer))

        w_in_b = w_in[layer]
        proj = in_projection(h, w_in_b, cos, sin)
        proj3 = proj.reshape(b, s, -1)
        tile = DIL_GROUP_WIDTH
        outs = [dilated_group(proj3.reshape(b, 1, s, -1), P_QA // tile, P_KA // tile, P_VA // tile)]
        for g, (_, dil) in enumerate(DIL_CONFIGS):
            if dil > 1:
                qkv = in_projection_dilated(h, w_in_b, cos, sin, g, dil, b)
                outs.append(dilated_group(qkv, 0, 1, 2))
        out_a = dilated_merge(outs).reshape(t, -1)
        out_b = differential_attention(proj3, lambda_q1[layer], lambda_k1[layer],
                                       lambda_q2[layer], lambda_k2[layer],
                                       diff_subln[layer], lam_init).reshape(t, -1)
        merged = gated_merge(out_a, w_a[layer], out_b, w_b[layer], proj)
        y = matmul(merged, w_mix_out[layer], BF16, 1024, 512, name="mix_out")

        mn = rmsnorm(memt, norm_mem_kv[layer])
        kv = matmul(mn, w_mem_kv[layer], BF16, b * m_len, 512, name="mem_kv")
        xt, h = memory_block(y, xt, norm_mix_post[layer], norm_mem_pre[layer],
                             w_mem_q[layer].astype(BF16), kv.reshape(b, m_len, -1),
                             w_mem_o[layer].astype(BF16), norm_mem_post[layer],
                             norm_mlp_pre[layer], s)

        u = matmul(h, w_mlp_up[layer], BF16, 2048, 512, epilogue="relu2", name="mlp_up")
        y = matmul_kgrid(u, w_mlp_down[layer], BF16, 1024, 1024, 2048, name="mlp_down")
        if layer + 1 < depth:
            xt, h = residual_norm(y, xt, norm_mlp_post[layer], norm_mix_pre[layer + 1])
        else:
            xt = residual_norm(y, xt, norm_mlp_post[layer])
    return xt.reshape(b, s, d)
```

```python
import functools

import numpy as np
import jax
import jax.numpy as jnp
from jax import lax
from jax.experimental import pallas as pl
from jax.experimental.pallas import tpu as pltpu

F32 = jnp.float32
BF16 = jnp.bfloat16

HEAD_DIM = 128
BF16_ROWS = 16
NORM_CHUNK = 512
SOFTMAX_CHUNK = 256
DIL_CONFIGS = ((128, 1), (512, 4), (2048, 16))
N_DIL_GROUPS = 3
DIL_HEADS = 8
DIL_GROUP_WIDTH = DIL_HEADS * HEAD_DIM
DIL_WIDTH = N_DIL_GROUPS * DIL_GROUP_WIDTH
DIFF_HEADS = 8
DIFF_QK_WIDTH = DIFF_HEADS * 2 * HEAD_DIM
DIFF_V_DIM = 2 * HEAD_DIM
DIFF_V_WIDTH = DIFF_HEADS * DIFF_V_DIM
MEM_HEADS = 4
MEM_WIDTH = MEM_HEADS * HEAD_DIM
ROPE_THETA = 10000.0
Q_BLOCK = 128
MAX_DIL_STEP = 4 * Q_BLOCK
NORM_EPS = 1e-6
MASK_VALUE = -1e30
LOG2E = 1.4426950408889634
Q_PRESCALE = HEAD_DIM ** -0.5 * LOG2E

W_QA = 0
W_KA = W_QA + DIL_WIDTH
W_VA = W_KA + DIL_WIDTH
W_QB = W_VA + DIL_WIDTH

PROJ_TN = 512
P_QA = 0
P_KA = P_QA + DIL_GROUP_WIDTH
P_VA = P_KA + DIL_GROUP_WIDTH
P_QB = P_VA + DIL_GROUP_WIDTH
P_KB = P_QB + DIFF_QK_WIDTH
P_VB = P_KB + DIFF_QK_WIDTH
P_GA = P_VB + DIFF_V_WIDTH

V7X_VMEM_BYTES = 64 * 1024 * 1024
VMEM_LIMIT = V7X_VMEM_BYTES - 6 * 1024 * 1024
MAX_DOUBLE_BUFFERED_ROW_TILE = V7X_VMEM_BYTES // 8
INPUT_RING = 3


def _params(*sem):
    return pltpu.CompilerParams(dimension_semantics=sem, vmem_limit_bytes=VMEM_LIMIT)


def _rms(x, g):
    return x * lax.rsqrt(jnp.mean(x * x, axis=-1, keepdims=True) + NORM_EPS) * g


def _heads(width):
    return [slice(hh * HEAD_DIM, (hh + 1) * HEAD_DIM) for hh in range(width // HEAD_DIM)]


def _qkt(q, k):
    return lax.dot_general(q, k, (((1,), (1,)), ((), ())), preferred_element_type=F32)


def _ring_copy(hbm_ref, buf, sem, step):
    tr = buf.shape[1]
    slot = step % INPUT_RING
    rows = pl.ds(pl.multiple_of(step * tr, tr), tr)
    return pltpu.make_async_copy(hbm_ref.at[rows, :], buf.at[slot], sem.at[slot])


def _ring_advance(streams):
    s = pl.program_id(0)
    n = pl.num_programs(0)

    @pl.when(s == 0)
    def _():
        for k in range(INPUT_RING - 1):
            for stream in streams:
                _ring_copy(*stream, k).start()

    @pl.when(s + INPUT_RING - 1 < n)
    def _():
        for stream in streams:
            _ring_copy(*stream, s + INPUT_RING - 1).start()

    for stream in streams:
        _ring_copy(*stream, s).wait()
    return s % INPUT_RING


def _ring_scratch(tr, d, dtype):
    return [pltpu.VMEM((INPUT_RING, tr, d), dtype), pltpu.SemaphoreType.DMA((INPUT_RING,))]


def _row_operand_spec(tm, k, dtype):
    nbytes = tm * k * jnp.dtype(dtype).itemsize
    mode = pl.Buffered(1) if nbytes > MAX_DOUBLE_BUFFERED_ROW_TILE else None
    return pl.BlockSpec((tm, k), lambda i, j: (i, 0), pipeline_mode=mode)


def _serpentine(i, j, n):
    return jnp.where(i % 2 == 0, j, n - 1 - j)


def _rmsnorm_kernel(x_ref, g_ref, o_ref):
    o_ref[...] = _rms(x_ref[...], g_ref[...]).astype(o_ref.dtype)


def rmsnorm(x, g, tr=256):
    t, d = x.shape
    return pl.pallas_call(
        _rmsnorm_kernel,
        grid=(t // tr,),
        in_specs=[pl.BlockSpec((tr, d), lambda i: (i, 0)),
                  pl.BlockSpec((1, d), lambda i: (0, 0))],
        out_specs=pl.BlockSpec((tr, d), lambda i: (i, 0)),
        out_shape=jax.ShapeDtypeStruct((t, d), BF16),
        compiler_params=_params("parallel"),
        name="rmsnorm",
    )(x, g.reshape(1, d))


def _rmsnorm_rope_kernel(x_hbm, g_ref, pos_ref, inv_ref, sign_ref, o_ref, cos_ref, sin_ref,
                         x_buf, x_sem):
    slot = _ring_advance([(x_hbm, x_buf, x_sem)])
    o_ref[...] = _rms(x_buf[slot], g_ref[...]).astype(o_ref.dtype)
    ang = pos_ref[...] * inv_ref[...]
    cos_ref[...] = jnp.cos(ang)
    sin_ref[...] = jnp.sin(ang) * sign_ref[...]


def rmsnorm_and_rope_tables(x, g, positions, tr=512):
    t, d = x.shape
    half = HEAD_DIM // 2
    inv = ROPE_THETA ** (-jnp.arange(0, HEAD_DIM, 2, dtype=F32) / HEAD_DIM)
    inv2 = jnp.concatenate([inv, inv]).reshape(1, HEAD_DIM)
    sign = jnp.concatenate([-jnp.ones((half,), F32), jnp.ones((half,), F32)]).reshape(1, HEAD_DIM)
    pos = positions.astype(F32).reshape(t, 1)
    vec = pl.BlockSpec((1, HEAD_DIM), lambda i: (0, 0))
    table = pl.BlockSpec((tr, HEAD_DIM), lambda i: (i, 0))
    assert t // tr >= INPUT_RING
    return pl.pallas_call(
        _rmsnorm_rope_kernel,
        grid=(t // tr,),
        in_specs=[pl.BlockSpec(memory_space=pl.ANY),
                  pl.BlockSpec((1, d), lambda i: (0, 0)),
                  pl.BlockSpec((tr, 1), lambda i: (i, 0)), vec, vec],
        out_specs=[pl.BlockSpec((tr, d), lambda i: (i, 0)), table, table],
        out_shape=[jax.ShapeDtypeStruct((t, d), BF16),
                   jax.ShapeDtypeStruct((t, HEAD_DIM), F32),
                   jax.ShapeDtypeStruct((t, HEAD_DIM), F32)],
        scratch_shapes=_ring_scratch(tr, d, x.dtype),
        compiler_params=_params("arbitrary"),
        name="rmsnorm_rope_tables",
    )(x, g.reshape(1, d), pos, inv2, sign)


def _rope(xh, c, s):
    return xh * c + pltpu.roll(xh, HEAD_DIM // 2, 1) * s


def _inproj_kernel(h_ref, w_ref, cos_ref, sin_ref, o_ref):
    j = _serpentine(pl.program_id(0), pl.program_id(1), pl.num_programs(1))
    acc = jnp.dot(h_ref[...], w_ref[...].astype(BF16), preferred_element_type=F32)
    is_q = (j < P_KA // PROJ_TN) | ((j >= P_QB // PROJ_TN) & (j < P_KB // PROJ_TN))
    is_k = (((j >= P_KA // PROJ_TN) & (j < P_VA // PROJ_TN))
            | ((j >= P_KB // PROJ_TN) & (j < P_VB // PROJ_TN)))

    @pl.when(is_q | is_k)
    def _():
        f = jnp.where(is_q, Q_PRESCALE, 1.0).astype(F32)
        c = cos_ref[...] * f
        s = sin_ref[...] * f
        for sl in _heads(PROJ_TN):
            o_ref[:, sl] = _rope(acc[:, sl], c, s).astype(o_ref.dtype)

    @pl.when(jnp.logical_not(is_q | is_k))
    def _():
        o_ref[...] = acc.astype(o_ref.dtype)


def in_projection(h, w, cos, sin, tm=2048):
    t, k = h.shape
    tn = PROJ_TN
    n_out = w.shape[1] - 3 * (DIL_WIDTH - DIL_GROUP_WIDTH)
    per = DIL_GROUP_WIDTH // tn
    n_group0 = 3 * per
    stride_a = DIL_WIDTH // tn
    skip = W_QB // tn - n_group0

    n_tiles = n_out // tn

    def w_map(i, j):
        jc = _serpentine(i, j, n_tiles)
        return (0, jnp.where(jc < n_group0, (jc // per) * stride_a + jc % per, jc + skip))

    return pl.pallas_call(
        _inproj_kernel,
        grid=(t // tm, n_tiles),
        in_specs=[_row_operand_spec(tm, k, h.dtype),
                  pl.BlockSpec((k, tn), w_map),
                  pl.BlockSpec((tm, HEAD_DIM), lambda i, j: (i, 0)),
                  pl.BlockSpec((tm, HEAD_DIM), lambda i, j: (i, 0))],
        out_specs=pl.BlockSpec((tm, tn), lambda i, j: (i, _serpentine(i, j, n_tiles))),
        out_shape=jax.ShapeDtypeStruct((t, n_out), BF16),
        compiler_params=_params("parallel", "arbitrary"),
        name="in_projection",
    )(h, w, cos, sin)


def _inproj_dilated_kernel(h_ref, w_ref, cos_ref, sin_ref, o_ref, slab_ref, *, dil):
    j = _serpentine(pl.program_id(0), pl.program_id(1), pl.num_programs(1))
    seg = j // (DIL_GROUP_WIDTH // PROJ_TN)
    tm = h_ref.shape[0]
    acc = jnp.dot(h_ref[...], w_ref[...].astype(BF16), preferred_element_type=F32)
    f = jnp.where(seg == 0, Q_PRESCALE, 1.0).astype(F32)
    c = jnp.where(seg < 2, cos_ref[...] * f, 1.0)
    s = jnp.where(seg < 2, sin_ref[...] * f, 0.0)
    for hh, sl in enumerate(_heads(PROJ_TN)):
        slab_ref[hh] = _rope(acc[:, sl], c, s)
        for r in range(dil):
            o_ref[0, r, :, sl] = slab_ref[hh, pl.ds(r, tm // dil, stride=dil), :].astype(o_ref.dtype)


def in_projection_dilated(h, w, cos, sin, g, dil, batch, tm=1024):
    t, k = h.shape
    tn = PROJ_TN
    s_len = t // batch
    tiles_per_seq = s_len // tm
    per = DIL_GROUP_WIDTH // tn
    stride_a = DIL_WIDTH // tn
    n_tiles = 3 * per

    def w_map(i, j):
        jc = _serpentine(i, j, n_tiles)
        return (0, (jc // per) * stride_a + g * per + jc % per)

    return pl.pallas_call(
        functools.partial(_inproj_dilated_kernel, dil=dil),
        grid=(t // tm, n_tiles),
        in_specs=[pl.BlockSpec((tm, k), lambda i, j: (i, 0)),
                  pl.BlockSpec((k, tn), w_map),
                  pl.BlockSpec((tm, HEAD_DIM), lambda i, j: (i, 0)),
                  pl.BlockSpec((tm, HEAD_DIM), lambda i, j: (i, 0))],
        out_specs=pl.BlockSpec((1, dil, tm // dil, tn),
                               lambda i, j: (i // tiles_per_seq, 0, i % tiles_per_seq,
                                             _serpentine(i, j, n_tiles))),
        out_shape=jax.ShapeDtypeStruct((batch, dil, s_len // dil, 3 * DIL_GROUP_WIDTH), BF16),
        scratch_shapes=[pltpu.VMEM((tn // HEAD_DIM, tm, HEAD_DIM), F32)],
        compiler_params=_params("parallel", "arbitrary"),
        name=f"in_projection_d{dil}",
    )(h, w, cos, sin)


def _mm_kernel(a_ref, w_ref, o_ref, *, epilogue):
    acc = jnp.dot(a_ref[...], w_ref[...].astype(BF16), preferred_element_type=F32)
    if epilogue == "relu2":
        acc = jnp.square(jnp.maximum(acc, 0.0))
    o_ref[...] = acc.astype(o_ref.dtype)


def matmul(a, w, out_dtype, tm, tn, epilogue=None, name="matmul"):
    m, k = a.shape
    n = w.shape[1]
    n_tiles = n // tn
    return pl.pallas_call(
        functools.partial(_mm_kernel, epilogue=epilogue),
        grid=(m // tm, n_tiles),
        in_specs=[_row_operand_spec(tm, k, a.dtype),
                  pl.BlockSpec((k, tn), lambda i, j: (0, _serpentine(i, j, n_tiles)))],
        out_specs=pl.BlockSpec((tm, tn), lambda i, j: (i, _serpentine(i, j, n_tiles))),
        out_shape=jax.ShapeDtypeStruct((m, n), out_dtype),
        compiler_params=_params("parallel", "arbitrary"),
        name=name,
    )(a, w)


def _mm_kgrid_kernel(a_ref, w_ref, o_ref, acc_ref):
    kk = pl.program_id(2)

    @pl.when(kk == 0)
    def _():
        acc_ref[...] = jnp.zeros_like(acc_ref)

    acc_ref[...] += jnp.dot(a_ref[...], w_ref[...].astype(BF16), preferred_element_type=F32)

    @pl.when(kk == pl.num_programs(2) - 1)
    def _():
        o_ref[...] = acc_ref[...].astype(o_ref.dtype)


def matmul_kgrid(a, w, out_dtype, tm, tn, tk, name="matmul_kgrid"):
    m, k = a.shape
    n = w.shape[1]
    return pl.pallas_call(
        _mm_kgrid_kernel,
        grid=(m // tm, n // tn, k // tk),
        in_specs=[pl.BlockSpec((tm, tk), lambda i, j, kk: (i, kk)),
                  pl.BlockSpec((tk, tn), lambda i, j, kk: (kk, j))],
        out_specs=pl.BlockSpec((tm, tn), lambda i, j, kk: (i, j)),
        out_shape=jax.ShapeDtypeStruct((m, n), out_dtype),
        scratch_shapes=[pltpu.VMEM((tm, tn), F32)],
        compiler_params=_params("parallel", "parallel", "arbitrary"),
        name=name,
    )(a, w)


def _dilated_kernel(q_ref, kp_ref, kc_ref, kn_ref, vp_ref, vc_ref, vn_ref,
                    o_ref, lse_ref, *, class_len):
    i = pl.program_id(2)
    n_cls, step = q_ref.shape[1], q_ref.shape[2]
    half = Q_BLOCK // 2
    nk = 2 * Q_BLOCK
    row = lax.broadcasted_iota(jnp.int32, (Q_BLOCK, nk), 0)
    col = lax.broadcasted_iota(jnp.int32, (Q_BLOCK, nk), 1)
    lane = lax.broadcasted_iota(jnp.int32, (Q_BLOCK, HEAD_DIM), 1)
    rel = col - half - row
    in_band = (rel >= -half) & (rel <= half)

    def window(prev_ref, cur_ref, next_ref, cls, q0, sl):
        parts = []
        if q0 == 0:
            parts.append(prev_ref[0, cls, :, sl])
        lo, hi = max(q0 - half, 0), min(q0 + Q_BLOCK + half, step)
        parts.append(cur_ref[0, cls, lo:hi, sl])
        if q0 + Q_BLOCK == step:
            parts.append(next_ref[0, cls, :, sl])
        return jnp.concatenate(parts, axis=0)

    for cls in range(n_cls):
        for sb in range(step // Q_BLOCK):
            q0 = sb * Q_BLOCK
            kpos = i * step + q0 - half + col
            valid = in_band & (kpos >= 0) & (kpos < class_len)
            lse_tile = jnp.zeros((Q_BLOCK, HEAD_DIM), F32)
            for hh, sl in enumerate(_heads(DIL_GROUP_WIDTH)):
                q = q_ref[0, cls, q0:q0 + Q_BLOCK, sl]
                k = window(kp_ref, kc_ref, kn_ref, cls, q0, sl)
                v = window(vp_ref, vc_ref, vn_ref, cls, q0, sl)
                s = jnp.where(valid, _qkt(q, k), MASK_VALUE)
                m = jnp.max(s, axis=-1, keepdims=True)
                p = jnp.exp2(s - m)
                l = jnp.sum(p, axis=-1, keepdims=True)
                o = jnp.dot(p.astype(BF16), v, preferred_element_type=F32) / l
                o_ref[0, cls, q0:q0 + Q_BLOCK, sl] = o.astype(o_ref.dtype)
                lse_tile = jnp.where(lane == hh, m + jnp.log2(l), lse_tile)
            lse_ref[0, cls, q0:q0 + Q_BLOCK, :] = lse_tile


def dilated_group(qkv, cq, ck, cv):
    b, dil, class_len, _ = qkv.shape
    half = Q_BLOCK // 2
    step = min(class_len, MAX_DIL_STEP)
    n_cls = min(dil, MAX_DIL_STEP // step)
    per_step = step // half
    n_half_blocks = class_len // half
    w = DIL_GROUP_WIDTH

    def cur(c):
        return pl.BlockSpec((1, n_cls, step, w), lambda bb, r, i: (bb, r, i, c))

    def prev(c):
        return pl.BlockSpec((1, n_cls, half, w),
                            lambda bb, r, i: (bb, r, jnp.maximum(per_step * i - 1, 0), c))

    def nxt(c):
        return pl.BlockSpec((1, n_cls, half, w),
                            lambda bb, r, i: (bb, r, jnp.minimum(per_step * (i + 1), n_half_blocks - 1), c))

    return pl.pallas_call(
        functools.partial(_dilated_kernel, class_len=class_len),
        grid=(b, dil // n_cls, class_len // step),
        in_specs=[cur(cq), prev(ck), cur(ck), nxt(ck), prev(cv), cur(cv), nxt(cv)],
        out_specs=[pl.BlockSpec((1, n_cls, step, w), lambda bb, r, i: (bb, r, i, 0)),
                   pl.BlockSpec((1, n_cls, step, HEAD_DIM), lambda bb, r, i: (bb, r, i, 0))],
        out_shape=[jax.ShapeDtypeStruct((b, dil, class_len, w), BF16),
                   jax.ShapeDtypeStruct((b, dil, class_len, HEAD_DIM), F32)],
        compiler_params=_params("parallel", "parallel", "arbitrary"),
        name=f"dilated_attention_d{dil}",
    )(qkv, qkv, qkv, qkv, qkv, qkv, qkv)


def _dil_merge_kernel(o0_ref, l0_ref, o1_ref, l1_ref, o2_ref, l2_ref, out_ref,
                      so1, sl1, so2, sl2):
    for src_o, src_l, dst_o, dst_l in ((o1_ref, l1_ref, so1, sl1), (o2_ref, l2_ref, so2, sl2)):
        dil, n = src_o.shape[1], src_o.shape[2]
        for r in range(dil):
            dst_l[pl.ds(r, n, stride=dil), :] = src_l[0, r]
            for hh, sl in enumerate(_heads(DIL_GROUP_WIDTH)):
                dst_o[hh, pl.ds(r, n, stride=dil), :] = src_o[0, r, :, sl].astype(F32)
    a0, a1, a2 = l0_ref[0, 0], sl1[...], sl2[...]
    m = jnp.maximum(jnp.maximum(a0, a1), a2)
    e0, e1, e2 = jnp.exp2(a0 - m), jnp.exp2(a1 - m), jnp.exp2(a2 - m)
    inv = 1.0 / (e0 + e1 + e2)
    w0, w1, w2 = e0 * inv, e1 * inv, e2 * inv
    for hh, sl in enumerate(_heads(DIL_GROUP_WIDTH)):
        h1 = slice(hh, hh + 1)
        out = (w0[:, h1] * o0_ref[0, 0, :, sl].astype(F32) + w1[:, h1] * so1[hh]
               + w2[:, h1] * so2[hh])
        out_ref[0, :, sl] = out.astype(out_ref.dtype)


def dilated_merge(outs, tr=1024):
    (o0, l0), (o1, l1), (o2, l2) = outs
    b, _, s_len, w = o0.shape

    def spec(a):
        dil = a.shape[1]
        return pl.BlockSpec((1, dil, tr // dil, a.shape[3]), lambda bb, i: (bb, 0, i, 0))

    slab_o = pltpu.VMEM((w // HEAD_DIM, tr, HEAD_DIM), F32)
    slab_l = pltpu.VMEM((tr, HEAD_DIM), F32)
    return pl.pallas_call(
        _dil_merge_kernel,
        grid=(b, s_len // tr),
        in_specs=[spec(o0), spec(l0), spec(o1), spec(l1), spec(o2), spec(l2)],
        out_specs=pl.BlockSpec((1, tr, w), lambda bb, i: (bb, i, 0)),
        out_shape=jax.ShapeDtypeStruct((b, s_len, w), BF16),
        scratch_shapes=[slab_o, slab_l, slab_o, slab_l],
        compiler_params=_params("parallel", "arbitrary"),
        name="dilated_merge",
    )(o0, l0, o1, l1, o2, l2)


def _diff_kernel(q_ref, k_ref, v_ref, lq1, lk1, lq2, lk2, g_ref, o_ref,
                 s_scr, p_scr, l_scr, *, lam_init, n_parts):
    lam = (jnp.exp(jnp.sum(lq1[...] * lk1[...], axis=-1, keepdims=True))
           - jnp.exp(jnp.sum(lq2[...] * lk2[...], axis=-1, keepdims=True)) + lam_init)
    th = q_ref.shape[1] // n_parts
    n_keys = k_ref.shape[1]
    n_units = 2 * n_parts

    def scores(u):
        part, c = divmod(u, 2)
        sl = slice(c * HEAD_DIM, (c + 1) * HEAD_DIM)
        s_scr[u % 2] = _qkt(q_ref[0, part * th:(part + 1) * th, sl], k_ref[0, :, sl])

    def softmax(u):
        buf = u % 2
        for r in range(th // BF16_ROWS):
            rows = slice(r * BF16_ROWS, (r + 1) * BF16_ROWS)
            m = jnp.max(s_scr[buf, rows, :], axis=-1, keepdims=True)
            l = jnp.zeros((BF16_ROWS, 1), F32)
            for c0 in range(0, n_keys, SOFTMAX_CHUNK):
                cols = slice(c0, c0 + SOFTMAX_CHUNK)
                p = jnp.exp2(s_scr[buf, rows, cols] - m)
                l = l + jnp.sum(p, axis=-1, keepdims=True)
                p_scr[buf, rows, cols] = p.astype(BF16)
            l_scr[buf, rows, :] = jnp.broadcast_to(l, (BF16_ROWS, HEAD_DIM))

    def values(u):
        buf = u % 2
        return jnp.dot(p_scr[buf], v_ref[0], preferred_element_type=F32) / l_scr[buf, :, :1]

    def finish(part, o1, o2):
        o = o1 - lam * o2
        o_ref[0, part * th:(part + 1) * th, :] = (
            _rms(o, g_ref[...]) * (1.0 - lam_init)).astype(o_ref.dtype)

    scores(0)
    scores(1)
    outs = {}
    for u in range(n_units):
        softmax(u)
        if u + 2 < n_units:
            scores(u + 2)
        if u >= 1:
            outs[u - 1] = values(u - 1)
            if (u - 1) % 2 == 1:
                finish((u - 1) // 2, outs.pop(u - 2), outs.pop(u - 1))
    outs[n_units - 1] = values(n_units - 1)
    finish(n_parts - 1, outs.pop(n_units - 2), outs.pop(n_units - 1))


def differential_attention(proj, lq1, lk1, lq2, lk2, subln, lam_init, tq=2048, n_parts=4):
    b, s, _ = proj.shape
    vec = pl.BlockSpec((1, HEAD_DIM), lambda bb, h, i: (0, 0))
    cq, ck, cv = P_QB // DIFF_V_DIM, P_KB // DIFF_V_DIM, P_VB // DIFF_V_DIM
    return pl.pallas_call(
        functools.partial(_diff_kernel, lam_init=lam_init, n_parts=n_parts),
        grid=(b, DIFF_HEADS, s // tq),
        in_specs=[pl.BlockSpec((1, tq, DIFF_V_DIM), lambda bb, h, i: (bb, i, cq + h)),
                  pl.BlockSpec((1, s, DIFF_V_DIM), lambda bb, h, i: (bb, 0, ck + h)),
                  pl.BlockSpec((1, s, DIFF_V_DIM), lambda bb, h, i: (bb, 0, cv + h)),
                  vec, vec, vec, vec,
                  pl.BlockSpec((1, DIFF_V_DIM), lambda bb, h, i: (0, 0))],
        out_specs=pl.BlockSpec((1, tq, DIFF_V_DIM), lambda bb, h, i: (bb, i, h)),
        out_shape=jax.ShapeDtypeStruct((b, s, DIFF_V_WIDTH), BF16),
        scratch_shapes=[pltpu.VMEM((2, tq // n_parts, s), F32),
                        pltpu.VMEM((2, tq // n_parts, s), BF16),
                        pltpu.VMEM((2, tq // n_parts, HEAD_DIM), F32)],
        compiler_params=_params("parallel", "parallel", "arbitrary"),
        name="differential_attention",
    )(proj, proj, proj, lq1.reshape(1, -1), lk1.reshape(1, -1), lq2.reshape(1, -1),
      lk2.reshape(1, -1), subln.reshape(1, -1))


def _gate_kernel(oa_ref, wa_ref, ob_ref, wb_ref, ga_ref, gb_ref, o_ref):
    ya = jnp.dot(oa_ref[...], wa_ref[...].astype(BF16), preferred_element_type=F32)
    yb = jnp.dot(ob_ref[...], wb_ref[...].astype(BF16), preferred_element_type=F32)
    o = (jax.nn.sigmoid(ga_ref[...].astype(F32)) * ya
         + jax.nn.sigmoid(gb_ref[...].astype(F32)) * yb)
    o_ref[...] = o.astype(o_ref.dtype)


def gated_merge(out_a, w_a, out_b, w_b, proj2d, tm=1024, tn=512):
    t = out_a.shape[0]
    n = w_a.shape[1]
    ca = P_GA // tn
    cb = (P_GA + n) // tn
    n_tiles = n // tn

    def col(i, j):
        return _serpentine(i, j, n_tiles)

    return pl.pallas_call(
        _gate_kernel,
        grid=(t // tm, n_tiles),
        in_specs=[pl.BlockSpec((tm, out_a.shape[1]), lambda i, j: (i, 0)),
                  pl.BlockSpec((w_a.shape[0], tn), lambda i, j: (0, col(i, j))),
                  pl.BlockSpec((tm, out_b.shape[1]), lambda i, j: (i, 0)),
                  pl.BlockSpec((w_b.shape[0], tn), lambda i, j: (0, col(i, j))),
                  pl.BlockSpec((tm, tn), lambda i, j: (i, ca + col(i, j))),
                  pl.BlockSpec((tm, tn), lambda i, j: (i, cb + col(i, j)))],
        out_specs=pl.BlockSpec((tm, tn), lambda i, j: (i, col(i, j))),
        out_shape=jax.ShapeDtypeStruct((t, n), BF16),
        compiler_params=_params("parallel", "arbitrary"),
        name="gated_merge",
    )(out_a, w_a, out_b, w_b, proj2d, proj2d)


def _resnorm_kernel(y_ref, x_ref, gp_ref, gn_ref, xo_ref, ho_ref):
    xn = x_ref[...] + _rms(y_ref[...].astype(F32), gp_ref[...])
    xo_ref[...] = xn
    ho_ref[...] = _rms(xn, gn_ref[...]).astype(ho_ref.dtype)


def _resnorm_last_kernel(y_ref, x_ref, gp_ref, xo_ref):
    xo_ref[...] = x_ref[...] + _rms(y_ref[...].astype(F32), gp_ref[...])


def residual_norm(y, x, g_post, g_next=None, tr=512):
    t, d = x.shape
    row = pl.BlockSpec((tr, d), lambda i: (i, 0))
    vec = pl.BlockSpec((1, d), lambda i: (0, 0))
    if g_next is None:
        return pl.pallas_call(
            _resnorm_last_kernel,
            grid=(t // tr,),
            in_specs=[row, row, vec],
            out_specs=row,
            out_shape=jax.ShapeDtypeStruct((t, d), F32),
            compiler_params=_params("parallel"),
            name="residual_norm_last",
        )(y, x, g_post.reshape(1, d))
    return pl.pallas_call(
        _resnorm_kernel,
        grid=(t // tr,),
        in_specs=[row, row, vec, vec],
        out_specs=[row, row],
        out_shape=[jax.ShapeDtypeStruct((t, d), F32), jax.ShapeDtypeStruct((t, d), BF16)],
        compiler_params=_params("parallel"),
        name="residual_norm",
    )(y, x, g_post.reshape(1, d), g_next.reshape(1, d))


def _row_ssq(ref_rows, n_cols):
    acc = jnp.zeros((ref_rows(0).shape[0], 1), F32)
    for c0 in range(0, n_cols, NORM_CHUNK):
        v = ref_rows(c0)
        acc = acc + jnp.sum(v * v, axis=-1, keepdims=True)
    return acc


def _mem_block_kernel(y_ref, x_ref, gp1_ref, gpre_ref, wq_ref, kv_ref, wo_ref, gp2_ref, gn_ref,
                      x2_ref, hn_ref, x1_scr, h_scr, y2_scr, o_scr):
    tr, d = x_ref.shape

    def cols(c0):
        return slice(c0, c0 + NORM_CHUNK)

    for r0 in range(0, tr, BF16_ROWS):
        rows = slice(r0, r0 + BF16_ROWS)
        r1 = lax.rsqrt(_row_ssq(lambda c0: y_ref[rows, cols(c0)].astype(F32), d) / d + NORM_EPS)
        ssq = jnp.zeros((BF16_ROWS, 1), F32)
        for c0 in range(0, d, NORM_CHUNK):
            x1 = x_ref[rows, cols(c0)] + y_ref[rows, cols(c0)].astype(F32) * r1 * gp1_ref[:, cols(c0)]
            x1_scr[rows, cols(c0)] = x1
            ssq = ssq + jnp.sum(x1 * x1, axis=-1, keepdims=True)
        r2 = lax.rsqrt(ssq / d + NORM_EPS)
        for c0 in range(0, d, NORM_CHUNK):
            h_scr[rows, cols(c0)] = (x1_scr[rows, cols(c0)] * r2 * gpre_ref[:, cols(c0)]).astype(BF16)

    q = (jnp.dot(h_scr[...], wq_ref[...], preferred_element_type=F32) * Q_PRESCALE).astype(BF16)
    for hh, sl in enumerate(_heads(MEM_WIDTH)):
        slv = slice(MEM_WIDTH + hh * HEAD_DIM, MEM_WIDTH + (hh + 1) * HEAD_DIM)
        s = _qkt(q[:, sl], kv_ref[0, :, sl])
        p = jnp.exp2(s - jnp.max(s, axis=-1, keepdims=True))
        l = jnp.sum(p, axis=-1, keepdims=True)
        o = jnp.dot(p.astype(BF16), kv_ref[0, :, slv], preferred_element_type=F32) / l
        o_scr[:, sl] = o.astype(o_scr.dtype)
    y2_scr[...] = jnp.dot(o_scr[...], wo_ref[...], preferred_element_type=F32)

    for r0 in range(0, tr, BF16_ROWS):
        rows = slice(r0, r0 + BF16_ROWS)
        r3 = lax.rsqrt(_row_ssq(lambda c0: y2_scr[rows, cols(c0)], d) / d + NORM_EPS)
        ssq = jnp.zeros((BF16_ROWS, 1), F32)
        for c0 in range(0, d, NORM_CHUNK):
            x2 = x1_scr[rows, cols(c0)] + y2_scr[rows, cols(c0)] * r3 * gp2_ref[:, cols(c0)]
            x2_ref[rows, cols(c0)] = x2
            ssq = ssq + jnp.sum(x2 * x2, axis=-1, keepdims=True)
        r4 = lax.rsqrt(ssq / d + NORM_EPS)
        for c0 in range(0, d, NORM_CHUNK):
            hn_ref[rows, cols(c0)] = (x2_ref[rows, cols(c0)] * r4 * gn_ref[:, cols(c0)]).astype(hn_ref.dtype)


def memory_block(y, x, g_post1, g_pre, w_q, kv, w_o, g_post2, g_next, seq_len, tr=256):
    t, d = x.shape
    tiles_per_seq = seq_len // tr
    row = pl.BlockSpec((tr, d), lambda i: (i, 0))
    vec = pl.BlockSpec((1, d), lambda i: (0, 0))
    once = pl.Buffered(1)
    return pl.pallas_call(
        _mem_block_kernel,
        grid=(t // tr,),
        in_specs=[row, row, vec, vec,
                  pl.BlockSpec(w_q.shape, lambda i: (0, 0), pipeline_mode=once),
                  pl.BlockSpec((1,) + kv.shape[1:], lambda i: (i // tiles_per_seq, 0, 0)),
                  pl.BlockSpec(w_o.shape, lambda i: (0, 0), pipeline_mode=once),
                  vec, vec],
        out_specs=[row, row],
        out_shape=[jax.ShapeDtypeStruct((t, d), F32), jax.ShapeDtypeStruct((t, d), BF16)],
        scratch_shapes=[pltpu.VMEM((tr, d), F32), pltpu.VMEM((tr, d), BF16),
                        pltpu.VMEM((tr, d), F32), pltpu.VMEM((tr, MEM_WIDTH), BF16)],
        compiler_params=_params("arbitrary"),
        name="memory_block",
    )(y, x, g_post1.reshape(1, d), g_pre.reshape(1, d), w_q, kv, w_o,
      g_post2.reshape(1, d), g_next.reshape(1, d))


def kernel(x, mem, positions, norm_mix_pre, w_in, w_a, w_b, w_mix_out, norm_mix_post,
           lambda_q1, lambda_k1, lambda_q2, lambda_k2, diff_subln,
           norm_mem_pre, norm_mem_kv, w_mem_q, w_mem_kv, w_mem_o, norm_mem_post,
           norm_mlp_pre, w_mlp_up, w_mlp_down, norm_mlp_post):
    b, s, d = x.shape
    t = b * s
    depth = w_in.shape[0]
    m_len = mem.shape[1]
    assert all(window // (2 * dil) == Q_BLOCK // 2 for window, dil in DIL_CONFIGS)
    assert w_in.shape[2] == W_QB + 2 * DIFF_QK_WIDTH + DIFF_V_WIDTH + 2 * d
    assert positions.shape == (b, s) and mem.shape == (b, m_len, d)
    xt = x.reshape(t, d)
    memt = mem.reshape(b * m_len, d)
    h, cos, sin = rmsnorm_and_rope_tables(xt, norm_mix_pre[0], positions)
    for layer in range(depth):
        lam_init = 0.8 - 0.6 * float(np.exp(-0.3 * layer))

        w_in_b = w_in[layer]
        proj = in_projection(h, w_in_b, cos, sin)
        proj3 = proj.reshape(b, s, -1)
        tile = DIL_GROUP_WIDTH
        outs = [dilated_group(proj3.reshape(b, 1, s, -1), P_QA // tile, P_KA // tile, P_VA // tile)]
        for g, (_, dil) in enumerate(DIL_CONFIGS):
            if dil > 1:
                qkv = in_projection_dilated(h, w_in_b, cos, sin, g, dil, b)
                outs.append(dilated_group(qkv, 0, 1, 2))
        out_a = dilated_merge(outs).reshape(t, -1)
        out_b = differential_attention(proj3, lambda_q1[layer], lambda_k1[layer],
                                       lambda_q2[layer], lambda_k2[layer],
                                       diff_subln[layer], lam_init).reshape(t, -1)
        merged = gated_merge(out_a, w_a[layer], out_b, w_b[layer], proj)
        y = matmul(merged, w_mix_out[layer], BF16, 1024, 512, name="mix_out")

        mn = rmsnorm(memt, norm_mem_kv[layer])
        kv = matmul(mn, w_mem_kv[layer], BF16, b * m_len, 512, name="mem_kv")
        xt, h = memory_block(y, xt, norm_mix_post[layer], norm_mem_pre[layer],
                             w_mem_q[layer].astype(BF16), kv.reshape(b, m_len, -1),
                             w_mem_o[layer].astype(BF16), norm_mem_post[layer],
                             norm_mlp_pre[layer], s)

        u = matmul(h, w_mlp_up[layer], BF16, 2048, 512, epilogue="relu2", name="mlp_up")
        y = matmul_kgrid(u, w_mlp_down[layer], BF16, 1024, 1024, 2048, name="mlp_down")
        if layer + 1 < depth:
            xt, h = residual_norm(y, xt, norm_mlp_post[layer], norm_mix_pre[layer + 1])
        else:
            xt = residual_norm(y, xt, norm_mlp_post[layer])
    return xt.reshape(b, s, d)
```
